```python
import math
import jax
import jax.numpy as jnp
from jax import lax
import numpy as np

D_MODEL = 2048
BATCH = 2
SEQ = 8192
DEPTH = 1
DEC_BATCH = 32
DEC_SEQ = 4
PAST_LEN = 16384
PAGE_SIZE = 128

GDN_HEADS = 8
GDN_HD = 128
GDN_WIDTH = GDN_HEADS * GDN_HD
CONV_W = 4
GDN_CHUNK = 64
DIFF_HEADS = 8
DIFF_HD = 64
DIFF_VD = 2 * DIFF_HD
DIFF_QK_WIDTH = DIFF_HEADS * 2 * DIFF_HD
DIFF_V_WIDTH = DIFF_HEADS * DIFF_VD
ROT_DIM = DIFF_HD // 4
ROPE_THETA = 500000.0
LAMBDA_INIT = 0.2
Q_BLOCK = 128
NEG_INIT = -1e30
MEM_LEN = 256
MEM_HEADS = 4
MEM_HD = 128
MEM_WIDTH = MEM_HEADS * MEM_HD
N_GROUPS = 4
EXPERTS_PER_GROUP = 8
N_EXPERTS = N_GROUPS * EXPERTS_PER_GROUP
EXPERT_FF = 512
TOP_K_INNER = 2
RMS_EPS = 1e-6
L2_EPS = 1e-6
IN_SPLITS = (GDN_WIDTH, GDN_WIDTH, GDN_WIDTH, GDN_WIDTH, GDN_HEADS, GDN_HEADS,
             DIFF_QK_WIDTH, DIFF_QK_WIDTH, DIFF_V_WIDTH, D_MODEL, D_MODEL)
IN_WIDTH = 4 * GDN_WIDTH + 2 * GDN_HEADS + 2 * DIFF_QK_WIDTH + DIFF_V_WIDTH + 2 * D_MODEL

kernel_name = 'hybrid_gdn_diffattn_hmoe_step'


def _rmsnorm(x, g):
    xf = x.astype(jnp.float32)
    y = xf * lax.rsqrt(jnp.mean(xf * xf, axis=-1, keepdims=True) + RMS_EPS)
    return (y * g.astype(jnp.float32)).astype(x.dtype)


def _l2norm(x):
    xf = x.astype(jnp.float32)
    return xf * lax.rsqrt(jnp.sum(xf * xf, axis=-1, keepdims=True) + L2_EPS)


def _split_columns(u):
    offsets, o = [], 0
    for w in IN_SPLITS[:-1]:
        o += w
        offsets.append(o)
    return jnp.split(u, offsets, axis=-1)


def _short_conv(u, conv_state, w_conv):
    T = u.shape[1]
    full = jnp.concatenate([conv_state.astype(u.dtype), u], axis=1)
    out = full[:, 0:T] * w_conv[0]
    for i in range(1, CONV_W):
        out = out + full[:, i:i + T] * w_conv[i]
    return jax.nn.silu(out), full[:, T:]


def _rope_partial(x, pos):
    half = ROT_DIM // 2
    inv_freq = ROPE_THETA ** (-jnp.arange(0, ROT_DIM, 2, dtype=jnp.float32) / ROT_DIM)
    ang = pos.astype(jnp.float32)[:, None] * inv_freq[None, :]
    cos = jnp.cos(ang)[:, None, None, :]
    sin = jnp.sin(ang)[:, None, None, :]
    xf = x.astype(jnp.float32)
    x1, x2 = xf[..., :half], xf[..., half:ROT_DIM]
    rot = jnp.concatenate([x1 * cos - x2 * sin, x2 * cos + x1 * sin], axis=-1).astype(x.dtype)
    return jnp.concatenate([rot, x[..., ROT_DIM:]], axis=-1)


def _gated_delta_rule(q, k, v, g, beta, s0):
    B, T, H, _ = q.shape
    Dv = v.shape[-1]
    C = min(GDN_CHUNK, T)
    pad = (-T) % C
    if pad:
        pw = ((0, 0), (0, pad), (0, 0), (0, 0))
        q, k, v = jnp.pad(q, pw), jnp.pad(k, pw), jnp.pad(v, pw)
        g, beta = jnp.pad(g, pw[:3]), jnp.pad(beta, pw[:3])
    N = (T + pad) // C

    def chunks(a):
        a = a.astype(jnp.float32).reshape((B, N, C, H) + a.shape[3:])
        return jnp.moveaxis(a, (1, 3), (0, 2))

    q, k, v, g, beta = chunks(q), chunks(k), chunks(v), chunks(g), chunks(beta)
    gc = jnp.cumsum(g, axis=-1)
    causal = jnp.tril(jnp.ones((C, C), bool))
    strict = jnp.tril(jnp.ones((C, C), bool), -1)
    diff = gc[..., :, None] - gc[..., None, :]
    decay = jnp.where(causal, jnp.exp(jnp.where(causal, diff, 0.0)), 0.0)
    kb = k * beta[..., None]
    a_kk = jnp.where(strict, jnp.einsum('nbhcd,nbhsd->nbhcs', kb, k) * decay, 0.0)
    rhs = jnp.concatenate([v * beta[..., None], kb * jnp.exp(gc)[..., None]], axis=-1)
    sol = lax.linalg.triangular_solve(a_kk, rhs, left_side=True, lower=True, unit_diagonal=True)
    u, w = sol[..., :Dv], sol[..., Dv:]
    a_qk = jnp.where(causal, jnp.einsum('nbhcd,nbhsd->nbhcs', q, k) * decay, 0.0)

    def step(S, xs):
        q_i, k_i, u_i, w_i, gc_i, a_i = xs
        v_new = u_i - jnp.einsum('bhcd,bhde->bhce', w_i, S)
        o = (jnp.einsum('bhcd,bhde->bhce', q_i * jnp.exp(gc_i)[..., None], S)
             + jnp.einsum('bhcs,bhse->bhce', a_i, v_new))
        g_last = gc_i[..., -1:]
        S = (S * jnp.exp(g_last)[..., None]
             + jnp.einsum('bhcd,bhce->bhde', k_i * jnp.exp(g_last - gc_i)[..., None], v_new))
        return S, o

    S, o = lax.scan(step, s0.astype(jnp.float32), (q, k, u, w, gc, a_qk))
    o = jnp.moveaxis(o, (0, 2), (1, 3)).reshape(B, N * C, H, Dv)[:, :T]
    return o, S.astype(s0.dtype)


def _diff_attn_prompt(q, k, v, lam):
    B, T, H = q.shape[:3]
    QB = min(Q_BLOCK, T)
    nq = T // QB
    scale = DIFF_HD ** -0.5
    qb = jnp.moveaxis(q.reshape(B, nq, QB, H, 2, DIFF_HD), 1, 0)
    kpos = jnp.arange(T)

    def block(args):
        q_i, i = args
        s = jnp.einsum('bqhcd,bkhcd->bhcqk', q_i, k).astype(jnp.float32) * scale
        qpos = i * QB + jnp.arange(QB)
        s = jnp.where(kpos[None, :] <= qpos[:, None], s, -jnp.inf)
        p = jax.nn.softmax(s, axis=-1)
        a = p[:, :, 0] - lam * p[:, :, 1]
        return jnp.einsum('bhqk,bkhe->bqhe', a.astype(v.dtype), v)

    o = lax.map(block, (qb, jnp.arange(nq)))
    return jnp.moveaxis(o, 0, 1).reshape(B, T, H, v.shape[-1])


def _diff_attn_sample(q, k, v, lam, cache_k, cache_v, page_table):
    Bd, T, H = q.shape[:3]
    Dv = v.shape[-1]
    qf = q.astype(jnp.float32) * DIFF_HD ** -0.5

    def update(carry, kb, vb, mask):
        m, l, acc = carry
        s = jnp.einsum('bthcd,bshcd->bhcts', qf, kb.astype(jnp.float32))
        if mask is not None:
            s = jnp.where(mask, s, -jnp.inf)
        m_new = jnp.maximum(m, jnp.max(s, axis=-1))
        alpha = jnp.exp(m - m_new)
        p = jnp.exp(s - m_new[..., None])
        l = l * alpha + jnp.sum(p, axis=-1)
        acc = acc * alpha[..., None] + jnp.einsum('bhcts,bshe->bhcte', p, vb.astype(jnp.float32))
        return (m_new, l, acc)

    init = (jnp.full((Bd, H, 2, T), NEG_INIT, jnp.float32),
            jnp.zeros((Bd, H, 2, T), jnp.float32),
            jnp.zeros((Bd, H, 2, T, Dv), jnp.float32))

    def page_step(carry, phys):
        kb = cache_k[phys].reshape(Bd, -1, H, 2, DIFF_HD)
        return update(carry, kb, cache_v[phys], None), None

    carry, _ = lax.scan(page_step, init, page_table.T)
    causal = jnp.tril(jnp.ones((T, T), bool))
    m, l, acc = update(carry, k, v, causal)
    o = acc / l[..., None]
    o = o[:, :, 0] - lam * o[:, :, 1]
    return jnp.transpose(o, (0, 2, 1, 3)).astype(v.dtype)


def _mem_kv(mem, norm_mem, w_mk, w_mv):
    B, M, _ = mem.shape
    h = _rmsnorm(mem, norm_mem)
    return ((h @ w_mk).reshape(B, M, MEM_HEADS, MEM_HD),
            (h @ w_mv).reshape(B, M, MEM_HEADS, MEM_HD))


def _mem_cross(h, mem_k, mem_v, w_mq, w_mo):
    B, T, _ = h.shape
    q = (h @ w_mq).reshape(B, T, MEM_HEADS, MEM_HD)
    s = jnp.einsum('bthd,bmhd->bhtm', q, mem_k).astype(jnp.float32) * MEM_HD ** -0.5
    p = jax.nn.softmax(s, axis=-1).astype(h.dtype)
    o = jnp.einsum('bhtm,bmhd->bthd', p, mem_v).reshape(B, T, MEM_WIDTH)
    return o @ w_mo


def _hier_moe(h, w_rg, b_rg, w_re, b_re, w_gate, w_up, w_down):
    B, T, D = h.shape
    x = h.reshape(B * T, D)
    g_logits = (x @ w_rg).astype(jnp.float32) + b_rg.astype(jnp.float32)
    g_prob = jax.nn.softmax(g_logits, axis=-1)
    g_w, g_idx = lax.top_k(g_prob, 1)
    e_logits = ((x @ w_re).astype(jnp.float32) + b_re.astype(jnp.float32)).reshape(-1, N_GROUPS, EXPERTS_PER_GROUP)
    e_logits = jnp.einsum('ng,nge->ne', jax.nn.one_hot(g_idx[:, 0], N_GROUPS, dtype=jnp.float32), e_logits)
    e_prob = jax.nn.softmax(e_logits, axis=-1)
    e_w, e_idx = lax.top_k(e_prob, TOP_K_INNER)
    e_w = e_w / jnp.sum(e_w, axis=-1, keepdims=True)
    expert_id = g_idx * EXPERTS_PER_GROUP + e_idx
    gates = jnp.sum(jax.nn.one_hot(expert_id, N_EXPERTS, dtype=jnp.float32) * (g_w * e_w)[..., None], axis=1)

    def expert_step(acc, ws):
        wg, wu, wd, ge = ws
        y = (jax.nn.silu(x @ wg) * (x @ wu)) @ wd
        return acc + ge[:, None] * y, None

    out, _ = lax.scan(expert_step, jnp.zeros_like(x), (w_gate, w_up, w_down, gates.T.astype(x.dtype)))
    return out.reshape(B, T, D)


def _layer(x, pos, conv_state, delta_state, attend, mem_k, mem_v, p):
    B, T, _ = x.shape
    h = _rmsnorm(x, p['norm_mix'])
    (a_q, a_k, a_v, a_z, a_beta, a_dt, d_q, d_k, d_v, gate_a, gate_b) = _split_columns(h @ p['w_in'])
    qkv, new_conv = _short_conv(jnp.concatenate([a_q, a_k, a_v], axis=-1), conv_state, p['w_conv'])
    q, k, v = jnp.split(qkv, 3, axis=-1)
    q = _l2norm(q.reshape(B, T, GDN_HEADS, GDN_HD)) * GDN_HD ** -0.5
    k = _l2norm(k.reshape(B, T, GDN_HEADS, GDN_HD))
    v = v.reshape(B, T, GDN_HEADS, GDN_HD)
    beta = jax.nn.sigmoid(a_beta.astype(jnp.float32))
    g = -jnp.exp(p['a_log'].astype(jnp.float32)) * jax.nn.softplus(
        a_dt.astype(jnp.float32) + p['dt_bias'].astype(jnp.float32))
    o_a, new_delta = _gated_delta_rule(q, k, v, g, beta, delta_state)
    o_a = _rmsnorm(o_a.astype(x.dtype), p['gdn_norm']) * jax.nn.silu(a_z.reshape(B, T, GDN_HEADS, GDN_HD))
    br_a = o_a.reshape(B, T, GDN_WIDTH) @ p['w_branch_a']
    d_q = _rope_partial(d_q.reshape(B, T, DIFF_HEADS, 2, DIFF_HD), pos)
    d_k = _rope_partial(d_k.reshape(B, T, DIFF_HEADS, 2, DIFF_HD), pos)
    d_v = d_v.reshape(B, T, DIFF_HEADS, DIFF_VD)
    f32 = jnp.float32
    lam = (jnp.exp(jnp.sum(p['lambda_q1'].astype(f32) * p['lambda_k1'].astype(f32)))
           - jnp.exp(jnp.sum(p['lambda_q2'].astype(f32) * p['lambda_k2'].astype(f32))) + LAMBDA_INIT)
    o_b = attend(d_q, d_k, d_v, lam).astype(x.dtype)
    o_b = _rmsnorm(o_b, p['diff_norm']) * (1.0 - LAMBDA_INIT)
    br_b = o_b.reshape(B, T, DIFF_V_WIDTH) @ p['w_branch_b']
    mixed = jax.nn.sigmoid(gate_a) * br_a + jax.nn.sigmoid(gate_b) * br_b
    x = x + mixed @ p['w_out']
    x = x + _mem_cross(_rmsnorm(x, p['norm_cross']), mem_k, mem_v, p['w_mq'], p['w_mo'])
    x = x + _hier_moe(_rmsnorm(x, p['norm_ffn']), p['w_router_group'], p['b_router_group'],
                      p['w_router_expert'], p['b_router_expert'], p['w_gate'], p['w_up'], p['w_down'])
    return x, new_conv, new_delta, d_k.reshape(B, T, DIFF_HEADS, 2 * DIFF_HD), d_v


def setup_inputs(seed: int = 0) -> dict:
    key = jax.random.key(seed)
    ks = jax.random.split(key, 40)
    nrm = jax.random.normal
    n_pages = PAST_LEN // PAGE_SIZE
    n_pool = (5 * DEC_BATCH * n_pages) // 4
    page_table = jax.random.permutation(ks[0], n_pool)[:DEC_BATCH * n_pages].reshape(DEC_BATCH, n_pages).astype(jnp.int32)
    dt = jnp.exp(jax.random.uniform(ks[1], (GDN_HEADS,)) * (math.log(0.1) - math.log(0.001)) + math.log(0.001))
    gain = lambda kk, n: 1.0 + 0.01 * nrm(kk, (n,), jnp.float32)
    return {
        'x_prompt': nrm(ks[2], (BATCH, SEQ, D_MODEL), jnp.float32),
        'x_sample': nrm(ks[3], (DEC_BATCH, DEC_SEQ, D_MODEL), jnp.float32),
        'cache_k': nrm(ks[4], (n_pool, PAGE_SIZE, DIFF_HEADS, 2 * DIFF_HD), jnp.float32),
        'cache_v': nrm(ks[5], (n_pool, PAGE_SIZE, DIFF_HEADS, DIFF_VD), jnp.float32),
        'cache_mem_k': nrm(ks[6], (DEC_BATCH, MEM_LEN, MEM_HEADS, MEM_HD), jnp.float32),
        'cache_mem_v': nrm(ks[7], (DEC_BATCH, MEM_LEN, MEM_HEADS, MEM_HD), jnp.float32),
        'state_delta': 0.1 * nrm(ks[8], (DEC_BATCH, GDN_HEADS, GDN_HD, GDN_HD), jnp.float32),
        'state_conv': nrm(ks[9], (DEC_BATCH, CONV_W - 1, 3 * GDN_WIDTH), jnp.float32),
        'page_table': page_table,
        'mem_prompt': nrm(ks[10], (BATCH, MEM_LEN, D_MODEL), jnp.float32),
        'norm_mix': gain(ks[11], D_MODEL),
        'w_in': nrm(ks[12], (D_MODEL, IN_WIDTH), jnp.float32) * D_MODEL ** -0.5,
        'w_conv': nrm(ks[13], (CONV_W, 3 * GDN_WIDTH), jnp.float32) * CONV_W ** -0.5,
        'a_log': jnp.log(jax.random.uniform(ks[14], (GDN_HEADS,), minval=1.0, maxval=16.0)),
        'dt_bias': dt + jnp.log(-jnp.expm1(-dt)),
        'gdn_norm': gain(ks[15], GDN_HD),
        'lambda_q1': 0.1 * nrm(ks[16], (DIFF_HD,), jnp.float32),
        'lambda_k1': 0.1 * nrm(ks[17], (DIFF_HD,), jnp.float32),
        'lambda_q2': 0.1 * nrm(ks[18], (DIFF_HD,), jnp.float32),
        'lambda_k2': 0.1 * nrm(ks[19], (DIFF_HD,), jnp.float32),
        'diff_norm': gain(ks[20], DIFF_VD),
        'w_branch_a': nrm(ks[21], (GDN_WIDTH, D_MODEL), jnp.float32) * GDN_WIDTH ** -0.5,
        'w_branch_b': nrm(ks[22], (DIFF_V_WIDTH, D_MODEL), jnp.float32) * DIFF_V_WIDTH ** -0.5,
        'w_out': nrm(ks[23], (D_MODEL, D_MODEL), jnp.float32) * D_MODEL ** -0.5,
        'norm_cross': gain(ks[24], D_MODEL),
        'norm_mem': gain(ks[25], D_MODEL),
        'w_mq': nrm(ks[26], (D_MODEL, MEM_WIDTH), jnp.float32) * D_MODEL ** -0.5,
        'w_mk': nrm(ks[27], (D_MODEL, MEM_WIDTH), jnp.float32) * D_MODEL ** -0.5,
        'w_mv': nrm(ks[28], (D_MODEL, MEM_WIDTH), jnp.float32) * D_MODEL ** -0.5,
        'w_mo': nrm(ks[29], (MEM_WIDTH, D_MODEL), jnp.float32) * MEM_WIDTH ** -0.5,
        'norm_ffn': gain(ks[30], D_MODEL),
        'w_router_group': nrm(ks[31], (D_MODEL, N_GROUPS), jnp.float32) * D_MODEL ** -0.5,
        'b_router_group': 0.01 * nrm(ks[32], (N_GROUPS,), jnp.float32),
        'w_router_expert': nrm(ks[33], (D_MODEL, N_EXPERTS), jnp.float32) * D_MODEL ** -0.5,
        'b_router_expert': 0.01 * nrm(ks[34], (N_EXPERTS,), jnp.float32),
        'w_gate': nrm(ks[35], (N_EXPERTS, D_MODEL, EXPERT_FF), jnp.float32) * D_MODEL ** -0.5,
        'w_up': nrm(ks[36], (N_EXPERTS, D_MODEL, EXPERT_FF), jnp.float32) * D_MODEL ** -0.5,
        'w_down': nrm(ks[37], (N_EXPERTS, EXPERT_FF, D_MODEL), jnp.float32) * EXPERT_FF ** -0.5,
        'norm_final': gain(ks[38], D_MODEL),
    }


def reference(x_prompt, x_sample, cache_k, cache_v, cache_mem_k, cache_mem_v, state_delta, state_conv,
              page_table, mem_prompt, norm_mix, w_in, w_conv, a_log, dt_bias, gdn_norm,
              lambda_q1, lambda_k1, lambda_q2, lambda_k2, diff_norm, w_branch_a, w_branch_b, w_out,
              norm_cross, norm_mem, w_mq, w_mk, w_mv, w_mo, norm_ffn, w_router_group, b_router_group,
              w_router_expert, b_router_expert, w_gate, w_up, w_down, norm_final):
    p = dict(norm_mix=norm_mix, w_in=w_in, w_conv=w_conv, a_log=a_log, dt_bias=dt_bias, gdn_norm=gdn_norm,
             lambda_q1=lambda_q1, lambda_k1=lambda_k1, lambda_q2=lambda_q2, lambda_k2=lambda_k2,
             diff_norm=diff_norm, w_branch_a=w_branch_a, w_branch_b=w_branch_b, w_out=w_out,
             norm_cross=norm_cross, w_mq=w_mq, w_mo=w_mo, norm_ffn=norm_ffn,
             w_router_group=w_router_group, b_router_group=b_router_group,
             w_router_expert=w_router_expert, b_router_expert=b_router_expert,
             w_gate=w_gate, w_up=w_up, w_down=w_down)
    Bp, Tp, _ = x_prompt.shape
    Ts = x_sample.shape[1]
    pos_p = jnp.arange(Tp)
    pos_s = PAST_LEN + jnp.arange(Ts)
    conv0 = jnp.zeros((Bp, CONV_W - 1, 3 * GDN_WIDTH), x_prompt.dtype)
    delta0 = jnp.zeros((Bp, GDN_HEADS, GDN_HD, GDN_HD), x_prompt.dtype)
    mem_k_p, mem_v_p = _mem_kv(mem_prompt, norm_mem, w_mk, w_mv)
    sample_attend = lambda q, k, v, lam: _diff_attn_sample(q, k, v, lam, cache_k, cache_v, page_table)

    hp, hs = x_prompt, x_sample
    for _ in range(DEPTH):
        hp, conv_p, delta_p, k_p, v_p = _layer(hp, pos_p, conv0, delta0, _diff_attn_prompt, mem_k_p, mem_v_p, p)
        hs, conv_s, delta_s, k_s, v_s = _layer(hs, pos_s, state_conv, state_delta, sample_attend,
                                               cache_mem_k, cache_mem_v, p)
    y_prompt = _rmsnorm(hp, norm_final)
    y_sample = _rmsnorm(hs, norm_final)
    return (y_prompt, y_sample, k_p, v_p, mem_k_p, mem_v_p, delta_p, conv_p, k_s, v_s, delta_s, conv_s)
```

```python
import functools
import math

import jax
import jax.numpy as jnp
from jax import lax
from jax.experimental import pallas as pl
from jax.experimental.pallas import tpu as pltpu

F32 = jnp.float32
BF16 = jnp.bfloat16

GDN_HEADS = 8
GDN_HD = 128
GDN_WIDTH = GDN_HEADS * GDN_HD
CONV_W = 4
GDN_CHUNK = 64
DIFF_HEADS = 8
DIFF_HD = 64
DIFF_VD = 2 * DIFF_HD
DIFF_WIDTH = DIFF_HEADS * DIFF_VD
ROT_DIM = DIFF_HD // 4
ROPE_THETA = 500000.0
LAMBDA_INIT = 0.2
PAGE_SIZE = 128
MEM_HEADS = 4
MEM_HD = 128
MEM_WIDTH = MEM_HEADS * MEM_HD
N_GROUPS = 4
EXPERTS_PER_GROUP = 8
N_EXPERTS = N_GROUPS * EXPERTS_PER_GROUP
TOP_K_INNER = 2
RMS_EPS = 1e-6
L2_EPS = 1e-6
NEG_INIT = -1e30

LANES = 128
SUBLANES = 8
VMEM_LIMIT = 48 * 1024 * 1024

_HI = lax.Precision.HIGHEST


def _cparams(*sem):
    return pltpu.CompilerParams(dimension_semantics=sem, vmem_limit_bytes=VMEM_LIMIT)


def _tile(n, pref):
    if n <= pref:
        return n
    t = pref
    while n % t:
        t //= 2
    return t


def _dot(a, b):
    return jnp.dot(a.astype(BF16), b.astype(BF16), preferred_element_type=F32)


def _dot_nt(a, b):
    return lax.dot_general(a.astype(BF16), b.astype(BF16), (((1,), (1,)), ((), ())),
                           preferred_element_type=F32)


def _dot_tn(a, b):
    return lax.dot_general(a.astype(BF16), b.astype(BF16), (((0,), (0,)), ((), ())),
                           preferred_element_type=F32)


def _dot_hi(a, b):
    return jnp.dot(a, b, preferred_element_type=F32, precision=_HI)


def _rmsnorm_body(x_ref, g_ref, o_ref):
    x = x_ref[...]
    ms = jnp.mean(x * x, axis=-1, keepdims=True)
    o_ref[...] = (x * lax.rsqrt(ms + RMS_EPS) * g_ref[...]).astype(o_ref.dtype)


def rmsnorm_rows(x, g, out_dtype=BF16):
    m, d = x.shape
    tm = _tile(m, 512)
    return pl.pallas_call(
        _rmsnorm_body,
        grid=(m // tm,),
        in_specs=[pl.BlockSpec((tm, d), lambda i: (i, 0)), pl.BlockSpec((1, d), lambda i: (0, 0))],
        out_specs=pl.BlockSpec((tm, d), lambda i: (i, 0)),
        out_shape=jax.ShapeDtypeStruct((m, d), out_dtype),
        compiler_params=_cparams("parallel"),
        name="rmsnorm_rows",
    )(x, g.reshape(1, d).astype(F32))


def _mm_body(a_ref, w_ref, o_ref):
    o_ref[...] = jnp.dot(a_ref[...], w_ref[...], preferred_element_type=F32).astype(o_ref.dtype)


def _mm_res_body(a_ref, w_ref, r_ref, o_ref):
    o_ref[...] = r_ref[...] + jnp.dot(a_ref[...], w_ref[...], preferred_element_type=F32)


def _mm_rope_body(a_ref, w_ref, c_ref, s1_ref, s2_ref, o_ref, *, scale, reps):
    o = jnp.dot(a_ref[...], w_ref[...], preferred_element_type=F32)
    tn = o.shape[1]
    c = jnp.tile(c_ref[...], (1, reps))
    s1 = jnp.tile(s1_ref[...], (1, reps))
    s2 = jnp.tile(s2_ref[...], (1, reps))
    r = o * c + pltpu.roll(o, tn - ROT_DIM // 2, 1) * s1 + pltpu.roll(o, ROT_DIM // 2, 1) * s2
    if scale != 1.0:
        r = r * scale
    o_ref[...] = r.astype(o_ref.dtype)


def matmul(a, w, out_dtype=F32, residual=None, rope=None, scale=1.0, tm_pref=1024, tn_pref=512):
    m, k = a.shape
    n = w.shape[1]
    tm = _tile(m, tm_pref)
    tn = _tile(n, tn_pref)
    in_specs = [pl.BlockSpec((tm, k), lambda i, j: (i, 0)), pl.BlockSpec((k, tn), lambda i, j: (0, j))]
    args = [a, w]
    if residual is not None:
        body = _mm_res_body
        in_specs.append(pl.BlockSpec((tm, tn), lambda i, j: (i, j)))
        args.append(residual)
    elif rope is not None:
        c, s1, s2 = rope
        nt = c.shape[0] // tm
        body = functools.partial(_mm_rope_body, scale=scale, reps=tn // LANES)
        tspec = pl.BlockSpec((tm, LANES), lambda i, j: (i % nt, 0))
        in_specs += [tspec, tspec, tspec]
        args += [c, s1, s2]
    else:
        body = _mm_body
    return pl.pallas_call(
        body,
        grid=(m // tm, n // tn),
        in_specs=in_specs,
        out_specs=pl.BlockSpec((tm, tn), lambda i, j: (i, j)),
        out_shape=jax.ShapeDtypeStruct((m, n), out_dtype),
        compiler_params=_cparams("parallel", "arbitrary"),
        name="matmul",
    )(*args)


def _merge_body(oa_ref, ob_ref, wa_ref, wb_ref, ga_ref, gb_ref, o_ref):
    br_a = jnp.dot(oa_ref[...], wa_ref[...], preferred_element_type=F32)
    br_b = jnp.dot(ob_ref[...], wb_ref[...], preferred_element_type=F32)
    o_ref[...] = (jax.nn.sigmoid(ga_ref[...]) * br_a + jax.nn.sigmoid(gb_ref[...]) * br_b).astype(o_ref.dtype)


def merge_branches(oa, ob, wa, wb, gates):
    m, ka = oa.shape
    n = wa.shape[1]
    tm = _tile(m, 1024)
    tn = _tile(n, 512)
    nb = n // tn
    return pl.pallas_call(
        _merge_body,
        grid=(m // tm, nb),
        in_specs=[
            pl.BlockSpec((tm, ka), lambda i, j: (i, 0)),
            pl.BlockSpec((tm, ob.shape[1]), lambda i, j: (i, 0)),
            pl.BlockSpec((ka, tn), lambda i, j: (0, j)),
            pl.BlockSpec((wb.shape[0], tn), lambda i, j: (0, j)),
            pl.BlockSpec((tm, tn), lambda i, j: (i, j)),
            pl.BlockSpec((tm, tn), lambda i, j: (i, j + nb)),
        ],
        out_specs=pl.BlockSpec((tm, tn), lambda i, j: (i, j)),
        out_shape=jax.ShapeDtypeStruct((m, n), BF16),
        compiler_params=_cparams("parallel", "arbitrary"),
        name="merge_branches",
    )(oa, ob, wa, wb, gates, gates)


def _col(arr, idx):
    lane = lax.broadcasted_iota(jnp.int32, arr.shape, 1)
    return jnp.sum(jnp.where(lane == idx, arr, 0.0), axis=1, keepdims=True)


def _row(arr, idx):
    sub = lax.broadcasted_iota(jnp.int32, arr.shape, 0)
    return jnp.sum(jnp.where(sub == idx, arr, 0.0), axis=0, keepdims=True)


def _gdn_body(cur_ref, prev_ref, st8_ref, bd_ref, s0_ref, wconv_ref, alog_ref, dtb_ref, gnorm_ref,
              o_ref, s_ref, ext_ref, *, chunk, n_valid):
    c = pl.program_id(1)
    C = chunk
    H = GDN_HEADS
    W3 = 3 * GDN_WIDTH

    @pl.when(c == 0)
    def _():
        s_ref[...] = s0_ref[...]
        ext_ref[0:SUBLANES, :] = st8_ref[0]

    @pl.when(c != 0)
    def _():
        ext_ref[0:SUBLANES, :] = prev_ref[:, 0:W3]

    ext_ref[SUBLANES:SUBLANES + C, :] = cur_ref[:, 0:W3]
    base = SUBLANES - (CONV_W - 1)
    acc = ext_ref[base:base + C, :] * wconv_ref[0:1, :]
    for i in range(1, CONV_W):
        acc = acc + ext_ref[base + i:base + i + C, :] * wconv_ref[i:i + 1, :]
    qkv = acc * jax.nn.sigmoid(acc)

    bd = bd_ref[...]
    lane = lax.broadcasted_iota(jnp.int32, bd.shape, 1)
    beta_all = jax.nn.sigmoid(bd)
    xg = bd + dtb_ref[...]
    softplus = jnp.maximum(xg, 0.0) + jnp.log1p(jnp.exp(-jnp.abs(xg)))
    g_all = -jnp.exp(alog_ref[...]) * softplus
    gb = jnp.where(lane < H, beta_all, jnp.where(lane < 2 * H, g_all, 0.0))
    if n_valid < C:
        gb = jnp.where(lax.broadcasted_iota(jnp.int32, bd.shape, 0) < n_valid, gb, 0.0)
    ri = lax.broadcasted_iota(jnp.int32, (C, C), 0)
    ci = lax.broadcasted_iota(jnp.int32, (C, C), 1)
    causal = ri >= ci
    strict = ri > ci
    ltri = jnp.where(causal, 1.0, 0.0).astype(F32)
    gc_cols = _dot_hi(ltri, gb)
    gc_rows = gc_cols.T
    eye = jnp.where(ri == ci, 1.0, 0.0).astype(F32)

    for h in range(H):
        sl = slice(h * GDN_HD, (h + 1) * GDN_HD)
        q = qkv[:, h * GDN_HD:(h + 1) * GDN_HD]
        k = qkv[:, GDN_WIDTH + h * GDN_HD:GDN_WIDTH + (h + 1) * GDN_HD]
        v = qkv[:, 2 * GDN_WIDTH + h * GDN_HD:2 * GDN_WIDTH + (h + 1) * GDN_HD]
        z = cur_ref[:, W3 + h * GDN_HD:W3 + (h + 1) * GDN_HD]
        q = q * lax.rsqrt(jnp.sum(q * q, axis=-1, keepdims=True) + L2_EPS) * (GDN_HD ** -0.5)
        k = k * lax.rsqrt(jnp.sum(k * k, axis=-1, keepdims=True) + L2_EPS)
        beta_c = _col(gb, h)
        gc_c = _col(gc_cols, H + h)
        gc_r = _row(gc_rows, H + h)
        g_last = gc_c[C - 1:C, :]
        decay = jnp.where(causal, jnp.exp(jnp.where(causal, gc_c - gc_r, 0.0)), 0.0)
        kb = k * beta_c
        a_kk = jnp.where(strict, _dot_nt(kb, k) * decay, 0.0)
        a_qk = jnp.where(causal, _dot_nt(q, k) * decay, 0.0)
        nmat = -a_kk
        tinv = eye + nmat
        span = 2
        while span < C:
            nmat = _dot_hi(nmat, nmat)
            tinv = tinv + _dot_hi(tinv, nmat)
            span *= 2
        rhs = jnp.concatenate([v * beta_c, kb * jnp.exp(gc_c)], axis=1)
        sol = _dot_hi(tinv, rhs)
        u = sol[:, 0:GDN_HD]
        w = sol[:, GDN_HD:2 * GDN_HD]
        s = s_ref[0, h]
        v_new = u - _dot(w, s)
        o = _dot(q * jnp.exp(gc_c), s) + _dot(a_qk, v_new)
        s_ref[0, h] = s * jnp.exp(g_last) + _dot_tn(k * jnp.exp(g_last - gc_c), v_new)
        o = o * lax.rsqrt(jnp.mean(o * o, axis=-1, keepdims=True) + RMS_EPS) * gnorm_ref[...]
        o = o * (z * jax.nn.sigmoid(z))
        o_ref[:, sl] = o.astype(o_ref.dtype)


def gdn(u_a, u_bd, state8, s0, w_conv, a_log, dt_bias, gdn_norm, batch, seq, chunk, n_valid):
    nc = seq // chunk
    rb = chunk // SUBLANES
    alog = jnp.zeros((1, LANES), F32).at[0, GDN_HEADS:2 * GDN_HEADS].set(a_log.astype(F32))
    dtb = jnp.zeros((1, LANES), F32).at[0, GDN_HEADS:2 * GDN_HEADS].set(dt_bias.astype(F32))
    W3 = 3 * GDN_WIDTH
    return pl.pallas_call(
        functools.partial(_gdn_body, chunk=chunk, n_valid=n_valid),
        grid=(batch, nc),
        in_specs=[
            pl.BlockSpec((chunk, 4 * GDN_WIDTH), lambda b, c: (b * nc + c, 0)),
            pl.BlockSpec((SUBLANES, 4 * GDN_WIDTH), lambda b, c: (jnp.maximum((b * nc + c) * rb - 1, 0), 0)),
            pl.BlockSpec((1, SUBLANES, W3), lambda b, c: (b, 0, 0)),
            pl.BlockSpec((chunk, LANES), lambda b, c: (b * nc + c, 0)),
            pl.BlockSpec((1, GDN_HEADS, GDN_HD, GDN_HD), lambda b, c: (b, 0, 0, 0)),
            pl.BlockSpec((CONV_W, W3), lambda b, c: (0, 0)),
            pl.BlockSpec((1, LANES), lambda b, c: (0, 0)),
            pl.BlockSpec((1, LANES), lambda b, c: (0, 0)),
            pl.BlockSpec((1, GDN_HD), lambda b, c: (0, 0)),
        ],
        out_specs=[
            pl.BlockSpec((chunk, GDN_WIDTH), lambda b, c: (b * nc + c, 0)),
            pl.BlockSpec((1, GDN_HEADS, GDN_HD, GDN_HD), lambda b, c: (b, 0, 0, 0)),
        ],
        out_shape=[
            jax.ShapeDtypeStruct((batch * seq, GDN_WIDTH), BF16),
            jax.ShapeDtypeStruct((batch, GDN_HEADS, GDN_HD, GDN_HD), F32),
        ],
        scratch_shapes=[pltpu.VMEM((SUBLANES + chunk, W3), F32)],
        compiler_params=_cparams("parallel", "arbitrary"),
        name="gdn",
    )(u_a, u_a, state8, u_bd, s0, w_conv.astype(F32), alog, dtb, gdn_norm.reshape(1, GDN_HD).astype(F32))


def _flash_body(lam_ref, q_ref, k_ref, v_ref, dn_ref, o_ref, qs_ref, m_ref, l_ref, acc_ref, *, tq):
    qi = pl.program_id(2)
    ki = pl.program_id(3)

    @pl.when(ki == 0)
    def _():
        q = q_ref[...]
        lane = lax.broadcasted_iota(jnp.int32, q.shape, 1)
        zero = jnp.zeros_like(q)
        qs_ref[0:tq, :] = jnp.where(lane < DIFF_HD, q, zero)
        qs_ref[tq:2 * tq, :] = jnp.where(lane >= DIFF_HD, q, zero)
        m_ref[...] = jnp.full(m_ref.shape, NEG_INIT, F32)
        l_ref[...] = jnp.zeros(l_ref.shape, F32)
        acc_ref[...] = jnp.zeros(acc_ref.shape, F32)

    def update(masked):
        s = lax.dot_general(qs_ref[...], k_ref[...], (((1,), (1,)), ((), ())),
                            preferred_element_type=F32)
        if masked:
            r = lax.broadcasted_iota(jnp.int32, s.shape, 0)
            r = jnp.where(r >= tq, r - tq, r)
            cidx = lax.broadcasted_iota(jnp.int32, s.shape, 1)
            s = jnp.where(cidx <= r, s, NEG_INIT)
        m_prev = m_ref[...]
        m_new = jnp.maximum(m_prev, jnp.max(s, axis=1, keepdims=True))
        alpha = jnp.exp(m_prev - m_new)
        p = jnp.exp(s - m_new)
        l_ref[...] = alpha * l_ref[...] + jnp.sum(p, axis=1, keepdims=True)
        acc_ref[...] = alpha * acc_ref[...] + jnp.dot(p.astype(BF16), v_ref[...], preferred_element_type=F32)
        m_ref[...] = m_new

    @pl.when(ki < qi)
    def _():
        update(False)

    @pl.when(ki == qi)
    def _():
        update(True)
        lam = lam_ref[0]
        inv = 1.0 / l_ref[...]
        o = acc_ref[0:tq, :] * inv[0:tq] - lam * (acc_ref[tq:2 * tq, :] * inv[tq:2 * tq])
        o = o * lax.rsqrt(jnp.mean(o * o, axis=-1, keepdims=True) + RMS_EPS) * dn_ref[...]
        o_ref[...] = (o * (1.0 - LAMBDA_INIT)).astype(o_ref.dtype)


def diff_attn_prompt(q, k, v, lam, diff_norm, batch, seq):
    tq = _tile(seq, 256)
    nq = seq // tq
    grid_spec = pltpu.PrefetchScalarGridSpec(
        num_scalar_prefetch=1,
        grid=(batch, DIFF_HEADS, nq, nq),
        in_specs=[
            pl.BlockSpec((tq, DIFF_VD), lambda b, h, i, j, lam: (b * nq + i, h)),
            pl.BlockSpec((tq, DIFF_VD), lambda b, h, i, j, lam: (b * nq + jnp.minimum(i, j), h)),
            pl.BlockSpec((tq, DIFF_VD), lambda b, h, i, j, lam: (b * nq + jnp.minimum(i, j), h)),
            pl.BlockSpec((1, DIFF_VD), lambda b, h, i, j, lam: (0, 0)),
        ],
        out_specs=pl.BlockSpec((tq, DIFF_VD), lambda b, h, i, j, lam: (b * nq + i, h)),
        scratch_shapes=[
            pltpu.VMEM((2 * tq, DIFF_VD), BF16),
            pltpu.VMEM((2 * tq, 1), F32),
            pltpu.VMEM((2 * tq, 1), F32),
            pltpu.VMEM((2 * tq, DIFF_VD), F32),
        ],
    )
    return pl.pallas_call(
        functools.partial(_flash_body, tq=tq),
        grid_spec=grid_spec,
        out_shape=jax.ShapeDtypeStruct((batch * seq, DIFF_WIDTH), BF16),
        compiler_params=_cparams("parallel", "parallel", "parallel", "arbitrary"),
        name="diff_attn_prompt",
    )(lam.reshape(1), q, k, v, diff_norm.reshape(1, DIFF_VD).astype(F32))


TOK_PAD = SUBLANES
QROWS = 2 * DIFF_HEADS * TOK_PAD


def _paged_body(pt_ref, lam_ref, qbd_ref, kp_ref, vp_ref, kn_ref, vn_ref, dn_ref, o_ref,
                m_ref, l_ref, acc_ref, *, n_tok):
    p = pl.program_id(1)
    npg = pl.num_programs(1)

    @pl.when(p == 0)
    def _():
        m_ref[...] = jnp.full(m_ref.shape, NEG_INIT, F32)
        l_ref[...] = jnp.zeros(l_ref.shape, F32)
        acc_ref[...] = jnp.zeros(acc_ref.shape, F32)

    def update(kb, vb, mask):
        s = lax.dot_general(qbd_ref[0], kb.astype(BF16), (((1,), (1,)), ((), ())),
                            preferred_element_type=F32)
        if mask is not None:
            s = jnp.where(mask, s, NEG_INIT)
        m_prev = m_ref[...]
        m_new = jnp.maximum(m_prev, jnp.max(s, axis=1, keepdims=True))
        alpha = jnp.exp(m_prev - m_new)
        pr = jnp.exp(s - m_new)
        l_ref[...] = alpha * l_ref[...] + jnp.sum(pr, axis=1, keepdims=True)
        acc_ref[...] = alpha * acc_ref[...] + jnp.dot(pr.astype(BF16), vb.astype(BF16),
                                                      preferred_element_type=F32)
        m_ref[...] = m_new

    update(kp_ref[0], vp_ref[0], None)

    @pl.when(p == npg - 1)
    def _():
        r = lax.broadcasted_iota(jnp.int32, (QROWS, TOK_PAD), 0)
        t = jnp.bitwise_and(r, TOK_PAD - 1)
        j = lax.broadcasted_iota(jnp.int32, (QROWS, TOK_PAD), 1)
        update(kn_ref[0], vn_ref[0], jnp.logical_and(j <= t, j < n_tok))
        lam = lam_ref[0]
        inv = 1.0 / l_ref[...]
        half = QROWS // 2
        for h in range(DIFF_HEADS):
            r0 = h * TOK_PAD
            sl = slice(h * DIFF_VD, (h + 1) * DIFF_VD)
            o1 = acc_ref[r0:r0 + TOK_PAD, sl] * inv[r0:r0 + TOK_PAD]
            o2 = acc_ref[half + r0:half + r0 + TOK_PAD, sl] * inv[half + r0:half + r0 + TOK_PAD]
            o = o1 - lam * o2
            o = o * lax.rsqrt(jnp.mean(o * o, axis=-1, keepdims=True) + RMS_EPS) * dn_ref[...]
            o_ref[0, :, sl] = (o * (1.0 - LAMBDA_INIT)).astype(o_ref.dtype)


def diff_attn_sample(q, k_new, v_new, lam, diff_norm, cache_k, cache_v, page_table, n_tok):
    bd = q.shape[0]
    n_pool = cache_k.shape[0]
    width = DIFF_WIDTH
    n_pages = page_table.shape[1]
    qh = q.reshape(bd, n_tok, DIFF_HEADS, 2, DIFF_HD)
    qh = jnp.pad(qh, ((0, 0), (0, TOK_PAD - n_tok), (0, 0), (0, 0), (0, 0)))
    qh = jnp.transpose(qh, (0, 3, 2, 1, 4))
    eye_m = jnp.eye(2, dtype=q.dtype)
    eye_h = jnp.eye(DIFF_HEADS, dtype=q.dtype)
    qbd = jnp.einsum('bchtd,cx,hy->bchtyxd', qh, eye_m, eye_h).reshape(bd, QROWS, width).astype(BF16)
    padn = ((0, 0), (0, TOK_PAD - n_tok), (0, 0))
    kn = jnp.pad(k_new, padn)
    vn = jnp.pad(v_new, padn)
    ck = cache_k.reshape(n_pool, PAGE_SIZE, width)
    cv = cache_v.reshape(n_pool, PAGE_SIZE, width)
    grid_spec = pltpu.PrefetchScalarGridSpec(
        num_scalar_prefetch=2,
        grid=(bd, n_pages),
        in_specs=[
            pl.BlockSpec((1, QROWS, width), lambda b, p, pt, lam: (b, 0, 0)),
            pl.BlockSpec((1, PAGE_SIZE, width), lambda b, p, pt, lam: (pt[b, p], 0, 0)),
            pl.BlockSpec((1, PAGE_SIZE, width), lambda b, p, pt, lam: (pt[b, p], 0, 0)),
            pl.BlockSpec((1, TOK_PAD, width), lambda b, p, pt, lam: (b, 0, 0)),
            pl.BlockSpec((1, TOK_PAD, width), lambda b, p, pt, lam: (b, 0, 0)),
            pl.BlockSpec((1, DIFF_VD), lambda b, p, pt, lam: (0, 0)),
        ],
        out_specs=pl.BlockSpec((1, TOK_PAD, width), lambda b, p, pt, lam: (b, 0, 0)),
        scratch_shapes=[
            pltpu.VMEM((QROWS, 1), F32),
            pltpu.VMEM((QROWS, 1), F32),
            pltpu.VMEM((QROWS, width), F32),
        ],
    )
    return pl.pallas_call(
        functools.partial(_paged_body, n_tok=n_tok),
        grid_spec=grid_spec,
        out_shape=jax.ShapeDtypeStruct((bd, TOK_PAD, width), BF16),
        compiler_params=_cparams("parallel", "arbitrary"),
        name="diff_attn_sample",
    )(page_table, lam.reshape(1), qbd, ck, cv, kn, vn, diff_norm.reshape(1, DIFF_VD).astype(F32))


def _mem_body(q_ref, k_ref, v_ref, o_ref):
    for h in range(MEM_HEADS):
        sl = slice(h * MEM_HD, (h + 1) * MEM_HD)
        s = lax.dot_general(q_ref[0, :, sl], k_ref[0, :, sl], (((1,), (1,)), ((), ())),
                            preferred_element_type=F32) * (MEM_HD ** -0.5)
        s = s - jnp.max(s, axis=1, keepdims=True)
        e = jnp.exp(s)
        p = e / jnp.sum(e, axis=1, keepdims=True)
        o_ref[0, :, sl] = jnp.dot(p.astype(BF16), v_ref[0, :, sl], preferred_element_type=F32).astype(o_ref.dtype)


def mem_attn(q, mem_k, mem_v):
    b, t, _ = q.shape
    mlen = mem_k.shape[1]
    tq = _tile(t, 512)
    return pl.pallas_call(
        _mem_body,
        grid=(b, t // tq),
        in_specs=[
            pl.BlockSpec((1, tq, MEM_WIDTH), lambda i, j: (i, j, 0)),
            pl.BlockSpec((1, mlen, MEM_WIDTH), lambda i, j: (i, 0, 0)),
            pl.BlockSpec((1, mlen, MEM_WIDTH), lambda i, j: (i, 0, 0)),
        ],
        out_specs=pl.BlockSpec((1, tq, MEM_WIDTH), lambda i, j: (i, j, 0)),
        out_shape=jax.ShapeDtypeStruct((b, t, MEM_WIDTH), BF16),
        compiler_params=_cparams("parallel", "arbitrary"),
        name="mem_attn",
    )(q, mem_k, mem_v)


def _norm_router_body(x_ref, g_ref, wr_ref, br_ref, h_ref, lg_ref):
    x = x_ref[...]
    hn = x * lax.rsqrt(jnp.mean(x * x, axis=-1, keepdims=True) + RMS_EPS) * g_ref[...]
    h_ref[...] = hn.astype(h_ref.dtype)
    lg_ref[...] = _dot_hi(hn, wr_ref[...]) + br_ref[...]


def norm_router(x, g, w_router, b_router):
    m, d = x.shape
    tm = _tile(m, 512)
    return pl.pallas_call(
        _norm_router_body,
        grid=(m // tm,),
        in_specs=[
            pl.BlockSpec((tm, d), lambda i: (i, 0)),
            pl.BlockSpec((1, d), lambda i: (0, 0)),
            pl.BlockSpec((d, LANES), lambda i: (0, 0)),
            pl.BlockSpec((1, LANES), lambda i: (0, 0)),
        ],
        out_specs=[pl.BlockSpec((tm, d), lambda i: (i, 0)), pl.BlockSpec((tm, LANES), lambda i: (i, 0))],
        out_shape=[jax.ShapeDtypeStruct((m, d), BF16), jax.ShapeDtypeStruct((m, LANES), F32)],
        compiler_params=_cparams("parallel"),
        name="norm_router",
    )(x, g.reshape(1, d).astype(F32), w_router, b_router)


def _expert_body(te_ref, tv_ref, x_ref, wg_ref, wu_ref, wd_ref, gate_ref, o_ref):
    i = pl.program_id(0)

    @pl.when(tv_ref[i] != 0)
    def _():
        x = x_ref[...]
        a = jnp.dot(x, wg_ref[0].astype(BF16), preferred_element_type=F32)
        b = jnp.dot(x, wu_ref[0].astype(BF16), preferred_element_type=F32)
        hid = (a * jax.nn.sigmoid(a)) * b
        y = jnp.dot(hid.astype(BF16), wd_ref[0].astype(BF16), preferred_element_type=F32)
        o_ref[...] = y * gate_ref[...]

    @pl.when(tv_ref[i] == 0)
    def _():
        o_ref[...] = jnp.zeros(o_ref.shape, o_ref.dtype)


def grouped_experts(xs, row_gate, tile_expert, tile_valid, w_gate, w_up, w_down, tm):
    rows, d = xs.shape
    nt = rows // tm
    ff = w_gate.shape[2]
    grid_spec = pltpu.PrefetchScalarGridSpec(
        num_scalar_prefetch=2,
        grid=(nt,),
        in_specs=[
            pl.BlockSpec((tm, d), lambda i, te, tv: (i, 0)),
            pl.BlockSpec((1, d, ff), lambda i, te, tv: (te[i], 0, 0)),
            pl.BlockSpec((1, d, ff), lambda i, te, tv: (te[i], 0, 0)),
            pl.BlockSpec((1, ff, d), lambda i, te, tv: (te[i], 0, 0)),
            pl.BlockSpec((tm, 1), lambda i, te, tv: (i, 0)),
        ],
        out_specs=pl.BlockSpec((tm, d), lambda i, te, tv: (i, 0)),
    )
    return pl.pallas_call(
        _expert_body,
        grid_spec=grid_spec,
        out_shape=jax.ShapeDtypeStruct((rows, d), F32),
        compiler_params=_cparams("arbitrary"),
        name="grouped_experts",
    )(tile_expert, tile_valid, xs, w_gate, w_up, w_down, row_gate)


def _final_body(x_ref, ya_ref, yb_ref, g_ref, o_ref):
    x = x_ref[...] + (ya_ref[...] + yb_ref[...])
    o_ref[...] = x * lax.rsqrt(jnp.mean(x * x, axis=-1, keepdims=True) + RMS_EPS) * g_ref[...]


def final_norm(x, ya, yb, g):
    m, d = x.shape
    tm = _tile(m, 512)
    spec = pl.BlockSpec((tm, d), lambda i: (i, 0))
    return pl.pallas_call(
        _final_body,
        grid=(m // tm,),
        in_specs=[spec, spec, spec, pl.BlockSpec((1, d), lambda i: (0, 0))],
        out_specs=spec,
        out_shape=jax.ShapeDtypeStruct((m, d), F32),
        compiler_params=_cparams("parallel"),
        name="final_norm",
    )(x, ya, yb, g.reshape(1, d).astype(F32))


MOE_TILE = 256


def moe_and_final(x_list, norm_ffn, w_rg, b_rg, w_re, b_re, w_gate, w_up, w_down, norm_final):
    d = x_list[0].shape[1]
    w_router = jnp.zeros((d, LANES), F32).at[:, 0:N_GROUPS].set(w_rg.astype(F32))
    w_router = w_router.at[:, N_GROUPS:N_GROUPS + N_EXPERTS].set(w_re.astype(F32))
    b_router = jnp.zeros((1, LANES), F32).at[0, 0:N_GROUPS].set(b_rg.astype(F32))
    b_router = b_router.at[0, N_GROUPS:N_GROUPS + N_EXPERTS].set(b_re.astype(F32))
    hs, lgs = [], []
    for x in x_list:
        hn, lg = norm_router(x, norm_ffn, w_router, b_router)
        hs.append(hn)
        lgs.append(lg)
    hn = jnp.concatenate(hs, axis=0)
    lg = jnp.concatenate(lgs, axis=0)
    n = hn.shape[0]
    g_prob = jax.nn.softmax(lg[:, 0:N_GROUPS], axis=-1)
    g_w, g_idx = lax.top_k(g_prob, 1)
    e_all = lg[:, N_GROUPS:N_GROUPS + N_EXPERTS].reshape(n, N_GROUPS, EXPERTS_PER_GROUP)
    e_logits = jnp.take_along_axis(e_all, g_idx[:, :, None], axis=1)[:, 0]
    e_prob = jax.nn.softmax(e_logits, axis=-1)
    e_w, e_idx = lax.top_k(e_prob, TOP_K_INNER)
    e_w = e_w / jnp.sum(e_w, axis=-1, keepdims=True)
    expert_id = (g_idx * EXPERTS_PER_GROUP + e_idx).astype(jnp.int32)
    gate = g_w * e_w
    tm = MOE_TILE
    flat_e = expert_id.reshape(-1)
    na = flat_e.shape[0]
    order = jnp.argsort(flat_e, stable=True).astype(jnp.int32)
    counts = jnp.zeros((N_EXPERTS,), jnp.int32).at[flat_e].add(1)
    padded = ((counts + tm - 1) // tm) * tm
    pad_start = jnp.cumsum(padded) - padded
    start = jnp.cumsum(counts) - counts
    sorted_e = flat_e[order]
    dest_sorted = pad_start[sorted_e] + (jnp.arange(na, dtype=jnp.int32) - start[sorted_e])
    nt = (na + tm - 1) // tm + N_EXPERTS
    rows = nt * tm
    row_tok = jnp.zeros((rows,), jnp.int32).at[dest_sorted].set(order // TOP_K_INNER)
    row_gate = jnp.zeros((rows,), F32).at[dest_sorted].set(gate.reshape(-1)[order])
    dest = jnp.zeros((na,), jnp.int32).at[order].set(dest_sorted)
    tile_start = jnp.arange(nt, dtype=jnp.int32) * tm
    pad_end = jnp.cumsum(padded)
    tile_expert = jnp.minimum(jnp.searchsorted(pad_end, tile_start, side='right'), N_EXPERTS - 1).astype(jnp.int32)
    tile_valid = (tile_start < pad_end[-1]).astype(jnp.int32)
    xs = jnp.take(hn, row_tok, axis=0)
    ys = grouped_experts(xs, row_gate.reshape(rows, 1), tile_expert, tile_valid, w_gate, w_up, w_down, tm)
    dest2 = dest.reshape(n, TOP_K_INNER)
    ya = jnp.take(ys, dest2[:, 0], axis=0)
    yb = jnp.take(ys, dest2[:, 1], axis=0)
    outs, o = [], 0
    for x in x_list:
        m = x.shape[0]
        outs.append(final_norm(x, ya[o:o + m], yb[o:o + m], norm_final))
        o += m
    return outs


def _rope_tables(pos, rows):
    half = ROT_DIM // 2
    inv_freq = ROPE_THETA ** (-jnp.arange(0, ROT_DIM, 2, dtype=F32) / ROT_DIM)
    ang = pos.astype(F32)[:, None] * inv_freq[None, :]
    cos, sin = jnp.cos(ang), jnp.sin(ang)
    t = pos.shape[0]
    one = jnp.ones((t, DIFF_HD - ROT_DIM), F32)
    zero = jnp.zeros((t, DIFF_HD - ROT_DIM), F32)
    zh = jnp.zeros((t, half), F32)
    c = jnp.concatenate([cos, cos, one], axis=1)
    s1 = jnp.concatenate([-sin, zh, zero], axis=1)
    s2 = jnp.concatenate([zh, sin, zero], axis=1)
    rep = rows // t
    tile = lambda a: jnp.tile(jnp.concatenate([a, a], axis=1), (rep, 1))
    return tile(c), tile(s1), tile(s2)


def _prep_weights(p):
    w_in = p['w_in']
    o = 4 * GDN_WIDTH
    ob = o + 2 * GDN_HEADS
    d = w_in.shape[0]
    w = {}
    w['a'] = w_in[:, 0:o].astype(BF16)
    w['bd'] = jnp.zeros((d, LANES), BF16).at[:, 0:2 * GDN_HEADS].set(w_in[:, o:ob].astype(BF16))
    w['dq'] = w_in[:, ob:ob + DIFF_WIDTH].astype(BF16)
    w['dk'] = w_in[:, ob + DIFF_WIDTH:ob + 2 * DIFF_WIDTH].astype(BF16)
    w['dv'] = w_in[:, ob + 2 * DIFF_WIDTH:ob + 3 * DIFF_WIDTH].astype(BF16)
    w['g'] = w_in[:, ob + 3 * DIFF_WIDTH:].astype(BF16)
    for name in ('w_branch_a', 'w_branch_b', 'w_out', 'w_mq', 'w_mk', 'w_mv', 'w_mo'):
        w[name] = p[name].astype(BF16)
    return w


def _mixers(x, batch, seq, pos, conv_state, delta_state, mem_k, mem_v, p, w, lam, sample_ctx):
    m, d = x.shape
    h = rmsnorm_rows(x, p['norm_mix'])
    u_a = matmul(h, w['a'])
    u_bd = matmul(h, w['bd'])
    u_g = matmul(h, w['g'])
    tm = _tile(m, 1024)
    tables = _rope_tables(pos, max(tm, seq))
    d_v = matmul(h, w['dv'])
    d_k = matmul(h, w['dk'], rope=tables)
    chunk = min(GDN_CHUNK, seq)
    seq_pad = seq
    if chunk % SUBLANES:
        chunk = SUBLANES
        seq_pad = SUBLANES
        padr = lambda a: jnp.pad(a.reshape(batch, seq, -1), ((0, 0), (0, seq_pad - seq), (0, 0))).reshape(
            batch * seq_pad, -1)
        u_a_g, u_bd_g = padr(u_a), padr(u_bd)
    else:
        u_a_g, u_bd_g = u_a, u_bd
    state8 = jnp.pad(conv_state.astype(F32), ((0, 0), (SUBLANES - (CONV_W - 1), 0), (0, 0)))
    o_a, new_delta = gdn(u_a_g, u_bd_g, state8, delta_state.astype(F32), p['w_conv'], p['a_log'], p['dt_bias'],
                         p['gdn_norm'], batch, seq_pad, chunk, min(seq, chunk))
    if seq_pad != seq:
        o_a = o_a.reshape(batch, seq_pad, GDN_WIDTH)[:, :seq].reshape(m, GDN_WIDTH)
    full = jnp.concatenate([conv_state.astype(F32), u_a[:, 0:3 * GDN_WIDTH].reshape(batch, seq, -1)], axis=1)
    new_conv = full[:, seq:]
    if sample_ctx is None:
        d_q = matmul(h, w['dq'], out_dtype=BF16, rope=tables, scale=DIFF_HD ** -0.5)
        o_b = diff_attn_prompt(d_q, d_k.astype(BF16), d_v.astype(BF16), lam, p['diff_norm'], batch, seq)
    else:
        cache_k, cache_v, page_table = sample_ctx
        d_q = matmul(h, w['dq'], rope=tables, scale=DIFF_HD ** -0.5)
        o_b = diff_attn_sample(d_q.reshape(batch, seq, -1), d_k.reshape(batch, seq, -1), d_v.reshape(batch, seq, -1),
                               lam, p['diff_norm'], cache_k, cache_v, page_table, seq)
        o_b = o_b[:, :seq].reshape(m, DIFF_WIDTH)
    mixed = merge_branches(o_a, o_b, w['w_branch_a'], w['w_branch_b'], u_g)
    x = matmul(mixed, w['w_out'], residual=x)
    hc = rmsnorm_rows(x, p['norm_cross'])
    mq = matmul(hc, w['w_mq'], out_dtype=BF16).reshape(batch, seq, MEM_WIDTH)
    if seq % SUBLANES:
        mq = jnp.pad(mq, ((0, 0), (0, SUBLANES - seq), (0, 0)))
    mo = mem_attn(mq, mem_k, mem_v)[:, :seq].reshape(m, MEM_WIDTH)
    x = matmul(mo, w['w_mo'], residual=x)
    return x, new_conv, new_delta, d_k, d_v


def kernel(x_prompt, x_sample, cache_k, cache_v, cache_mem_k, cache_mem_v, state_delta, state_conv, page_table, mem_prompt, norm_mix, w_in, w_conv, a_log, dt_bias, gdn_norm, lambda_q1, lambda_k1, lambda_q2, lambda_k2, diff_norm, w_branch_a, w_branch_b, w_out, norm_cross, norm_mem, w_mq, w_mk, w_mv, w_mo, norm_ffn, w_router_group, b_router_group, w_router_expert, b_router_expert, w_gate, w_up, w_down, norm_final):
    p = dict(norm_mix=norm_mix, w_in=w_in, w_conv=w_conv, a_log=a_log, dt_bias=dt_bias, gdn_norm=gdn_norm,
             diff_norm=diff_norm, w_branch_a=w_branch_a, w_branch_b=w_branch_b, w_out=w_out,
             norm_cross=norm_cross, w_mq=w_mq, w_mk=w_mk, w_mv=w_mv, w_mo=w_mo)
    bp, tp, d = x_prompt.shape
    bs, ts, _ = x_sample.shape
    past_len = page_table.shape[1] * PAGE_SIZE
    w = _prep_weights(p)
    lam = (jnp.exp(jnp.sum(lambda_q1.astype(F32) * lambda_k1.astype(F32)))
           - jnp.exp(jnp.sum(lambda_q2.astype(F32) * lambda_k2.astype(F32))) + LAMBDA_INIT)
    mlen = mem_prompt.shape[1]
    hm = rmsnorm_rows(mem_prompt.reshape(bp * mlen, d), norm_mem)
    mem_k_p = matmul(hm, w['w_mk'])
    mem_v_p = matmul(hm, w['w_mv'])
    conv0 = jnp.zeros((bp, CONV_W - 1, 3 * GDN_WIDTH), F32)
    delta0 = jnp.zeros((bp, GDN_HEADS, GDN_HD, GDN_HD), F32)
    xp, conv_p, delta_p, k_p, v_p = _mixers(
        x_prompt.reshape(bp * tp, d), bp, tp, jnp.arange(tp), conv0, delta0,
        mem_k_p.astype(BF16).reshape(bp, mlen, MEM_WIDTH), mem_v_p.astype(BF16).reshape(bp, mlen, MEM_WIDTH),
        p, w, lam, None)
    xs, conv_s, delta_s, k_s, v_s = _mixers(
        x_sample.reshape(bs * ts, d), bs, ts, past_len + jnp.arange(ts), state_conv, state_delta,
        cache_mem_k.astype(BF16).reshape(bs, -1, MEM_WIDTH), cache_mem_v.astype(BF16).reshape(bs, -1, MEM_WIDTH),
        p, w, lam, (cache_k, cache_v, page_table))
    yp, ys = moe_and_final([xp, xs], norm_ffn, w_router_group, b_router_group, w_router_expert, b_router_expert,
                           w_gate, w_up, w_down, norm_final)
    return (yp.reshape(bp, tp, d), ys.reshape(bs, ts, d),
            k_p.reshape(bp, tp, DIFF_HEADS, DIFF_VD), v_p.reshape(bp, tp, DIFF_HEADS, DIFF_VD),
            mem_k_p.reshape(bp, mlen, MEM_HEADS, MEM_HD), mem_v_p.reshape(bp, mlen, MEM_HEADS, MEM_HD),
            delta_p.astype(state_delta.dtype), conv_p.astype(x_prompt.dtype),
            k_s.reshape(bs, ts, DIFF_HEADS, DIFF_VD), v_s.reshape(bs, ts, DIFF_HEADS, DIFF_VD),
            delta_s.astype(state_delta.dtype), conv_s.astype(state_conv.dtype))
```

```python
import functools
import math

import jax
import jax.numpy as jnp
from jax import lax
from jax.experimental import pallas as pl
from jax.experimental.pallas import tpu as pltpu

F32 = jnp.float32
BF16 = jnp.bfloat16

GDN_HEADS = 8
GDN_HD = 128
GDN_WIDTH = GDN_HEADS * GDN_HD
CONV_W = 4
GDN_CHUNK = 64
DIFF_HEADS = 8
DIFF_HD = 64
DIFF_VD = 2 * DIFF_HD
DIFF_WIDTH = DIFF_HEADS * DIFF_VD
ROT_DIM = DIFF_HD // 4
ROPE_THETA = 500000.0
LAMBDA_INIT = 0.2
PAGE_SIZE = 128
MEM_HEADS = 4
MEM_HD = 128
MEM_WIDTH = MEM_HEADS * MEM_HD
N_GROUPS = 4
EXPERTS_PER_GROUP = 8
N_EXPERTS = N_GROUPS * EXPERTS_PER_GROUP
TOP_K_INNER = 2
RMS_EPS = 1e-6
L2_EPS = 1e-6
NEG_INIT = -1e30

LANES = 128
SUBLANES = 8
VMEM_LIMIT = 48 * 1024 * 1024
FLASH_TILE = 512
FLASH_SUB = 1024
PAGES_PER_STEP = 4
LOG2E = math.log2(math.e)

def _cparams(*sem):
    return pltpu.CompilerParams(dimension_semantics=sem, vmem_limit_bytes=VMEM_LIMIT)


def _tile(n, pref):
    if n <= pref:
        return n
    t = pref
    while n % t:
        t //= 2
    return t


def _dot(a, b):
    return jnp.dot(a.astype(BF16), b.astype(BF16), preferred_element_type=F32)


def _dot_nt(a, b):
    return lax.dot_general(a.astype(BF16), b.astype(BF16), (((1,), (1,)), ((), ())),
                           preferred_element_type=F32)


def _dot_tn(a, b):
    return lax.dot_general(a.astype(BF16), b.astype(BF16), (((0,), (0,)), ((), ())),
                           preferred_element_type=F32)


def _split3(x):
    hi = x.astype(BF16)
    r = x - hi.astype(F32)
    mid = r.astype(BF16)
    lo = (r - mid.astype(F32)).astype(BF16)
    return hi, mid, lo


def _dot3(a, b):
    ah, am, _ = _split3(a)
    bh, bm, _ = _split3(b)
    d = lambda x, y: jnp.dot(x, y, preferred_element_type=F32)
    return d(ah, bh) + (d(ah, bm) + d(am, bh))


def _rmsnorm_body(x_ref, g_ref, o_ref):
    x = x_ref[...]
    ms = jnp.mean(x * x, axis=-1, keepdims=True)
    o_ref[...] = (x * lax.rsqrt(ms + RMS_EPS) * g_ref[...]).astype(o_ref.dtype)


def rmsnorm_rows(x, g, out_dtype=BF16):
    m, d = x.shape
    tm = _tile(m, 512)
    return pl.pallas_call(
        _rmsnorm_body,
        grid=(m // tm,),
        in_specs=[pl.BlockSpec((tm, d), lambda i: (i, 0)), pl.BlockSpec((1, d), lambda i: (0, 0))],
        out_specs=pl.BlockSpec((tm, d), lambda i: (i, 0)),
        out_shape=jax.ShapeDtypeStruct((m, d), out_dtype),
        compiler_params=_cparams("parallel"),
        name="rmsnorm_rows",
    )(x, g.reshape(1, d).astype(F32))


def _mm_body(a_ref, w_ref, o_ref):
    o_ref[...] = jnp.dot(a_ref[...], w_ref[...], preferred_element_type=F32).astype(o_ref.dtype)


def _mm_res_body(a_ref, w_ref, r_ref, o_ref):
    o_ref[...] = r_ref[...] + jnp.dot(a_ref[...], w_ref[...], preferred_element_type=F32)


def _mm_rope_body(a_ref, w_ref, c_ref, s1_ref, s2_ref, o_ref, *, scale, reps):
    o = jnp.dot(a_ref[...], w_ref[...], preferred_element_type=F32)
    tn = o.shape[1]
    c = jnp.tile(c_ref[...], (1, reps))
    s1 = jnp.tile(s1_ref[...], (1, reps))
    s2 = jnp.tile(s2_ref[...], (1, reps))
    r = o * c + pltpu.roll(o, tn - ROT_DIM // 2, 1) * s1 + pltpu.roll(o, ROT_DIM // 2, 1) * s2
    if scale != 1.0:
        r = r * scale
    o_ref[...] = r.astype(o_ref.dtype)


def matmul(a, w, out_dtype=F32, residual=None, rope=None, scale=1.0, tm_pref=1024, tn_pref=512):
    m, k = a.shape
    n = w.shape[1]
    tm = _tile(m, tm_pref)
    tn = _tile(n, tn_pref)
    in_specs = [pl.BlockSpec((tm, k), lambda i, j: (i, 0)), pl.BlockSpec((k, tn), lambda i, j: (0, j))]
    args = [a, w]
    if residual is not None:
        body = _mm_res_body
        in_specs.append(pl.BlockSpec((tm, tn), lambda i, j: (i, j)))
        args.append(residual)
    elif rope is not None:
        c, s1, s2 = rope
        nt = c.shape[0] // tm
        body = functools.partial(_mm_rope_body, scale=scale, reps=tn // LANES)
        tspec = pl.BlockSpec((tm, LANES), lambda i, j: (i % nt, 0))
        in_specs += [tspec, tspec, tspec]
        args += [c, s1, s2]
    else:
        body = _mm_body
    return pl.pallas_call(
        body,
        grid=(m // tm, n // tn),
        in_specs=in_specs,
        out_specs=pl.BlockSpec((tm, tn), lambda i, j: (i, j)),
        out_shape=jax.ShapeDtypeStruct((m, n), out_dtype),
        compiler_params=_cparams("parallel", "arbitrary"),
        name="matmul",
    )(*args)


def _merge_body(oa_ref, ob_ref, wa_ref, wb_ref, ga_ref, gb_ref, o_ref):
    br_a = jnp.dot(oa_ref[...], wa_ref[...], preferred_element_type=F32)
    br_b = jnp.dot(ob_ref[...], wb_ref[...], preferred_element_type=F32)
    o_ref[...] = (jax.nn.sigmoid(ga_ref[...]) * br_a + jax.nn.sigmoid(gb_ref[...]) * br_b).astype(o_ref.dtype)


def merge_branches(oa, ob, wa, wb, gates):
    m, ka = oa.shape
    n = wa.shape[1]
    tm = _tile(m, 1024)
    tn = _tile(n, 512)
    nb = n // tn
    return pl.pallas_call(
        _merge_body,
        grid=(m // tm, nb),
        in_specs=[
            pl.BlockSpec((tm, ka), lambda i, j: (i, 0)),
            pl.BlockSpec((tm, ob.shape[1]), lambda i, j: (i, 0)),
            pl.BlockSpec((ka, tn), lambda i, j: (0, j)),
            pl.BlockSpec((wb.shape[0], tn), lambda i, j: (0, j)),
            pl.BlockSpec((tm, tn), lambda i, j: (i, j)),
            pl.BlockSpec((tm, tn), lambda i, j: (i, j + nb)),
        ],
        out_specs=pl.BlockSpec((tm, tn), lambda i, j: (i, j)),
        out_shape=jax.ShapeDtypeStruct((m, n), BF16),
        compiler_params=_cparams("parallel", "arbitrary"),
        name="merge_branches",
    )(oa, ob, wa, wb, gates, gates)


def _col(arr, idx):
    lane = lax.broadcasted_iota(jnp.int32, arr.shape, 1)
    return jnp.sum(jnp.where(lane == idx, arr, 0.0), axis=1, keepdims=True)


def _row(arr, idx):
    sub = lax.broadcasted_iota(jnp.int32, arr.shape, 0)
    return jnp.sum(jnp.where(sub == idx, arr, 0.0), axis=0, keepdims=True)


def _gdn_body(cur_ref, prev_ref, st8_ref, bd_ref, s0_ref, wconv_ref, alog_ref, dtb_ref, gnorm_ref,
              o_ref, s_ref, ext_ref, *, chunk, n_valid):
    c = pl.program_id(1)
    C = chunk
    H = GDN_HEADS
    W3 = 3 * GDN_WIDTH

    @pl.when(c == 0)
    def _():
        s_ref[...] = s0_ref[...]
        ext_ref[0:SUBLANES, :] = st8_ref[0]

    @pl.when(c != 0)
    def _():
        ext_ref[0:SUBLANES, :] = prev_ref[:, 0:W3]

    ext_ref[SUBLANES:SUBLANES + C, :] = cur_ref[:, 0:W3]
    base = SUBLANES - (CONV_W - 1)
    acc = ext_ref[base:base + C, :] * wconv_ref[0:1, :]
    for i in range(1, CONV_W):
        acc = acc + ext_ref[base + i:base + i + C, :] * wconv_ref[i:i + 1, :]
    qkv = acc * jax.nn.sigmoid(acc)

    bd = bd_ref[...]
    lane = lax.broadcasted_iota(jnp.int32, bd.shape, 1)
    beta_all = jax.nn.sigmoid(bd)
    xg = bd + dtb_ref[...]
    softplus = jnp.maximum(xg, 0.0) + jnp.log1p(jnp.exp(-jnp.abs(xg)))
    g_all = -jnp.exp(alog_ref[...]) * softplus
    gb = jnp.where(lane < H, beta_all, jnp.where(lane < 2 * H, g_all, 0.0))
    if n_valid < C:
        gb = jnp.where(lax.broadcasted_iota(jnp.int32, bd.shape, 0) < n_valid, gb, 0.0)
    ri = lax.broadcasted_iota(jnp.int32, (C, C), 0)
    ci = lax.broadcasted_iota(jnp.int32, (C, C), 1)
    causal = ri >= ci
    strict = ri > ci
    ltri = jnp.where(causal, 1.0, 0.0).astype(BF16)
    gc_cols = sum(jnp.dot(ltri, piece, preferred_element_type=F32) for piece in _split3(gb))
    gc_rows = gc_cols.T
    eye = jnp.where(ri == ci, 1.0, 0.0).astype(F32)

    heads = []
    for h in range(H):
        q = qkv[:, h * GDN_HD:(h + 1) * GDN_HD]
        k = qkv[:, GDN_WIDTH + h * GDN_HD:GDN_WIDTH + (h + 1) * GDN_HD]
        v = qkv[:, 2 * GDN_WIDTH + h * GDN_HD:2 * GDN_WIDTH + (h + 1) * GDN_HD]
        q = q * lax.rsqrt(jnp.sum(q * q, axis=-1, keepdims=True) + L2_EPS) * (GDN_HD ** -0.5)
        k = k * lax.rsqrt(jnp.sum(k * k, axis=-1, keepdims=True) + L2_EPS)
        beta_c = _col(gb, h)
        gc_c = _col(gc_cols, H + h)
        gc_r = _row(gc_rows, H + h)
        g_last = gc_c[C - 1:C, :]
        decay = jnp.where(causal, jnp.exp(jnp.where(causal, gc_c - gc_r, 0.0)), 0.0)
        kb = k * beta_c
        kk = _dot_nt(jnp.concatenate([kb, q], axis=0), k)
        a_kk = jnp.where(strict, kk[0:C] * decay, 0.0)
        a_qk = jnp.where(causal, kk[C:2 * C] * decay, 0.0)
        rhs = jnp.concatenate([v * beta_c, kb * jnp.exp(gc_c)], axis=1)
        heads.append(dict(q=q, k=k, gc_c=gc_c, g_last=g_last, a_qk=a_qk, rhs=rhs, nmat=-a_kk, tinv=eye - a_kk))
    span = 2
    while span < C:
        for hd in heads:
            hd['nmat'] = _dot3(hd['nmat'], hd['nmat'])
        for hd in heads:
            hd['tinv'] = hd['tinv'] + _dot3(hd['tinv'], hd['nmat'])
        span *= 2
    for hd in heads:
        hd['sol'] = _dot3(hd['tinv'], hd['rhs'])
    for h, hd in enumerate(heads):
        sl = slice(h * GDN_HD, (h + 1) * GDN_HD)
        q, k, gc_c, g_last = hd['q'], hd['k'], hd['gc_c'], hd['g_last']
        u = hd['sol'][:, 0:GDN_HD]
        w = hd['sol'][:, GDN_HD:2 * GDN_HD]
        s = s_ref[0, h]
        ws = _dot(jnp.concatenate([w, q * jnp.exp(gc_c)], axis=0), s)
        v_new = u - ws[0:C]
        o = ws[C:2 * C] + _dot(hd['a_qk'], v_new)
        s_ref[0, h] = s * jnp.exp(g_last) + _dot_tn(k * jnp.exp(g_last - gc_c), v_new)
        z = cur_ref[:, W3 + h * GDN_HD:W3 + (h + 1) * GDN_HD]
        o = o * lax.rsqrt(jnp.mean(o * o, axis=-1, keepdims=True) + RMS_EPS) * gnorm_ref[...]
        o = o * (z * jax.nn.sigmoid(z))
        o_ref[:, sl] = o.astype(o_ref.dtype)


def gdn(u_a, u_bd, state8, s0, w_conv, a_log, dt_bias, gdn_norm, batch, seq, chunk, n_valid):
    nc = seq // chunk
    rb = chunk // SUBLANES
    alog = jnp.zeros((1, LANES), F32).at[0, GDN_HEADS:2 * GDN_HEADS].set(a_log.astype(F32))
    dtb = jnp.zeros((1, LANES), F32).at[0, GDN_HEADS:2 * GDN_HEADS].set(dt_bias.astype(F32))
    W3 = 3 * GDN_WIDTH
    return pl.pallas_call(
        functools.partial(_gdn_body, chunk=chunk, n_valid=n_valid),
        grid=(batch, nc),
        in_specs=[
            pl.BlockSpec((chunk, 4 * GDN_WIDTH), lambda b, c: (b * nc + c, 0)),
            pl.BlockSpec((SUBLANES, 4 * GDN_WIDTH), lambda b, c: (jnp.maximum((b * nc + c) * rb - 1, 0), 0)),
            pl.BlockSpec((1, SUBLANES, W3), lambda b, c: (b, 0, 0)),
            pl.BlockSpec((chunk, LANES), lambda b, c: (b * nc + c, 0)),
            pl.BlockSpec((1, GDN_HEADS, GDN_HD, GDN_HD), lambda b, c: (b, 0, 0, 0)),
            pl.BlockSpec((CONV_W, W3), lambda b, c: (0, 0)),
            pl.BlockSpec((1, LANES), lambda b, c: (0, 0)),
            pl.BlockSpec((1, LANES), lambda b, c: (0, 0)),
            pl.BlockSpec((1, GDN_HD), lambda b, c: (0, 0)),
        ],
        out_specs=[
            pl.BlockSpec((chunk, GDN_WIDTH), lambda b, c: (b * nc + c, 0)),
            pl.BlockSpec((1, GDN_HEADS, GDN_HD, GDN_HD), lambda b, c: (b, 0, 0, 0)),
        ],
        out_shape=[
            jax.ShapeDtypeStruct((batch * seq, GDN_WIDTH), BF16),
            jax.ShapeDtypeStruct((batch, GDN_HEADS, GDN_HD, GDN_HD), F32),
        ],
        scratch_shapes=[pltpu.VMEM((SUBLANES + chunk, W3), F32)],
        compiler_params=_cparams("parallel", "arbitrary"),
        name="gdn",
    )(u_a, u_a, state8, u_bd, s0, w_conv.astype(F32), alog, dtb, gdn_norm.reshape(1, GDN_HD).astype(F32))


def _flash_body(qi_ref, ki_ref, lam_ref, q_ref, k_ref, vt_ref, dn_ref, o_ref, qs_ref, m_ref, l_ref, acc_ref,
                *, tq, sub):
    step = pl.program_id(2)
    qi = qi_ref[step]
    ki = ki_ref[step]

    @pl.when(ki == 0)
    def _():
        q = q_ref[...]
        lane = lax.broadcasted_iota(jnp.int32, q.shape, 1)
        zero = jnp.zeros_like(q)
        qs_ref[0:tq, :] = jnp.where(lane < DIFF_HD, q, zero)
        qs_ref[tq:2 * tq, :] = jnp.where(lane >= DIFF_HD, q, zero)
        m_ref[...] = jnp.full(m_ref.shape, NEG_INIT, F32)
        l_ref[...] = jnp.zeros(l_ref.shape, F32)
        acc_ref[...] = jnp.zeros(acc_ref.shape, F32)

    def update(masked):
        k = k_ref[...]
        vt = vt_ref[...]
        for j in range(2 * tq // sub):
            cs = slice(j * sub, (j + 1) * sub)
            s = lax.dot_general(k, qs_ref[cs, :], (((1,), (1,)), ((), ())),
                                preferred_element_type=F32)
            if masked:
                key = lax.broadcasted_iota(jnp.int32, s.shape, 0)
                qry = lax.rem(lax.broadcasted_iota(jnp.int32, s.shape, 1) + j * sub, tq)
                s = jnp.where(key <= qry, s, NEG_INIT)
            m_prev = m_ref[:, cs]
            m_new = jnp.maximum(m_prev, jnp.max(s, axis=0, keepdims=True))
            alpha = jnp.exp2(m_prev - m_new)
            p = jnp.exp2(s - m_new)
            l_ref[:, cs] = alpha * l_ref[:, cs] + jnp.sum(p, axis=0, keepdims=True)
            acc_ref[:, cs] = alpha * acc_ref[:, cs] + jnp.dot(vt, p.astype(BF16), preferred_element_type=F32)
            m_ref[:, cs] = m_new

    @pl.when(ki < qi)
    def _():
        update(False)

    @pl.when(ki == qi)
    def _():
        update(True)
        lam = lam_ref[0]
        inv = 1.0 / l_ref[...]
        o = acc_ref[:, 0:tq] * inv[:, 0:tq] - lam * (acc_ref[:, tq:2 * tq] * inv[:, tq:2 * tq])
        o = o * lax.rsqrt(jnp.mean(o * o, axis=0, keepdims=True) + RMS_EPS)
        o_ref[...] = (o.T * dn_ref[...] * (1.0 - LAMBDA_INIT)).astype(o_ref.dtype)


def diff_attn_prompt(q, k, vt, lam, diff_norm, batch, seq):
    tq = _tile(seq, FLASH_TILE)
    nq = seq // tq
    sub = min(FLASH_SUB, 2 * tq)
    pairs = [(i, j) for i in range(nq) for j in range(i + 1)]
    qi_tab = jnp.asarray([p[0] for p in pairs], jnp.int32)
    ki_tab = jnp.asarray([p[1] for p in pairs], jnp.int32)
    grid_spec = pltpu.PrefetchScalarGridSpec(
        num_scalar_prefetch=3,
        grid=(batch, DIFF_HEADS, len(pairs)),
        in_specs=[
            pl.BlockSpec((tq, DIFF_VD), lambda b, h, s, qt, kt, lam: (b * nq + qt[s], h)),
            pl.BlockSpec((tq, DIFF_VD), lambda b, h, s, qt, kt, lam: (b * nq + kt[s], h)),
            pl.BlockSpec((DIFF_VD, tq), lambda b, h, s, qt, kt, lam: (h, b * nq + kt[s])),
            pl.BlockSpec((1, DIFF_VD), lambda b, h, s, qt, kt, lam: (0, 0)),
        ],
        out_specs=pl.BlockSpec((tq, DIFF_VD), lambda b, h, s, qt, kt, lam: (b * nq + qt[s], h)),
        scratch_shapes=[
            pltpu.VMEM((2 * tq, DIFF_VD), BF16),
            pltpu.VMEM((1, 2 * tq), F32),
            pltpu.VMEM((1, 2 * tq), F32),
            pltpu.VMEM((DIFF_VD, 2 * tq), F32),
        ],
    )
    return pl.pallas_call(
        functools.partial(_flash_body, tq=tq, sub=sub),
        grid_spec=grid_spec,
        out_shape=jax.ShapeDtypeStruct((batch * seq, DIFF_WIDTH), BF16),
        compiler_params=_cparams("parallel", "parallel", "arbitrary"),
        name="diff_attn_prompt",
    )(qi_tab, ki_tab, lam.reshape(1), q, k, vt, diff_norm.reshape(1, DIFF_VD).astype(F32))


TOK_PAD = SUBLANES
QROWS = 2 * DIFF_HEADS * TOK_PAD
PAGE_ROWS = PAGE_SIZE * DIFF_HEADS


def _paged_body(pt_ref, lam_ref, q_ref, *rest, n_tok, pp):
    kp_refs = rest[0:pp]
    vp_refs = rest[pp:2 * pp]
    kn_ref, vn_ref, dn_ref, o_ref, m_ref, l_ref, acc_ref = rest[2 * pp:]
    p = pl.program_id(1)
    npg = pl.num_programs(1)
    hshift = (2 * TOK_PAD).bit_length() - 1
    kshift = DIFF_HEADS.bit_length() - 1

    @pl.when(p == 0)
    def _():
        m_ref[...] = jnp.full(m_ref.shape, NEG_INIT, F32)
        l_ref[...] = jnp.zeros(l_ref.shape, F32)
        acc_ref[...] = jnp.zeros(acc_ref.shape, F32)

    def scores(kb, extra_mask):
        s = lax.dot_general(q_ref[0], kb.astype(BF16), (((1,), (1,)), ((), ())),
                            preferred_element_type=F32)
        r = lax.broadcasted_iota(jnp.int32, s.shape, 0)
        c = lax.broadcasted_iota(jnp.int32, s.shape, 1)
        ok = jnp.bitwise_and(c, DIFF_HEADS - 1) == jnp.right_shift(r, hshift)
        if extra_mask is not None:
            ok = jnp.logical_and(ok, extra_mask(r, c))
        return jnp.where(ok, s, NEG_INIT)

    def update(ss, vbs):
        m_prev = m_ref[...]
        m_new = m_prev
        for s in ss:
            m_new = jnp.maximum(m_new, jnp.max(s, axis=1, keepdims=True))
        alpha = jnp.exp(m_prev - m_new)
        l_new = alpha * l_ref[...]
        acc = alpha * acc_ref[...]
        for s, vb in zip(ss, vbs):
            pr = jnp.exp(s - m_new)
            l_new = l_new + jnp.sum(pr, axis=1, keepdims=True)
            acc = acc + jnp.dot(pr.astype(BF16), vb.astype(BF16), preferred_element_type=F32)
        l_ref[...] = l_new
        acc_ref[...] = acc
        m_ref[...] = m_new

    update([scores(kp_refs[j][0], None) for j in range(pp)], [vp_refs[j][0] for j in range(pp)])

    @pl.when(p == npg - 1)
    def _():
        def causal(r, c):
            tok = jnp.right_shift(c, kshift)
            return jnp.logical_and(tok <= jnp.bitwise_and(r, TOK_PAD - 1), tok < n_tok)

        update([scores(kn_ref[0], causal)], [vn_ref[0]])
        lam = lam_ref[0]
        inv = 1.0 / l_ref[...]
        for h in range(DIFF_HEADS):
            r0 = h * 2 * TOK_PAD
            r1 = r0 + TOK_PAD
            o1 = acc_ref[r0:r0 + TOK_PAD, :] * inv[r0:r0 + TOK_PAD]
            o2 = acc_ref[r1:r1 + TOK_PAD, :] * inv[r1:r1 + TOK_PAD]
            o = o1 - lam * o2
            o = o * lax.rsqrt(jnp.mean(o * o, axis=-1, keepdims=True) + RMS_EPS) * dn_ref[...]
            o_ref[0, h * TOK_PAD:(h + 1) * TOK_PAD, :] = (o * (1.0 - LAMBDA_INIT)).astype(o_ref.dtype)


def diff_attn_sample(q, k_new, v_new, lam, diff_norm, cache_k, cache_v, page_table, n_tok):
    bd = q.shape[0]
    n_pool = cache_k.shape[0]
    n_pages = page_table.shape[1]
    pp = PAGES_PER_STEP
    while n_pages % pp:
        pp //= 2
    qh = q.reshape(bd, n_tok, DIFF_HEADS, 2, DIFF_HD)
    qh = jnp.pad(qh, ((0, 0), (0, TOK_PAD - n_tok), (0, 0), (0, 0), (0, 0)))
    qh = jnp.transpose(qh, (0, 2, 3, 1, 4))
    eye_m = jnp.eye(2, dtype=q.dtype)
    qrows = jnp.einsum('bhctd,cx->bhctxd', qh, eye_m).reshape(bd, QROWS, DIFF_VD).astype(BF16)
    new_rows = TOK_PAD * DIFF_HEADS
    padn = ((0, 0), (0, new_rows - n_tok * DIFF_HEADS), (0, 0))
    kn = jnp.pad(k_new.reshape(bd, n_tok * DIFF_HEADS, DIFF_VD), padn)
    vn = jnp.pad(v_new.reshape(bd, n_tok * DIFF_HEADS, DIFF_VD), padn)
    ck = cache_k.reshape(n_pool, PAGE_ROWS, DIFF_VD)
    cv = cache_v.reshape(n_pool, PAGE_ROWS, DIFF_VD)

    def page_spec(j):
        return pl.BlockSpec((1, PAGE_ROWS, DIFF_VD), lambda b, p, pt, lam: (pt[b, p * pp + j], 0, 0))

    grid_spec = pltpu.PrefetchScalarGridSpec(
        num_scalar_prefetch=2,
        grid=(bd, n_pages // pp),
        in_specs=(
            [pl.BlockSpec((1, QROWS, DIFF_VD), lambda b, p, pt, lam: (b, 0, 0))]
            + [page_spec(j) for j in range(pp)]
            + [page_spec(j) for j in range(pp)]
            + [pl.BlockSpec((1, new_rows, DIFF_VD), lambda b, p, pt, lam: (b, 0, 0)),
               pl.BlockSpec((1, new_rows, DIFF_VD), lambda b, p, pt, lam: (b, 0, 0)),
               pl.BlockSpec((1, DIFF_VD), lambda b, p, pt, lam: (0, 0))]
        ),
        out_specs=pl.BlockSpec((1, new_rows, DIFF_VD), lambda b, p, pt, lam: (b, 0, 0)),
        scratch_shapes=[
            pltpu.VMEM((QROWS, 1), F32),
            pltpu.VMEM((QROWS, 1), F32),
            pltpu.VMEM((QROWS, DIFF_VD), F32),
        ],
    )
    out = pl.pallas_call(
        functools.partial(_paged_body, n_tok=n_tok, pp=pp),
        grid_spec=grid_spec,
        out_shape=jax.ShapeDtypeStruct((bd, new_rows, DIFF_VD), BF16),
        compiler_params=_cparams("parallel", "arbitrary"),
        name="diff_attn_sample",
    )(page_table, lam.reshape(1), qrows, *([ck] * pp), *([cv] * pp), kn, vn,
      diff_norm.reshape(1, DIFF_VD).astype(F32))
    out = out.reshape(bd, DIFF_HEADS, TOK_PAD, DIFF_VD)[:, :, :n_tok]
    return jnp.transpose(out, (0, 2, 1, 3)).reshape(bd, n_tok, DIFF_WIDTH)


def _mem_body(q_ref, k_ref, v_ref, o_ref):
    for h in range(MEM_HEADS):
        sl = slice(h * MEM_HD, (h + 1) * MEM_HD)
        s = lax.dot_general(q_ref[0, :, sl], k_ref[0, :, sl], (((1,), (1,)), ((), ())),
                            preferred_element_type=F32) * (MEM_HD ** -0.5)
        s = s - jnp.max(s, axis=1, keepdims=True)
        e = jnp.exp(s)
        p = e / jnp.sum(e, axis=1, keepdims=True)
        o_ref[0, :, sl] = jnp.dot(p.astype(BF16), v_ref[0, :, sl], preferred_element_type=F32).astype(o_ref.dtype)


def mem_attn(q, mem_k, mem_v):
    b, t, _ = q.shape
    mlen = mem_k.shape[1]
    tq = _tile(t, 512)
    return pl.pallas_call(
        _mem_body,
        grid=(b, t // tq),
        in_specs=[
            pl.BlockSpec((1, tq, MEM_WIDTH), lambda i, j: (i, j, 0)),
            pl.BlockSpec((1, mlen, MEM_WIDTH), lambda i, j: (i, 0, 0)),
            pl.BlockSpec((1, mlen, MEM_WIDTH), lambda i, j: (i, 0, 0)),
        ],
        out_specs=pl.BlockSpec((1, tq, MEM_WIDTH), lambda i, j: (i, j, 0)),
        out_shape=jax.ShapeDtypeStruct((b, t, MEM_WIDTH), BF16),
        compiler_params=_cparams("parallel", "arbitrary"),
        name="mem_attn",
    )(q, mem_k, mem_v)


def _norm_router_body(x_ref, g_ref, wr_ref, br_ref, h_ref, rt_ref):
    x = x_ref[...]
    hn = x * lax.rsqrt(jnp.mean(x * x, axis=-1, keepdims=True) + RMS_EPS) * g_ref[...]
    h_ref[...] = hn.astype(h_ref.dtype)
    lg = _dot3(hn, wr_ref[...]) + br_ref[...]
    lane = lax.broadcasted_iota(jnp.int32, lg.shape, 1)
    lanef = lane.astype(F32)
    big = float(LANES)

    def first_argmax(vals, mask):
        top = jnp.max(jnp.where(mask, vals, NEG_INIT), axis=1, keepdims=True)
        idx = jnp.min(jnp.where(jnp.logical_and(mask, vals == top), lanef, big), axis=1, keepdims=True)
        return top, idx

    gmask = lane < N_GROUPS
    gtop, gidx = first_argmax(lg, gmask)
    g_w = 1.0 / jnp.sum(jnp.where(gmask, jnp.exp(lg - gtop), 0.0), axis=1, keepdims=True)
    first = N_GROUPS + EXPERTS_PER_GROUP * gidx.astype(jnp.int32)
    emask = jnp.logical_and(lane >= first, lane < first + EXPERTS_PER_GROUP)
    e1, i1 = first_argmax(lg, emask)
    e2, i2 = first_argmax(lg, jnp.logical_and(emask, lanef != i1))
    ez = jnp.sum(jnp.where(emask, jnp.exp(lg - e1), 0.0), axis=1, keepdims=True)
    p1 = 1.0 / ez
    p2 = jnp.exp(e2 - e1) / ez
    psum = p1 + p2
    rt_ref[...] = jnp.where(lane == 0, i1 - N_GROUPS,
                            jnp.where(lane == 1, i2 - N_GROUPS,
                                      jnp.where(lane == 2, g_w * (p1 / psum),
                                                jnp.where(lane == 3, g_w * (p2 / psum), 0.0))))


def norm_router(x, g, w_router, b_router):
    m, d = x.shape
    tm = _tile(m, 512)
    return pl.pallas_call(
        _norm_router_body,
        grid=(m // tm,),
        in_specs=[
            pl.BlockSpec((tm, d), lambda i: (i, 0)),
            pl.BlockSpec((1, d), lambda i: (0, 0)),
            pl.BlockSpec((d, LANES), lambda i: (0, 0)),
            pl.BlockSpec((1, LANES), lambda i: (0, 0)),
        ],
        out_specs=[pl.BlockSpec((tm, d), lambda i: (i, 0)), pl.BlockSpec((tm, LANES), lambda i: (i, 0))],
        out_shape=[jax.ShapeDtypeStruct((m, d), BF16), jax.ShapeDtypeStruct((m, LANES), F32)],
        compiler_params=_cparams("parallel"),
        name="norm_router",
    )(x, g.reshape(1, d).astype(F32), w_router, b_router)


def _expert_body(te_ref, tv_ref, x_ref, wg_ref, wu_ref, wd_ref, o_ref):
    i = pl.program_id(0)

    @pl.when(tv_ref[i] != 0)
    def _():
        x = x_ref[...]
        a = jnp.dot(x, wg_ref[0].astype(BF16), preferred_element_type=F32)
        b = jnp.dot(x, wu_ref[0].astype(BF16), preferred_element_type=F32)
        hid = (a * jax.nn.sigmoid(a)) * b
        o_ref[...] = jnp.dot(hid.astype(BF16), wd_ref[0].astype(BF16), preferred_element_type=F32)

    @pl.when(tv_ref[i] == 0)
    def _():
        o_ref[...] = jnp.zeros(o_ref.shape, o_ref.dtype)


def grouped_experts(xs, tile_expert, tile_valid, w_gate, w_up, w_down, tm):
    rows, d = xs.shape
    nt = rows // tm
    ff = w_gate.shape[2]
    grid_spec = pltpu.PrefetchScalarGridSpec(
        num_scalar_prefetch=2,
        grid=(nt,),
        in_specs=[
            pl.BlockSpec((tm, d), lambda i, te, tv: (i, 0)),
            pl.BlockSpec((1, d, ff), lambda i, te, tv: (te[i], 0, 0)),
            pl.BlockSpec((1, d, ff), lambda i, te, tv: (te[i], 0, 0)),
            pl.BlockSpec((1, ff, d), lambda i, te, tv: (te[i], 0, 0)),
        ],
        out_specs=pl.BlockSpec((tm, d), lambda i, te, tv: (i, 0)),
    )
    return pl.pallas_call(
        _expert_body,
        grid_spec=grid_spec,
        out_shape=jax.ShapeDtypeStruct((rows, d), F32),
        compiler_params=_cparams("arbitrary"),
        name="grouped_experts",
    )(tile_expert, tile_valid, xs, w_gate, w_up, w_down)


def _final_body(x_ref, ya_ref, yb_ref, rt_ref, g_ref, o_ref):
    rt = rt_ref[...]
    x = x_ref[...] + (_col(rt, 2) * ya_ref[...] + _col(rt, 3) * yb_ref[...])
    o_ref[...] = x * lax.rsqrt(jnp.mean(x * x, axis=-1, keepdims=True) + RMS_EPS) * g_ref[...]


def final_norm(x, ya, yb, route, g):
    m, d = x.shape
    tm = _tile(m, 512)
    spec = pl.BlockSpec((tm, d), lambda i: (i, 0))
    return pl.pallas_call(
        _final_body,
        grid=(m // tm,),
        in_specs=[spec, spec, spec, pl.BlockSpec((tm, LANES), lambda i: (i, 0)),
                  pl.BlockSpec((1, d), lambda i: (0, 0))],
        out_specs=spec,
        out_shape=jax.ShapeDtypeStruct((m, d), F32),
        compiler_params=_cparams("parallel"),
        name="final_norm",
    )(x, ya, yb, route, g.reshape(1, d).astype(F32))


MOE_TILE = 256


def moe_and_final(x_list, norm_ffn, w_rg, b_rg, w_re, b_re, w_gate, w_up, w_down, norm_final):
    d = x_list[0].shape[1]
    w_router = jnp.zeros((d, LANES), F32).at[:, 0:N_GROUPS].set(w_rg.astype(F32))
    w_router = w_router.at[:, N_GROUPS:N_GROUPS + N_EXPERTS].set(w_re.astype(F32))
    b_router = jnp.zeros((1, LANES), F32).at[0, 0:N_GROUPS].set(b_rg.astype(F32))
    b_router = b_router.at[0, N_GROUPS:N_GROUPS + N_EXPERTS].set(b_re.astype(F32))
    hs, rts = [], []
    for x in x_list:
        hn, rt = norm_router(x, norm_ffn, w_router, b_router)
        hs.append(hn)
        rts.append(rt)
    hn = jnp.concatenate(hs, axis=0)
    n = hn.shape[0]
    expert_id = jnp.concatenate([rt[:, 0:TOP_K_INNER] for rt in rts], axis=0).astype(jnp.int32)
    tm = MOE_TILE
    flat_e = expert_id.reshape(-1)
    na = flat_e.shape[0]
    onehot = (flat_e[:, None] == jnp.arange(N_EXPERTS, dtype=jnp.int32)[None, :]).astype(jnp.int32)
    csum = jnp.cumsum(onehot, axis=0)
    counts = csum[-1]
    padded = ((counts + tm - 1) // tm) * tm
    pad_end = jnp.cumsum(padded)
    pad_start = pad_end - padded
    dest = jnp.sum(onehot * (csum - 1 + pad_start[None, :]), axis=1)
    nt = (na + tm - 1) // tm + N_EXPERTS
    rows = nt * tm
    row_tok = jnp.zeros((rows,), jnp.int32).at[dest].set(jnp.arange(na, dtype=jnp.int32) // TOP_K_INNER)
    tile_start = jnp.arange(nt, dtype=jnp.int32) * tm
    tile_expert = jnp.minimum(jnp.sum((pad_end[None, :] <= tile_start[:, None]).astype(jnp.int32), axis=1),
                              N_EXPERTS - 1)
    tile_valid = (tile_start < pad_end[-1]).astype(jnp.int32)
    xs = jnp.take(hn, row_tok, axis=0)
    ys = grouped_experts(xs, tile_expert, tile_valid, w_gate, w_up, w_down, tm)
    dest2 = dest.reshape(n, TOP_K_INNER)
    ya = jnp.take(ys, dest2[:, 0], axis=0)
    yb = jnp.take(ys, dest2[:, 1], axis=0)
    outs, o = [], 0
    for x, rt in zip(x_list, rts):
        m = x.shape[0]
        outs.append(final_norm(x, ya[o:o + m], yb[o:o + m], rt, norm_final))
        o += m
    return outs


def _rope_tables(pos, rows):
    half = ROT_DIM // 2
    inv_freq = ROPE_THETA ** (-jnp.arange(0, ROT_DIM, 2, dtype=F32) / ROT_DIM)
    ang = pos.astype(F32)[:, None] * inv_freq[None, :]
    cos, sin = jnp.cos(ang), jnp.sin(ang)
    t = pos.shape[0]
    one = jnp.ones((t, DIFF_HD - ROT_DIM), F32)
    zero = jnp.zeros((t, DIFF_HD - ROT_DIM), F32)
    zh = jnp.zeros((t, half), F32)
    c = jnp.concatenate([cos, cos, one], axis=1)
    s1 = jnp.concatenate([-sin, zh, zero], axis=1)
    s2 = jnp.concatenate([zh, sin, zero], axis=1)
    rep = rows // t
    tile = lambda a: jnp.tile(jnp.concatenate([a, a], axis=1), (rep, 1))
    return tile(c), tile(s1), tile(s2)


def _prep_weights(p):
    w_in = p['w_in']
    o = 4 * GDN_WIDTH
    ob = o + 2 * GDN_HEADS
    d = w_in.shape[0]
    w = {}
    w['a'] = w_in[:, 0:o].astype(BF16)
    w['bd'] = jnp.zeros((d, LANES), BF16).at[:, 0:2 * GDN_HEADS].set(w_in[:, o:ob].astype(BF16))
    w['dq'] = w_in[:, ob:ob + DIFF_WIDTH].astype(BF16)
    w['dk'] = w_in[:, ob + DIFF_WIDTH:ob + 2 * DIFF_WIDTH].astype(BF16)
    w['dv'] = w_in[:, ob + 2 * DIFF_WIDTH:ob + 3 * DIFF_WIDTH].astype(BF16)
    w['g'] = w_in[:, ob + 3 * DIFF_WIDTH:].astype(BF16)
    for name in ('w_branch_a', 'w_branch_b', 'w_out', 'w_mq', 'w_mk', 'w_mv', 'w_mo'):
        w[name] = p[name].astype(BF16)
    return w


def _mixers(x, batch, seq, pos, conv_state, delta_state, mem_k, mem_v, p, w, lam, sample_ctx):
    m, d = x.shape
    h = rmsnorm_rows(x, p['norm_mix'])
    u_a = matmul(h, w['a'])
    u_bd = matmul(h, w['bd'])
    u_g = matmul(h, w['g'])
    tm = _tile(m, 1024)
    tables = _rope_tables(pos, max(tm, seq))
    d_v = matmul(h, w['dv'])
    d_k = matmul(h, w['dk'], rope=tables)
    chunk = min(GDN_CHUNK, seq)
    seq_pad = seq
    if chunk % SUBLANES:
        chunk = SUBLANES
        seq_pad = SUBLANES
        padr = lambda a: jnp.pad(a.reshape(batch, seq, -1), ((0, 0), (0, seq_pad - seq), (0, 0))).reshape(
            batch * seq_pad, -1)
        u_a_g, u_bd_g = padr(u_a), padr(u_bd)
    else:
        u_a_g, u_bd_g = u_a, u_bd
    state8 = jnp.pad(conv_state.astype(F32), ((0, 0), (SUBLANES - (CONV_W - 1), 0), (0, 0)))
    o_a, new_delta = gdn(u_a_g, u_bd_g, state8, delta_state.astype(F32), p['w_conv'], p['a_log'], p['dt_bias'],
                         p['gdn_norm'], batch, seq_pad, chunk, min(seq, chunk))
    if seq_pad != seq:
        o_a = o_a.reshape(batch, seq_pad, GDN_WIDTH)[:, :seq].reshape(m, GDN_WIDTH)
    full = jnp.concatenate([conv_state.astype(F32), u_a[:, 0:3 * GDN_WIDTH].reshape(batch, seq, -1)], axis=1)
    new_conv = full[:, seq:]
    if sample_ctx is None:
        d_q = matmul(h, w['dq'], out_dtype=BF16, rope=tables, scale=DIFF_HD ** -0.5 * LOG2E)
        o_b = diff_attn_prompt(d_q, d_k.astype(BF16), d_v.astype(BF16).T, lam, p['diff_norm'], batch, seq)
    else:
        cache_k, cache_v, page_table = sample_ctx
        d_q = matmul(h, w['dq'], rope=tables, scale=DIFF_HD ** -0.5)
        o_b = diff_attn_sample(d_q.reshape(batch, seq, -1), d_k.reshape(batch, seq, -1), d_v.reshape(batch, seq, -1),
                               lam, p['diff_norm'], cache_k, cache_v, page_table, seq)
        o_b = o_b[:, :seq].reshape(m, DIFF_WIDTH)
    mixed = merge_branches(o_a, o_b, w['w_branch_a'], w['w_branch_b'], u_g)
    x = matmul(mixed, w['w_out'], residual=x)
    hc = rmsnorm_rows(x, p['norm_cross'])
    mq = matmul(hc, w['w_mq'], out_dtype=BF16).reshape(batch, seq, MEM_WIDTH)
    if seq % SUBLANES:
        mq = jnp.pad(mq, ((0, 0), (0, SUBLANES - seq), (0, 0)))
    mo = mem_attn(mq, mem_k, mem_v)[:, :seq].reshape(m, MEM_WIDTH)
    x = matmul(mo, w['w_mo'], residual=x)
    return x, new_conv, new_delta, d_k, d_v


def kernel(x_prompt, x_sample, cache_k, cache_v, cache_mem_k, cache_mem_v, state_delta, state_conv, page_table, mem_prompt, norm_mix, w_in, w_conv, a_log, dt_bias, gdn_norm, lambda_q1, lambda_k1, lambda_q2, lambda_k2, diff_norm, w_branch_a, w_branch_b, w_out, norm_cross, norm_mem, w_mq, w_mk, w_mv, w_mo, norm_ffn, w_router_group, b_router_group, w_router_expert, b_router_expert, w_gate, w_up, w_down, norm_final):
    p = dict(norm_mix=norm_mix, w_in=w_in, w_conv=w_conv, a_log=a_log, dt_bias=dt_bias, gdn_norm=gdn_norm,
             diff_norm=diff_norm, w_branch_a=w_branch_a, w_branch_b=w_branch_b, w_out=w_out,
             norm_cross=norm_cross, w_mq=w_mq, w_mk=w_mk, w_mv=w_mv, w_mo=w_mo)
    bp, tp, d = x_prompt.shape
    bs, ts, _ = x_sample.shape
    past_len = page_table.shape[1] * PAGE_SIZE
    w = _prep_weights(p)
    lam = (jnp.exp(jnp.sum(lambda_q1.astype(F32) * lambda_k1.astype(F32)))
           - jnp.exp(jnp.sum(lambda_q2.astype(F32) * lambda_k2.astype(F32))) + LAMBDA_INIT)
    mlen = mem_prompt.shape[1]
    hm = rmsnorm_rows(mem_prompt.reshape(bp * mlen, d), norm_mem)
    mem_k_p = matmul(hm, w['w_mk'])
    mem_v_p = matmul(hm, w['w_mv'])
    conv0 = jnp.zeros((bp, CONV_W - 1, 3 * GDN_WIDTH), F32)
    delta0 = jnp.zeros((bp, GDN_HEADS, GDN_HD, GDN_HD), F32)
    xp, conv_p, delta_p, k_p, v_p = _mixers(
        x_prompt.reshape(bp * tp, d), bp, tp, jnp.arange(tp), conv0, delta0,
        mem_k_p.astype(BF16).reshape(bp, mlen, MEM_WIDTH), mem_v_p.astype(BF16).reshape(bp, mlen, MEM_WIDTH),
        p, w, lam, None)
    xs, conv_s, delta_s, k_s, v_s = _mixers(
        x_sample.reshape(bs * ts, d), bs, ts, past_len + jnp.arange(ts), state_conv, state_delta,
        cache_mem_k.astype(BF16).reshape(bs, -1, MEM_WIDTH), cache_mem_v.astype(BF16).reshape(bs, -1, MEM_WIDTH),
        p, w, lam, (cache_k, cache_v, page_table))
    yp, ys = moe_and_final([xp, xs], norm_ffn, w_router_group, b_router_group, w_router_expert, b_router_expert,
                           w_gate, w_up, w_down, norm_final)
    return (yp.reshape(bp, tp, d), ys.reshape(bs, ts, d),
            k_p.reshape(bp, tp, DIFF_HEADS, DIFF_VD), v_p.reshape(bp, tp, DIFF_HEADS, DIFF_VD),
            mem_k_p.reshape(bp, mlen, MEM_HEADS, MEM_HD), mem_v_p.reshape(bp, mlen, MEM_HEADS, MEM_HD),
            delta_p.astype(state_delta.dtype), conv_p.astype(x_prompt.dtype),
            k_s.reshape(bs, ts, DIFF_HEADS, DIFF_VD), v_s.reshape(bs, ts, DIFF_HEADS, DIFF_VD),
            delta_s.astype(state_delta.dtype), conv_s.astype(state_conv.dtype))
```

```python
import functools
import math

import jax
import jax.numpy as jnp
from jax import lax
from jax.experimental import pallas as pl
from jax.experimental.pallas import tpu as pltpu

F32 = jnp.float32
BF16 = jnp.bfloat16

GDN_HEADS = 8
GDN_HD = 128
GDN_WIDTH = GDN_HEADS * GDN_HD
CONV_W = 4
GDN_CHUNK = 64
DIFF_HEADS = 8
DIFF_HD = 64
DIFF_VD = 2 * DIFF_HD
DIFF_WIDTH = DIFF_HEADS * DIFF_VD
ROT_DIM = DIFF_HD // 4
ROPE_THETA = 500000.0
LAMBDA_INIT = 0.2
PAGE_SIZE = 128
MEM_HEADS = 4
MEM_HD = 128
MEM_WIDTH = MEM_HEADS * MEM_HD
N_GROUPS = 4
EXPERTS_PER_GROUP = 8
N_EXPERTS = N_GROUPS * EXPERTS_PER_GROUP
TOP_K_INNER = 2
RMS_EPS = 1e-6
L2_EPS = 1e-6
NEG_INIT = -1e30

LANES = 128
SUBLANES = 8
VMEM_LIMIT = 48 * 1024 * 1024
FLASH_TILE = 1024
FLASH_SUB = 2048
PAGES_PER_STEP = 8
LOG2E = math.log2(math.e)

def _cparams(*sem):
    return pltpu.CompilerParams(dimension_semantics=sem, vmem_limit_bytes=VMEM_LIMIT)


def _tile(n, pref):
    if n <= pref:
        return n
    t = pref
    while n % t:
        t //= 2
    return t


_NN = (((1,), (0,)), ((), ()))
_NT = (((1,), (1,)), ((), ()))
_TN = (((0,), (0,)), ((), ()))


def _dotg(a, b, dims=_NN, hp=False):
    d = lambda x, y: lax.dot_general(x, y, dims, preferred_element_type=F32)
    if not hp:
        return d(a.astype(BF16), b.astype(BF16))
    (ah, am), (bh, bm) = _split2(a), _split2(b)
    return d(ah, bh) + (d(ah, bm) + d(am, bh))


def _mm(a, b):
    return _dotg(a, b, _NN, hp=(a.dtype == F32 and b.dtype == F32))


def _split3(x):
    hi = x.astype(BF16)
    r = x - hi.astype(F32)
    mid = r.astype(BF16)
    lo = (r - mid.astype(F32)).astype(BF16)
    return hi, mid, lo


def _split2(x):
    hi = x.astype(BF16)
    return hi, (x - hi.astype(F32)).astype(BF16)


def _dot3s(a2, b2):
    (ah, am), (bh, bm) = a2, b2
    d = lambda x, y: jnp.dot(x, y, preferred_element_type=F32)
    return d(ah, bh) + (d(ah, bm) + d(am, bh))


def _dot3(a, b):
    return _dot3s(_split2(a), _split2(b))


def _rmsnorm_body(x_ref, g_ref, o_ref):
    x = x_ref[...]
    ms = jnp.mean(x * x, axis=-1, keepdims=True)
    o_ref[...] = (x * lax.rsqrt(ms + RMS_EPS) * g_ref[...]).astype(o_ref.dtype)


def rmsnorm_rows(x, g, out_dtype=BF16):
    m, d = x.shape
    tm = _tile(m, 512)
    return pl.pallas_call(
        _rmsnorm_body,
        grid=(m // tm,),
        in_specs=[pl.BlockSpec((tm, d), lambda i: (i, 0)), pl.BlockSpec((1, d), lambda i: (0, 0))],
        out_specs=pl.BlockSpec((tm, d), lambda i: (i, 0)),
        out_shape=jax.ShapeDtypeStruct((m, d), out_dtype),
        compiler_params=_cparams("parallel"),
        name="rmsnorm_rows",
    )(x, g.reshape(1, d).astype(F32))


def _mm_body(a_ref, w_ref, o_ref):
    o_ref[...] = _mm(a_ref[...], w_ref[...]).astype(o_ref.dtype)


def _mm_res_body(a_ref, w_ref, r_ref, o_ref):
    o_ref[...] = r_ref[...] + _mm(a_ref[...], w_ref[...])


def _mm_rope_body(a_ref, w_ref, c_ref, s1_ref, s2_ref, o_ref, *, scale, reps):
    o = _mm(a_ref[...], w_ref[...])
    tn = o.shape[1]
    c = jnp.tile(c_ref[...], (1, reps))
    s1 = jnp.tile(s1_ref[...], (1, reps))
    s2 = jnp.tile(s2_ref[...], (1, reps))
    r = o * c + pltpu.roll(o, tn - ROT_DIM // 2, 1) * s1 + pltpu.roll(o, ROT_DIM // 2, 1) * s2
    if scale != 1.0:
        r = r * scale
    o_ref[...] = r.astype(o_ref.dtype)


def matmul(a, w, out_dtype=F32, residual=None, rope=None, scale=1.0, tm_pref=1024, tn_pref=512):
    m, k = a.shape
    n = w.shape[1]
    tm = _tile(m, tm_pref)
    tn = _tile(n, tn_pref)
    in_specs = [pl.BlockSpec((tm, k), lambda i, j: (i, 0)), pl.BlockSpec((k, tn), lambda i, j: (0, j))]
    args = [a, w]
    if residual is not None:
        body = _mm_res_body
        in_specs.append(pl.BlockSpec((tm, tn), lambda i, j: (i, j)))
        args.append(residual)
    elif rope is not None:
        c, s1, s2 = rope
        nt = c.shape[0] // tm
        body = functools.partial(_mm_rope_body, scale=scale, reps=tn // LANES)
        tspec = pl.BlockSpec((tm, LANES), lambda i, j: (i % nt, 0))
        in_specs += [tspec, tspec, tspec]
        args += [c, s1, s2]
    else:
        body = _mm_body
    return pl.pallas_call(
        body,
        grid=(m // tm, n // tn),
        in_specs=in_specs,
        out_specs=pl.BlockSpec((tm, tn), lambda i, j: (i, j)),
        out_shape=jax.ShapeDtypeStruct((m, n), out_dtype),
        compiler_params=_cparams("parallel", "arbitrary"),
        name="matmul",
    )(*args)


def _merge_body(oa_ref, ob_ref, wa_ref, wb_ref, ga_ref, gb_ref, o_ref):
    br_a = _mm(oa_ref[...], wa_ref[...])
    br_b = _mm(ob_ref[...], wb_ref[...])
    o_ref[...] = (jax.nn.sigmoid(ga_ref[...]) * br_a + jax.nn.sigmoid(gb_ref[...]) * br_b).astype(o_ref.dtype)


def merge_branches(oa, ob, wa, wb, gates):
    m, ka = oa.shape
    n = wa.shape[1]
    tm = _tile(m, 1024)
    tn = _tile(n, 512)
    nb = n // tn
    return pl.pallas_call(
        _merge_body,
        grid=(m // tm, nb),
        in_specs=[
            pl.BlockSpec((tm, ka), lambda i, j: (i, 0)),
            pl.BlockSpec((tm, ob.shape[1]), lambda i, j: (i, 0)),
            pl.BlockSpec((ka, tn), lambda i, j: (0, j)),
            pl.BlockSpec((wb.shape[0], tn), lambda i, j: (0, j)),
            pl.BlockSpec((tm, tn), lambda i, j: (i, j)),
            pl.BlockSpec((tm, tn), lambda i, j: (i, j + nb)),
        ],
        out_specs=pl.BlockSpec((tm, tn), lambda i, j: (i, j)),
        out_shape=jax.ShapeDtypeStruct((m, n), oa.dtype),
        compiler_params=_cparams("parallel", "arbitrary"),
        name="merge_branches",
    )(oa, ob, wa, wb, gates, gates)


def _col(arr, idx):
    lane = lax.broadcasted_iota(jnp.int32, arr.shape, 1)
    return jnp.sum(jnp.where(lane == idx, arr, 0.0), axis=1, keepdims=True)


def _row(arr, idx):
    sub = lax.broadcasted_iota(jnp.int32, arr.shape, 0)
    return jnp.sum(jnp.where(sub == idx, arr, 0.0), axis=0, keepdims=True)


def _gdn_body(cur_ref, prev_ref, st8_ref, bd_ref, s0_ref, wconv_ref, alog_ref, dtb_ref, gnorm_ref,
              o_ref, s_ref, ext_ref, *, chunk, n_valid, hp):
    c = pl.program_id(1)
    C = chunk
    H = GDN_HEADS
    W3 = 3 * GDN_WIDTH

    @pl.when(c == 0)
    def _():
        s_ref[...] = s0_ref[...]
        ext_ref[0:SUBLANES, :] = st8_ref[0]

    @pl.when(c != 0)
    def _():
        ext_ref[0:SUBLANES, :] = prev_ref[:, 0:W3]

    ext_ref[SUBLANES:SUBLANES + C, :] = cur_ref[:, 0:W3]
    base = SUBLANES - (CONV_W - 1)
    acc = ext_ref[base:base + C, :] * wconv_ref[0:1, :]
    for i in range(1, CONV_W):
        acc = acc + ext_ref[base + i:base + i + C, :] * wconv_ref[i:i + 1, :]
    qkv = acc * jax.nn.sigmoid(acc)

    bd = bd_ref[...]
    lane = lax.broadcasted_iota(jnp.int32, bd.shape, 1)
    beta_all = jax.nn.sigmoid(bd)
    xg = bd + dtb_ref[...]
    softplus = jnp.maximum(xg, 0.0) + jnp.log1p(jnp.exp(-jnp.abs(xg)))
    g_all = -jnp.exp(alog_ref[...]) * softplus
    gb = jnp.where(lane < H, beta_all, jnp.where(lane < 2 * H, g_all, 0.0))
    if n_valid < C:
        gb = jnp.where(lax.broadcasted_iota(jnp.int32, bd.shape, 0) < n_valid, gb, 0.0)
    ri = lax.broadcasted_iota(jnp.int32, (C, C), 0)
    ci = lax.broadcasted_iota(jnp.int32, (C, C), 1)
    causal = ri >= ci
    strict = ri > ci
    ltri = jnp.where(causal, 1.0, 0.0).astype(BF16)
    gc_cols = sum(jnp.dot(ltri, piece, preferred_element_type=F32) for piece in _split3(gb))
    gc_rows = gc_cols.T
    eye = jnp.where(ri == ci, 1.0, 0.0).astype(F32)

    heads = []
    for h in range(H):
        q = qkv[:, h * GDN_HD:(h + 1) * GDN_HD]
        k = qkv[:, GDN_WIDTH + h * GDN_HD:GDN_WIDTH + (h + 1) * GDN_HD]
        v = qkv[:, 2 * GDN_WIDTH + h * GDN_HD:2 * GDN_WIDTH + (h + 1) * GDN_HD]
        q = q * lax.rsqrt(jnp.sum(q * q, axis=-1, keepdims=True) + L2_EPS) * (GDN_HD ** -0.5)
        k = k * lax.rsqrt(jnp.sum(k * k, axis=-1, keepdims=True) + L2_EPS)
        beta_c = _col(gb, h)
        gc_c = _col(gc_cols, H + h)
        gc_r = _row(gc_rows, H + h)
        g_last = gc_c[C - 1:C, :]
        decay = jnp.where(causal, jnp.exp(jnp.where(causal, gc_c - gc_r, 0.0)), 0.0)
        kb = k * beta_c
        kk = _dotg(jnp.concatenate([kb, q], axis=0), k, _NT, hp)
        a_kk = jnp.where(strict, kk[0:C] * decay, 0.0)
        a_qk = jnp.where(causal, kk[C:2 * C] * decay, 0.0)
        rhs = jnp.concatenate([v * beta_c, kb * jnp.exp(gc_c)], axis=1)
        heads.append(dict(q=q, k=k, gc_c=gc_c, g_last=g_last, a_qk=a_qk, rhs=rhs, nmat=-a_kk, tinv=eye - a_kk))
    for hd in heads:
        hd['ns'] = _split2(hd['nmat'])
    span = 2
    while span < C:
        for hd in heads:
            hd['ns'] = _split2(_dot3s(hd['ns'], hd['ns']))
        for hd in heads:
            hd['tinv'] = hd['tinv'] + _dot3s(_split2(hd['tinv']), hd['ns'])
        span *= 2
    for hd in heads:
        hd['sol'] = _dot3s(_split2(hd['tinv']), _split2(hd['rhs']))
    for h, hd in enumerate(heads):
        sl = slice(h * GDN_HD, (h + 1) * GDN_HD)
        q, k, gc_c, g_last = hd['q'], hd['k'], hd['gc_c'], hd['g_last']
        u = hd['sol'][:, 0:GDN_HD]
        w = hd['sol'][:, GDN_HD:2 * GDN_HD]
        s = s_ref[0, h]
        ws = _dotg(jnp.concatenate([w, q * jnp.exp(gc_c)], axis=0), s, _NN, hp)
        v_new = u - ws[0:C]
        o = ws[C:2 * C] + _dotg(hd['a_qk'], v_new, _NN, hp)
        s_ref[0, h] = s * jnp.exp(g_last) + _dotg(k * jnp.exp(g_last - gc_c), v_new, _TN, hp)
        z = cur_ref[:, W3 + h * GDN_HD:W3 + (h + 1) * GDN_HD]
        o = o * lax.rsqrt(jnp.mean(o * o, axis=-1, keepdims=True) + RMS_EPS) * gnorm_ref[...]
        o = o * (z * jax.nn.sigmoid(z))
        o_ref[:, sl] = o.astype(o_ref.dtype)


def gdn(u_a, u_bd, state8, s0, w_conv, a_log, dt_bias, gdn_norm, batch, seq, chunk, n_valid, hp):
    nc = seq // chunk
    rb = chunk // SUBLANES
    alog = jnp.zeros((1, LANES), F32).at[0, GDN_HEADS:2 * GDN_HEADS].set(a_log.astype(F32))
    dtb = jnp.zeros((1, LANES), F32).at[0, GDN_HEADS:2 * GDN_HEADS].set(dt_bias.astype(F32))
    W3 = 3 * GDN_WIDTH
    return pl.pallas_call(
        functools.partial(_gdn_body, chunk=chunk, n_valid=n_valid, hp=hp),
        grid=(batch, nc),
        in_specs=[
            pl.BlockSpec((chunk, 4 * GDN_WIDTH), lambda b, c: (b * nc + c, 0)),
            pl.BlockSpec((SUBLANES, 4 * GDN_WIDTH), lambda b, c: (jnp.maximum((b * nc + c) * rb - 1, 0), 0)),
            pl.BlockSpec((1, SUBLANES, W3), lambda b, c: (b, 0, 0)),
            pl.BlockSpec((chunk, LANES), lambda b, c: (b * nc + c, 0)),
            pl.BlockSpec((1, GDN_HEADS, GDN_HD, GDN_HD), lambda b, c: (b, 0, 0, 0)),
            pl.BlockSpec((CONV_W, W3), lambda b, c: (0, 0)),
            pl.BlockSpec((1, LANES), lambda b, c: (0, 0)),
            pl.BlockSpec((1, LANES), lambda b, c: (0, 0)),
            pl.BlockSpec((1, GDN_HD), lambda b, c: (0, 0)),
        ],
        out_specs=[
            pl.BlockSpec((chunk, GDN_WIDTH), lambda b, c: (b * nc + c, 0)),
            pl.BlockSpec((1, GDN_HEADS, GDN_HD, GDN_HD), lambda b, c: (b, 0, 0, 0)),
        ],
        out_shape=[
            jax.ShapeDtypeStruct((batch * seq, GDN_WIDTH), F32 if hp else BF16),
            jax.ShapeDtypeStruct((batch, GDN_HEADS, GDN_HD, GDN_HD), F32),
        ],
        scratch_shapes=[pltpu.VMEM((SUBLANES + chunk, W3), F32)],
        compiler_params=_cparams("parallel", "arbitrary"),
        name="gdn",
    )(u_a, u_a, state8, u_bd, s0, w_conv.astype(F32), alog, dtb, gdn_norm.reshape(1, GDN_HD).astype(F32))


def _flash_body(qi_ref, ki_ref, lam_ref, q_ref, k_ref, vt_ref, dn_ref, o_ref, qs_ref, m_ref, l_ref, acc_ref,
                *, tq, sub):
    step = pl.program_id(2)
    qi = qi_ref[step]
    ki = ki_ref[step]

    @pl.when(ki == 0)
    def _():
        q = q_ref[...]
        lane = lax.broadcasted_iota(jnp.int32, q.shape, 1)
        zero = jnp.zeros_like(q)
        qs_ref[0:tq, :] = jnp.where(lane < DIFF_HD, q, zero)
        qs_ref[tq:2 * tq, :] = jnp.where(lane >= DIFF_HD, q, zero)
        m_ref[...] = jnp.full(m_ref.shape, NEG_INIT, F32)
        l_ref[...] = jnp.zeros(l_ref.shape, F32)
        acc_ref[...] = jnp.zeros(acc_ref.shape, F32)

    def update(masked):
        k = k_ref[...]
        vt = vt_ref[...]
        nsub = 2 * tq // sub
        css = [slice(j * sub, (j + 1) * sub) for j in range(nsub)]
        qs = [qs_ref[cs, :] for cs in css]
        m_prevs = [m_ref[:, cs] for cs in css]
        l_prevs = [l_ref[:, cs] for cs in css]
        acc_prevs = [acc_ref[:, cs] for cs in css]
        outs = []
        for j in range(nsub):
            s = lax.dot_general(k, qs[j], (((1,), (1,)), ((), ())), preferred_element_type=F32)
            if masked:
                key = lax.broadcasted_iota(jnp.int32, s.shape, 0)
                qry = lax.rem(lax.broadcasted_iota(jnp.int32, s.shape, 1) + j * sub, tq)
                s = jnp.where(key <= qry, s, NEG_INIT)
            m_new = jnp.maximum(m_prevs[j], jnp.max(s, axis=0, keepdims=True))
            alpha = jnp.exp2(m_prevs[j] - m_new)
            p = jnp.exp2(s - m_new)
            l_new = alpha * l_prevs[j] + jnp.sum(p, axis=0, keepdims=True)
            acc_new = alpha * acc_prevs[j] + jnp.dot(vt, p.astype(BF16), preferred_element_type=F32)
            outs.append((m_new, l_new, acc_new))
        for cs, (m_new, l_new, acc_new) in zip(css, outs):
            m_ref[:, cs] = m_new
            l_ref[:, cs] = l_new
            acc_ref[:, cs] = acc_new

    @pl.when(ki < qi)
    def _():
        update(False)

    @pl.when(ki == qi)
    def _():
        update(True)
        lam = lam_ref[0]
        inv = 1.0 / l_ref[...]
        o = acc_ref[:, 0:tq] * inv[:, 0:tq] - lam * (acc_ref[:, tq:2 * tq] * inv[:, tq:2 * tq])
        o = o * lax.rsqrt(jnp.mean(o * o, axis=0, keepdims=True) + RMS_EPS)
        o_ref[...] = (o.T * dn_ref[...] * (1.0 - LAMBDA_INIT)).astype(o_ref.dtype)


def diff_attn_prompt(q, k, vt, lam, diff_norm, batch, seq):
    tq = _tile(seq, FLASH_TILE)
    nq = seq // tq
    sub = min(FLASH_SUB, 2 * tq)
    pairs = [(i, j) for i in range(nq) for j in range(i + 1)]
    qi_tab = jnp.asarray([p[0] for p in pairs], jnp.int32)
    ki_tab = jnp.asarray([p[1] for p in pairs], jnp.int32)
    grid_spec = pltpu.PrefetchScalarGridSpec(
        num_scalar_prefetch=3,
        grid=(batch, DIFF_HEADS, len(pairs)),
        in_specs=[
            pl.BlockSpec((tq, DIFF_VD), lambda b, h, s, qt, kt, lam: (b * nq + qt[s], h)),
            pl.BlockSpec((tq, DIFF_VD), lambda b, h, s, qt, kt, lam: (b * nq + kt[s], h)),
            pl.BlockSpec((DIFF_VD, tq), lambda b, h, s, qt, kt, lam: (h, b * nq + kt[s])),
            pl.BlockSpec((1, DIFF_VD), lambda b, h, s, qt, kt, lam: (0, 0)),
        ],
        out_specs=pl.BlockSpec((tq, DIFF_VD), lambda b, h, s, qt, kt, lam: (b * nq + qt[s], h)),
        scratch_shapes=[
            pltpu.VMEM((2 * tq, DIFF_VD), BF16),
            pltpu.VMEM((1, 2 * tq), F32),
            pltpu.VMEM((1, 2 * tq), F32),
            pltpu.VMEM((DIFF_VD, 2 * tq), F32),
        ],
    )
    return pl.pallas_call(
        functools.partial(_flash_body, tq=tq, sub=sub),
        grid_spec=grid_spec,
        out_shape=jax.ShapeDtypeStruct((batch * seq, DIFF_WIDTH), BF16),
        compiler_params=_cparams("parallel", "parallel", "arbitrary"),
        name="diff_attn_prompt",
    )(qi_tab, ki_tab, lam.reshape(1), q, k, vt, diff_norm.reshape(1, DIFF_VD).astype(F32))


PAGE_ROWS = PAGE_SIZE * DIFF_HEADS


def _head_mask(shape, tp, extra_mask=None):
    r = lax.broadcasted_iota(jnp.int32, shape, 0)
    c = lax.broadcasted_iota(jnp.int32, shape, 1)
    ok = jnp.bitwise_and(c, DIFF_HEADS - 1) == jnp.right_shift(r, (2 * tp).bit_length() - 1)
    if extra_mask is not None:
        ok = jnp.logical_and(ok, extra_mask(r, c))
    return jnp.where(ok, 0.0, NEG_INIT).astype(F32)


def _paged_body(pt_ref, lam_ref, q_ref, *rest, n_tok, pp, tp):
    kp_refs = rest[0:pp]
    vp_refs = rest[pp:2 * pp]
    kn_ref, vn_ref, dn_ref, o_ref, m_ref, l_ref, acc_ref, bias_ref = rest[2 * pp:]
    p = pl.program_id(1)
    npg = pl.num_programs(1)
    kshift = DIFF_HEADS.bit_length() - 1

    @pl.when(p == 0)
    def _():
        m_ref[...] = jnp.full(m_ref.shape, NEG_INIT, F32)
        l_ref[...] = jnp.zeros(l_ref.shape, F32)
        acc_ref[...] = jnp.zeros(acc_ref.shape, F32)
        bias_ref[...] = _head_mask(bias_ref.shape, tp)

    qh, qm = _split2(q_ref[0])
    qcat = jnp.concatenate([qh, qh, qm], axis=1)

    def scores(kb, bias):
        kh, km = _split2(kb)
        kcat = jnp.concatenate([kh, km, kh], axis=1)
        return lax.dot_general(qcat, kcat, _NT, preferred_element_type=F32) + bias

    def weighted_values(pr, vb):
        ph, pm = _split2(pr)
        vh, vm = _split2(vb)
        wide = jnp.dot(ph, jnp.concatenate([vh, vm], axis=1), preferred_element_type=F32)
        return (wide[:, 0:DIFF_VD] + wide[:, DIFF_VD:2 * DIFF_VD]) + jnp.dot(pm, vh, preferred_element_type=F32)

    def update(ss, vbs):
        m_prev = m_ref[...]
        m_new = m_prev
        for s in ss:
            m_new = jnp.maximum(m_new, jnp.max(s, axis=1, keepdims=True))
        alpha = jnp.exp(m_prev - m_new)
        l_new = alpha * l_ref[...]
        acc = alpha * acc_ref[...]
        for s, vb in zip(ss, vbs):
            pr = jnp.exp(s - m_new)
            l_new = l_new + jnp.sum(pr, axis=1, keepdims=True)
            acc = acc + weighted_values(pr, vb)
        l_ref[...] = l_new
        acc_ref[...] = acc
        m_ref[...] = m_new

    bias = bias_ref[...]
    update([scores(kp_refs[j][0], bias) for j in range(pp)], [vp_refs[j][0] for j in range(pp)])

    @pl.when(p == npg - 1)
    def _():
        def causal(r, c):
            tok = jnp.right_shift(c, kshift)
            return jnp.logical_and(tok <= jnp.bitwise_and(r, tp - 1), tok < n_tok)

        update([scores(kn_ref[0], _head_mask((q_ref.shape[1], kn_ref.shape[1]), tp, causal))], [vn_ref[0]])
        lam = lam_ref[0]
        inv = 1.0 / l_ref[...]
        for h in range(DIFF_HEADS):
            r0 = h * 2 * tp
            if tp == SUBLANES:
                o1 = acc_ref[r0:r0 + tp, :] * inv[r0:r0 + tp]
                o2 = acc_ref[r0 + tp:r0 + 2 * tp, :] * inv[r0 + tp:r0 + 2 * tp]
            else:
                o1 = acc_ref[r0:r0 + SUBLANES, :] * inv[r0:r0 + SUBLANES]
                o2 = pltpu.roll(o1, tp, 0)
            o = o1 - lam * o2
            o = o * lax.rsqrt(jnp.mean(o * o, axis=-1, keepdims=True) + RMS_EPS) * dn_ref[...]
            o_ref[0, h * SUBLANES:(h + 1) * SUBLANES, :] = (o * (1.0 - LAMBDA_INIT)).astype(o_ref.dtype)


def diff_attn_sample(q, k_new, v_new, lam, diff_norm, cache_k, cache_v, page_table, n_tok):
    bd = q.shape[0]
    n_pool = cache_k.shape[0]
    n_pages = page_table.shape[1]
    assert n_tok <= SUBLANES
    tp = SUBLANES // 2 if n_tok <= SUBLANES // 2 else SUBLANES
    qrows_n = 2 * DIFF_HEADS * tp
    pp = PAGES_PER_STEP
    while n_pages % pp:
        pp //= 2
    qh = q.reshape(bd, n_tok, DIFF_HEADS, 2, DIFF_HD)
    qh = jnp.pad(qh, ((0, 0), (0, tp - n_tok), (0, 0), (0, 0), (0, 0)))
    qh = jnp.transpose(qh, (0, 2, 3, 1, 4))
    zeros = jnp.zeros_like(qh[:, :, 0])
    qrows = jnp.stack([jnp.concatenate([qh[:, :, 0], zeros], axis=-1),
                       jnp.concatenate([zeros, qh[:, :, 1]], axis=-1)], axis=2).reshape(bd, qrows_n, DIFF_VD)
    new_rows = tp * DIFF_HEADS
    out_rows = SUBLANES * DIFF_HEADS
    padn = ((0, 0), (0, new_rows - n_tok * DIFF_HEADS), (0, 0))
    kn = jnp.pad(k_new.reshape(bd, n_tok * DIFF_HEADS, DIFF_VD), padn)
    vn = jnp.pad(v_new.reshape(bd, n_tok * DIFF_HEADS, DIFF_VD), padn)
    ck = cache_k.reshape(n_pool, PAGE_ROWS, DIFF_VD)
    cv = cache_v.reshape(n_pool, PAGE_ROWS, DIFF_VD)

    def page_spec(j):
        return pl.BlockSpec((1, PAGE_ROWS, DIFF_VD), lambda b, p, pt, lam: (pt[b, p * pp + j], 0, 0))

    grid_spec = pltpu.PrefetchScalarGridSpec(
        num_scalar_prefetch=2,
        grid=(bd, n_pages // pp),
        in_specs=(
            [pl.BlockSpec((1, qrows_n, DIFF_VD), lambda b, p, pt, lam: (b, 0, 0))]
            + [page_spec(j) for j in range(pp)]
            + [page_spec(j) for j in range(pp)]
            + [pl.BlockSpec((1, new_rows, DIFF_VD), lambda b, p, pt, lam: (b, 0, 0)),
               pl.BlockSpec((1, new_rows, DIFF_VD), lambda b, p, pt, lam: (b, 0, 0)),
               pl.BlockSpec((1, DIFF_VD), lambda b, p, pt, lam: (0, 0))]
        ),
        out_specs=pl.BlockSpec((1, out_rows, DIFF_VD), lambda b, p, pt, lam: (b, 0, 0)),
        scratch_shapes=[
            pltpu.VMEM((qrows_n, 1), F32),
            pltpu.VMEM((qrows_n, 1), F32),
            pltpu.VMEM((qrows_n, DIFF_VD), F32),
            pltpu.VMEM((qrows_n, PAGE_ROWS), F32),
        ],
    )
    out = pl.pallas_call(
        functools.partial(_paged_body, n_tok=n_tok, pp=pp, tp=tp),
        grid_spec=grid_spec,
        out_shape=jax.ShapeDtypeStruct((bd, out_rows, DIFF_VD), F32),
        compiler_params=_cparams("parallel", "arbitrary"),
        name="diff_attn_sample",
    )(page_table, lam.reshape(1), qrows, *([ck] * pp), *([cv] * pp), kn, vn,
      diff_norm.reshape(1, DIFF_VD).astype(F32))
    out = out.reshape(bd, DIFF_HEADS, SUBLANES, DIFF_VD)[:, :, :n_tok]
    return jnp.transpose(out, (0, 2, 1, 3)).reshape(bd, n_tok, DIFF_WIDTH)


def _mem_body(q_ref, k_ref, v_ref, o_ref):
    hp = q_ref.dtype == F32
    for h in range(MEM_HEADS):
        sl = slice(h * MEM_HD, (h + 1) * MEM_HD)
        s = _dotg(q_ref[0, :, sl], k_ref[0, :, sl], _NT, hp) * (MEM_HD ** -0.5)
        s = s - jnp.max(s, axis=1, keepdims=True)
        e = jnp.exp(s)
        p = e / jnp.sum(e, axis=1, keepdims=True)
        o_ref[0, :, sl] = _dotg(p, v_ref[0, :, sl], _NN, hp).astype(o_ref.dtype)


def mem_attn(q, mem_k, mem_v):
    b, t, _ = q.shape
    mlen = mem_k.shape[1]
    tq = _tile(t, 512)
    return pl.pallas_call(
        _mem_body,
        grid=(b, t // tq),
        in_specs=[
            pl.BlockSpec((1, tq, MEM_WIDTH), lambda i, j: (i, j, 0)),
            pl.BlockSpec((1, mlen, MEM_WIDTH), lambda i, j: (i, 0, 0)),
            pl.BlockSpec((1, mlen, MEM_WIDTH), lambda i, j: (i, 0, 0)),
        ],
        out_specs=pl.BlockSpec((1, tq, MEM_WIDTH), lambda i, j: (i, j, 0)),
        out_shape=jax.ShapeDtypeStruct((b, t, MEM_WIDTH), q.dtype),
        compiler_params=_cparams("parallel", "arbitrary"),
        name="mem_attn",
    )(q, mem_k, mem_v)


def _norm_router_body(x_ref, g_ref, wr_ref, br_ref, h_ref, rt_ref):
    x = x_ref[...]
    hn = x * lax.rsqrt(jnp.mean(x * x, axis=-1, keepdims=True) + RMS_EPS) * g_ref[...]
    h_ref[...] = hn.astype(h_ref.dtype)
    lg = _dot3(hn, wr_ref[...]) + br_ref[...]
    lane = lax.broadcasted_iota(jnp.int32, lg.shape, 1)
    lanef = lane.astype(F32)
    big = float(LANES)

    def first_argmax(vals, mask):
        top = jnp.max(jnp.where(mask, vals, NEG_INIT), axis=1, keepdims=True)
        idx = jnp.min(jnp.where(jnp.logical_and(mask, vals == top), lanef, big), axis=1, keepdims=True)
        return top, idx

    gmask = lane < N_GROUPS
    gtop, gidx = first_argmax(lg, gmask)
    g_w = 1.0 / jnp.sum(jnp.where(gmask, jnp.exp(lg - gtop), 0.0), axis=1, keepdims=True)
    first = N_GROUPS + EXPERTS_PER_GROUP * gidx.astype(jnp.int32)
    emask = jnp.logical_and(lane >= first, lane < first + EXPERTS_PER_GROUP)
    e1, i1 = first_argmax(lg, emask)
    e2, i2 = first_argmax(lg, jnp.logical_and(emask, lanef != i1))
    ez = jnp.sum(jnp.where(emask, jnp.exp(lg - e1), 0.0), axis=1, keepdims=True)
    p1 = 1.0 / ez
    p2 = jnp.exp(e2 - e1) / ez
    psum = p1 + p2
    rt_ref[...] = jnp.where(lane == 0, i1 - N_GROUPS,
                            jnp.where(lane == 1, i2 - N_GROUPS,
                                      jnp.where(lane == 2, g_w * (p1 / psum),
                                                jnp.where(lane == 3, g_w * (p2 / psum), 0.0))))


def norm_router(x, g, w_router, b_router):
    m, d = x.shape
    tm = _tile(m, 512)
    return pl.pallas_call(
        _norm_router_body,
        grid=(m // tm,),
        in_specs=[
            pl.BlockSpec((tm, d), lambda i: (i, 0)),
            pl.BlockSpec((1, d), lambda i: (0, 0)),
            pl.BlockSpec((d, LANES), lambda i: (0, 0)),
            pl.BlockSpec((1, LANES), lambda i: (0, 0)),
        ],
        out_specs=[pl.BlockSpec((tm, d), lambda i: (i, 0)), pl.BlockSpec((tm, LANES), lambda i: (i, 0))],
        out_shape=[jax.ShapeDtypeStruct((m, d), BF16), jax.ShapeDtypeStruct((m, LANES), F32)],
        compiler_params=_cparams("parallel"),
        name="norm_router",
    )(x, g.reshape(1, d).astype(F32), w_router, b_router)


def _expert_body(te_ref, tv_ref, x_ref, wg_ref, wu_ref, wd_ref, o_ref):
    i = pl.program_id(0)

    @pl.when(tv_ref[i] != 0)
    def _():
        x = x_ref[...]
        a = jnp.dot(x, wg_ref[0].astype(BF16), preferred_element_type=F32)
        b = jnp.dot(x, wu_ref[0].astype(BF16), preferred_element_type=F32)
        hid = (a * jax.nn.sigmoid(a)) * b
        o_ref[...] = jnp.dot(hid.astype(BF16), wd_ref[0].astype(BF16), preferred_element_type=F32)

    @pl.when(tv_ref[i] == 0)
    def _():
        o_ref[...] = jnp.zeros(o_ref.shape, o_ref.dtype)


def grouped_experts(xs, tile_expert, tile_valid, w_gate, w_up, w_down, tm):
    rows, d = xs.shape
    nt = rows // tm
    ff = w_gate.shape[2]
    grid_spec = pltpu.PrefetchScalarGridSpec(
        num_scalar_prefetch=2,
        grid=(nt,),
        in_specs=[
            pl.BlockSpec((tm, d), lambda i, te, tv: (i, 0)),
            pl.BlockSpec((1, d, ff), lambda i, te, tv: (te[i], 0, 0)),
            pl.BlockSpec((1, d, ff), lambda i, te, tv: (te[i], 0, 0)),
            pl.BlockSpec((1, ff, d), lambda i, te, tv: (te[i], 0, 0)),
        ],
        out_specs=pl.BlockSpec((tm, d), lambda i, te, tv: (i, 0)),
    )
    return pl.pallas_call(
        _expert_body,
        grid_spec=grid_spec,
        out_shape=jax.ShapeDtypeStruct((rows, d), F32),
        compiler_params=_cparams("arbitrary"),
        name="grouped_experts",
    )(tile_expert, tile_valid, xs, w_gate, w_up, w_down)


def _final_body(x_ref, ya_ref, yb_ref, rt_ref, g_ref, o_ref):
    rt = rt_ref[...]
    x = x_ref[...] + (_col(rt, 2) * ya_ref[...] + _col(rt, 3) * yb_ref[...])
    o_ref[...] = x * lax.rsqrt(jnp.mean(x * x, axis=-1, keepdims=True) + RMS_EPS) * g_ref[...]


def final_norm(x, ya, yb, row0, route, g):
    m, d = x.shape
    tm = _tile(m, 512)
    assert row0 % tm == 0
    blk0 = row0 // tm
    spec = pl.BlockSpec((tm, d), lambda i: (i, 0))
    yspec = pl.BlockSpec((tm, d), lambda i: (i + blk0, 0))
    return pl.pallas_call(
        _final_body,
        grid=(m // tm,),
        in_specs=[spec, yspec, yspec, pl.BlockSpec((tm, LANES), lambda i: (i, 0)),
                  pl.BlockSpec((1, d), lambda i: (0, 0))],
        out_specs=spec,
        out_shape=jax.ShapeDtypeStruct((m, d), F32),
        compiler_params=_cparams("parallel"),
        name="final_norm",
    )(x, ya, yb, route, g.reshape(1, d).astype(F32))


MOE_TILE = 256


def moe_and_final(x_list, norm_ffn, w_rg, b_rg, w_re, b_re, w_gate, w_up, w_down, norm_final):
    d = x_list[0].shape[1]
    w_router = jnp.zeros((d, LANES), F32).at[:, 0:N_GROUPS].set(w_rg.astype(F32))
    w_router = w_router.at[:, N_GROUPS:N_GROUPS + N_EXPERTS].set(w_re.astype(F32))
    b_router = jnp.zeros((1, LANES), F32).at[0, 0:N_GROUPS].set(b_rg.astype(F32))
    b_router = b_router.at[0, N_GROUPS:N_GROUPS + N_EXPERTS].set(b_re.astype(F32))
    hs, rts = [], []
    for x in x_list:
        hn, rt = norm_router(x, norm_ffn, w_router, b_router)
        hs.append(hn)
        rts.append(rt)
    hn = jnp.concatenate(hs, axis=0)
    n = hn.shape[0]
    expert_id = jnp.concatenate([rt[:, 0:TOP_K_INNER] for rt in rts], axis=0).astype(jnp.int32)
    tm = MOE_TILE
    flat_e = expert_id.reshape(-1)
    na = flat_e.shape[0]
    onehot = (flat_e[:, None] == jnp.arange(N_EXPERTS, dtype=jnp.int32)[None, :]).astype(jnp.int32)
    csum = jnp.cumsum(onehot, axis=0)
    counts = csum[-1]
    padded = ((counts + tm - 1) // tm) * tm
    pad_end = jnp.cumsum(padded)
    pad_start = pad_end - padded
    dest = jnp.sum(onehot * (csum - 1 + pad_start[None, :]), axis=1)
    nt = (na + tm - 1) // tm + N_EXPERTS
    rows = nt * tm
    row_tok = jnp.zeros((rows,), jnp.int32).at[dest].set(jnp.arange(na, dtype=jnp.int32) // TOP_K_INNER)
    tile_start = jnp.arange(nt, dtype=jnp.int32) * tm
    tile_expert = jnp.minimum(jnp.sum((pad_end[None, :] <= tile_start[:, None]).astype(jnp.int32), axis=1),
                              N_EXPERTS - 1)
    tile_valid = (tile_start < pad_end[-1]).astype(jnp.int32)
    xs = hn.at[row_tok].get(mode='promise_in_bounds')
    ys = grouped_experts(xs, tile_expert, tile_valid, w_gate, w_up, w_down, tm)
    dest2 = dest.reshape(n, TOP_K_INNER)
    ya = ys.at[dest2[:, 0]].get(mode='promise_in_bounds')
    yb = ys.at[dest2[:, 1]].get(mode='promise_in_bounds')
    outs, o = [], 0
    for x, rt in zip(x_list, rts):
        outs.append(final_norm(x, ya, yb, o, rt, norm_final))
        o += x.shape[0]
    return outs


def _rope_tables(pos, rows):
    half = ROT_DIM // 2
    inv_freq = ROPE_THETA ** (-jnp.arange(0, ROT_DIM, 2, dtype=F32) / ROT_DIM)
    ang = pos.astype(F32)[:, None] * inv_freq[None, :]
    cos, sin = jnp.cos(ang), jnp.sin(ang)
    t = pos.shape[0]
    one = jnp.ones((t, DIFF_HD - ROT_DIM), F32)
    zero = jnp.zeros((t, DIFF_HD - ROT_DIM), F32)
    zh = jnp.zeros((t, half), F32)
    c = jnp.concatenate([cos, cos, one], axis=1)
    s1 = jnp.concatenate([-sin, zh, zero], axis=1)
    s2 = jnp.concatenate([zh, sin, zero], axis=1)
    rep = rows // t
    tile = lambda a: jnp.tile(jnp.concatenate([a, a], axis=1), (rep, 1))
    return tile(c), tile(s1), tile(s2)


def _prep_weights(p, dtype):
    w_in = p['w_in']
    o = 4 * GDN_WIDTH
    ob = o + 2 * GDN_HEADS
    d = w_in.shape[0]
    w = {}
    w['a'] = w_in[:, 0:o].astype(dtype)
    w['bd'] = jnp.zeros((d, LANES), dtype).at[:, 0:2 * GDN_HEADS].set(w_in[:, o:ob].astype(dtype))
    w['dq'] = w_in[:, ob:ob + DIFF_WIDTH].astype(dtype)
    w['dk'] = w_in[:, ob + DIFF_WIDTH:ob + 2 * DIFF_WIDTH].astype(dtype)
    w['dv'] = w_in[:, ob + 2 * DIFF_WIDTH:ob + 3 * DIFF_WIDTH].astype(dtype)
    w['g'] = w_in[:, ob + 3 * DIFF_WIDTH:].astype(dtype)
    for name in ('w_branch_a', 'w_branch_b', 'w_out', 'w_mq', 'w_mk', 'w_mv', 'w_mo'):
        w[name] = p[name].astype(dtype)
    return w


def _mixers(x, batch, seq, pos, conv_state, delta_state, mem_k, mem_v, p, w, lam, sample_ctx):
    m, d = x.shape
    hp = sample_ctx is not None
    od = F32 if hp else BF16
    h = rmsnorm_rows(x, p['norm_mix'], out_dtype=od)
    u_a = matmul(h, w['a'])
    u_bd = matmul(h, w['bd'])
    u_g = matmul(h, w['g'])
    tm = _tile(m, 1024)
    tables = _rope_tables(pos, max(tm, seq))
    d_v = matmul(h, w['dv'])
    d_k = matmul(h, w['dk'], rope=tables)
    chunk = min(GDN_CHUNK, seq)
    seq_pad = seq
    if chunk % SUBLANES:
        chunk = SUBLANES
        seq_pad = SUBLANES
        padr = lambda a: jnp.pad(a.reshape(batch, seq, -1), ((0, 0), (0, seq_pad - seq), (0, 0))).reshape(
            batch * seq_pad, -1)
        u_a_g, u_bd_g = padr(u_a), padr(u_bd)
    else:
        u_a_g, u_bd_g = u_a, u_bd
    state8 = jnp.pad(conv_state.astype(F32), ((0, 0), (SUBLANES - (CONV_W - 1), 0), (0, 0)))
    o_a, new_delta = gdn(u_a_g, u_bd_g, state8, delta_state.astype(F32), p['w_conv'], p['a_log'], p['dt_bias'],
                         p['gdn_norm'], batch, seq_pad, chunk, min(seq, chunk), hp)
    if seq_pad != seq:
        o_a = o_a.reshape(batch, seq_pad, GDN_WIDTH)[:, :seq].reshape(m, GDN_WIDTH)
    keep = min(seq, CONV_W - 1)
    tail = u_a.reshape(batch, seq, -1)[:, seq - keep:, 0:3 * GDN_WIDTH]
    new_conv = jnp.concatenate([conv_state.astype(F32)[:, keep:], tail], axis=1)
    if sample_ctx is None:
        d_q = matmul(h, w['dq'], out_dtype=BF16, rope=tables, scale=DIFF_HD ** -0.5 * LOG2E)
        o_b = diff_attn_prompt(d_q, d_k.astype(BF16), d_v.astype(BF16).T, lam, p['diff_norm'], batch, seq)
    else:
        cache_k, cache_v, page_table = sample_ctx
        d_q = matmul(h, w['dq'], rope=tables, scale=DIFF_HD ** -0.5)
        o_b = diff_attn_sample(d_q.reshape(batch, seq, -1), d_k.reshape(batch, seq, -1), d_v.reshape(batch, seq, -1),
                               lam, p['diff_norm'], cache_k, cache_v, page_table, seq)
        o_b = o_b[:, :seq].reshape(m, DIFF_WIDTH)
    mixed = merge_branches(o_a, o_b, w['w_branch_a'], w['w_branch_b'], u_g)
    x = matmul(mixed, w['w_out'], residual=x)
    hc = rmsnorm_rows(x, p['norm_cross'], out_dtype=od)
    mq = matmul(hc, w['w_mq'], out_dtype=od).reshape(batch, seq, MEM_WIDTH)
    if seq % SUBLANES:
        mq = jnp.pad(mq, ((0, 0), (0, SUBLANES - seq), (0, 0)))
    mo = mem_attn(mq, mem_k, mem_v)[:, :seq].reshape(m, MEM_WIDTH)
    x = matmul(mo, w['w_mo'], residual=x)
    return x, new_conv, new_delta, d_k, d_v


def kernel(x_prompt, x_sample, cache_k, cache_v, cache_mem_k, cache_mem_v, state_delta, state_conv, page_table, mem_prompt, norm_mix, w_in, w_conv, a_log, dt_bias, gdn_norm, lambda_q1, lambda_k1, lambda_q2, lambda_k2, diff_norm, w_branch_a, w_branch_b, w_out, norm_cross, norm_mem, w_mq, w_mk, w_mv, w_mo, norm_ffn, w_router_group, b_router_group, w_router_expert, b_router_expert, w_gate, w_up, w_down, norm_final):
    p = dict(norm_mix=norm_mix, w_in=w_in, w_conv=w_conv, a_log=a_log, dt_bias=dt_bias, gdn_norm=gdn_norm,
             diff_norm=diff_norm, w_branch_a=w_branch_a, w_branch_b=w_branch_b, w_out=w_out,
             norm_cross=norm_cross, w_mq=w_mq, w_mk=w_mk, w_mv=w_mv, w_mo=w_mo)
    bp, tp, d = x_prompt.shape
    bs, ts, _ = x_sample.shape
    past_len = page_table.shape[1] * PAGE_SIZE
    w = _prep_weights(p, BF16)
    w_hp = _prep_weights(p, F32)
    lam =(jnp.exp(jnp.sum(lambda_q1.astype(F32) * lambda_k1.astype(F32)))
           - jnp.exp(jnp.sum(lambda_q2.astype(F32) * lambda_k2.astype(F32))) + LAMBDA_INIT)
    mlen = mem_prompt.shape[1]
    hm = rmsnorm_rows(mem_prompt.reshape(bp * mlen, d), norm_mem)
    mem_k_p = matmul(hm, w['w_mk'])
    mem_v_p = matmul(hm, w['w_mv'])
    conv0 = jnp.zeros((bp, CONV_W - 1, 3 * GDN_WIDTH), F32)
    delta0 = jnp.zeros((bp, GDN_HEADS, GDN_HD, GDN_HD), F32)
    xp, conv_p, delta_p, k_p, v_p = _mixers(
        x_prompt.reshape(bp * tp, d), bp, tp, jnp.arange(tp), conv0, delta0,
        mem_k_p.astype(BF16).reshape(bp, mlen, MEM_WIDTH), mem_v_p.astype(BF16).reshape(bp, mlen, MEM_WIDTH),
        p, w, lam, None)
    xs, conv_s, delta_s, k_s, v_s = _mixers(
        x_sample.reshape(bs * ts, d), bs, ts, past_len + jnp.arange(ts), state_conv, state_delta,
        cache_mem_k.astype(F32).reshape(bs, -1, MEM_WIDTH), cache_mem_v.astype(F32).reshape(bs, -1, MEM_WIDTH),
        p, w_hp, lam, (cache_k, cache_v, page_table))
    yp, ys = moe_and_final([xp, xs], norm_ffn, w_router_group, b_router_group, w_router_expert, b_router_expert,
                           w_gate, w_up, w_down, norm_final)
    return (yp.reshape(bp, tp, d), ys.reshape(bs, ts, d),
            k_p.reshape(bp, tp, DIFF_HEADS, DIFF_VD), v_p.reshape(bp, tp, DIFF_HEADS, DIFF_VD),
            mem_k_p.reshape(bp, mlen, MEM_HEADS, MEM_HD), mem_v_p.reshape(bp, mlen, MEM_HEADS, MEM_HD),
            delta_p.astype(state_delta.dtype), conv_p.astype(x_prompt.dtype),
            k_s.reshape(bs, ts, DIFF_HEADS, DIFF_VD), v_s.reshape(bs, ts, DIFF_HEADS, DIFF_VD),
            delta_s.astype(state_delta.dtype), conv_s.astype(state_conv.dtype))
```

```python
import functools
import math

import jax
import jax.numpy as jnp
from jax import lax
from jax.experimental import pallas as pl
from jax.experimental.pallas import tpu as pltpu

F32 = jnp.float32
BF16 = jnp.bfloat16

GDN_HEADS = 8
GDN_HD = 128
GDN_WIDTH = GDN_HEADS * GDN_HD
CONV_W = 4
GDN_CHUNK = 64
DIFF_HEADS = 8
DIFF_HD = 64
DIFF_VD = 2 * DIFF_HD
DIFF_WIDTH = DIFF_HEADS * DIFF_VD
ROT_DIM = DIFF_HD // 4
ROPE_THETA = 500000.0
LAMBDA_INIT = 0.2
PAGE_SIZE = 128
MEM_HEADS = 4
MEM_HD = 128
MEM_WIDTH = MEM_HEADS * MEM_HD
N_GROUPS = 4
EXPERTS_PER_GROUP = 8
N_EXPERTS = N_GROUPS * EXPERTS_PER_GROUP
TOP_K_INNER = 2
RMS_EPS = 1e-6
L2_EPS = 1e-6
NEG_INIT = -1e30

LANES = 128
SUBLANES = 8
VMEM_LIMIT = 48 * 1024 * 1024
FLASH_TILE = 1024
ONES_ROWS = 16
PAGES_PER_STEP = 8
LOG2E = math.log2(math.e)

def _cparams(*sem):
    return pltpu.CompilerParams(dimension_semantics=sem, vmem_limit_bytes=VMEM_LIMIT)


def _tile(n, pref):
    if n <= pref:
        return n
    t = pref
    while n % t:
        t //= 2
    return t


_NN = (((1,), (0,)), ((), ()))
_NT = (((1,), (1,)), ((), ()))
_TN = (((0,), (0,)), ((), ()))


def _dotg(a, b, dims=_NN, hp=False):
    d = lambda x, y: lax.dot_general(x, y, dims, preferred_element_type=F32)
    if not hp:
        return d(a.astype(BF16), b.astype(BF16))
    (ah, am), (bh, bm) = _split2(a), _split2(b)
    return d(ah, bh) + (d(ah, bm) + d(am, bh))


def _mm(a, b):
    return _dotg(a, b, _NN, hp=(a.dtype == F32 and b.dtype == F32))


def _split3(x):
    hi = x.astype(BF16)
    r = x - hi.astype(F32)
    mid = r.astype(BF16)
    lo = (r - mid.astype(F32)).astype(BF16)
    return hi, mid, lo


def _split2(x):
    hi = x.astype(BF16)
    return hi, (x - hi.astype(F32)).astype(BF16)


def _dot3s(a2, b2):
    (ah, am), (bh, bm) = a2, b2
    d = lambda x, y: jnp.dot(x, y, preferred_element_type=F32)
    return d(ah, bh) + (d(ah, bm) + d(am, bh))


def _dot3(a, b):
    return _dot3s(_split2(a), _split2(b))


def _rmsnorm_body(x_ref, g_ref, o_ref):
    x = x_ref[...]
    ms = jnp.mean(x * x, axis=-1, keepdims=True)
    o_ref[...] = (x * lax.rsqrt(ms + RMS_EPS) * g_ref[...]).astype(o_ref.dtype)


def rmsnorm_rows(x, g, out_dtype=BF16):
    m, d = x.shape
    tm = _tile(m, 512)
    return pl.pallas_call(
        _rmsnorm_body,
        grid=(m // tm,),
        in_specs=[pl.BlockSpec((tm, d), lambda i: (i, 0)), pl.BlockSpec((1, d), lambda i: (0, 0))],
        out_specs=pl.BlockSpec((tm, d), lambda i: (i, 0)),
        out_shape=jax.ShapeDtypeStruct((m, d), out_dtype),
        compiler_params=_cparams("parallel"),
        name="rmsnorm_rows",
    )(x, g.reshape(1, d).astype(F32))


def _mm_body(a_ref, w_ref, o_ref):
    o_ref[...] = _mm(a_ref[...], w_ref[...]).astype(o_ref.dtype)


def _mm_res_body(a_ref, w_ref, r_ref, o_ref):
    o_ref[...] = r_ref[...] + _mm(a_ref[...], w_ref[...])


def _mm_rope_body(a_ref, w_ref, c_ref, s1_ref, s2_ref, o_ref, *, scale, reps):
    o = _mm(a_ref[...], w_ref[...])
    tn = o.shape[1]
    c = jnp.tile(c_ref[...], (1, reps))
    s1 = jnp.tile(s1_ref[...], (1, reps))
    s2 = jnp.tile(s2_ref[...], (1, reps))
    r = o * c + pltpu.roll(o, tn - ROT_DIM // 2, 1) * s1 + pltpu.roll(o, ROT_DIM // 2, 1) * s2
    if scale != 1.0:
        r = r * scale
    o_ref[...] = r.astype(o_ref.dtype)


def matmul(a, w, out_dtype=F32, residual=None, rope=None, scale=1.0, tm_pref=1024, tn_pref=512):
    m, k = a.shape
    n = w.shape[1]
    tm = _tile(m, tm_pref)
    tn = _tile(n, tn_pref)
    in_specs = [pl.BlockSpec((tm, k), lambda i, j: (i, 0)), pl.BlockSpec((k, tn), lambda i, j: (0, j))]
    args = [a, w]
    if residual is not None:
        body = _mm_res_body
        in_specs.append(pl.BlockSpec((tm, tn), lambda i, j: (i, j)))
        args.append(residual)
    elif rope is not None:
        c, s1, s2 = rope
        nt = c.shape[0] // tm
        body = functools.partial(_mm_rope_body, scale=scale, reps=tn // LANES)
        tspec = pl.BlockSpec((tm, LANES), lambda i, j: (i % nt, 0))
        in_specs += [tspec, tspec, tspec]
        args += [c, s1, s2]
    else:
        body = _mm_body
    return pl.pallas_call(
        body,
        grid=(m // tm, n // tn),
        in_specs=in_specs,
        out_specs=pl.BlockSpec((tm, tn), lambda i, j: (i, j)),
        out_shape=jax.ShapeDtypeStruct((m, n), out_dtype),
        compiler_params=_cparams("parallel", "arbitrary"),
        name="matmul",
    )(*args)


def _merge_body(oa_ref, ob_ref, wa_ref, wb_ref, ga_ref, gb_ref, o_ref):
    br_a = _mm(oa_ref[...], wa_ref[...])
    br_b = _mm(ob_ref[...], wb_ref[...])
    o_ref[...] = (jax.nn.sigmoid(ga_ref[...]) * br_a + jax.nn.sigmoid(gb_ref[...]) * br_b).astype(o_ref.dtype)


def merge_branches(oa, ob, wa, wb, gates):
    m, ka = oa.shape
    n = wa.shape[1]
    tm = _tile(m, 1024)
    tn = _tile(n, 512)
    nb = n // tn
    return pl.pallas_call(
        _merge_body,
        grid=(m // tm, nb),
        in_specs=[
            pl.BlockSpec((tm, ka), lambda i, j: (i, 0)),
            pl.BlockSpec((tm, ob.shape[1]), lambda i, j: (i, 0)),
            pl.BlockSpec((ka, tn), lambda i, j: (0, j)),
            pl.BlockSpec((wb.shape[0], tn), lambda i, j: (0, j)),
            pl.BlockSpec((tm, tn), lambda i, j: (i, j)),
            pl.BlockSpec((tm, tn), lambda i, j: (i, j + nb)),
        ],
        out_specs=pl.BlockSpec((tm, tn), lambda i, j: (i, j)),
        out_shape=jax.ShapeDtypeStruct((m, n), oa.dtype),
        compiler_params=_cparams("parallel", "arbitrary"),
        name="merge_branches",
    )(oa, ob, wa, wb, gates, gates)


def _col(arr, idx):
    lane = lax.broadcasted_iota(jnp.int32, arr.shape, 1)
    return jnp.sum(jnp.where(lane == idx, arr, 0.0), axis=1, keepdims=True)


def _row(arr, idx):
    sub = lax.broadcasted_iota(jnp.int32, arr.shape, 0)
    return jnp.sum(jnp.where(sub == idx, arr, 0.0), axis=0, keepdims=True)


def _gdn_body(cur_ref, prev_ref, st8_ref, bd_ref, s0_ref, wconv_ref, alog_ref, dtb_ref, gnorm_ref,
              o_ref, s_ref, ext_ref, *, chunk, n_valid, hp):
    c = pl.program_id(1)
    C = chunk
    H = GDN_HEADS
    W3 = 3 * GDN_WIDTH

    @pl.when(c == 0)
    def _():
        s_ref[...] = s0_ref[...]
        ext_ref[0:SUBLANES, :] = st8_ref[0]

    @pl.when(c != 0)
    def _():
        ext_ref[0:SUBLANES, :] = prev_ref[:, 0:W3]

    ext_ref[SUBLANES:SUBLANES + C, :] = cur_ref[:, 0:W3]
    base = SUBLANES - (CONV_W - 1)
    acc = ext_ref[base:base + C, :] * wconv_ref[0:1, :]
    for i in range(1, CONV_W):
        acc = acc + ext_ref[base + i:base + i + C, :] * wconv_ref[i:i + 1, :]
    qkv = acc * jax.nn.sigmoid(acc)

    bd = bd_ref[...]
    lane = lax.broadcasted_iota(jnp.int32, bd.shape, 1)
    beta_all = jax.nn.sigmoid(bd)
    xg = bd + dtb_ref[...]
    softplus = jnp.maximum(xg, 0.0) + jnp.log1p(jnp.exp(-jnp.abs(xg)))
    g_all = -jnp.exp(alog_ref[...]) * softplus
    gb = jnp.where(lane < H, beta_all, jnp.where(lane < 2 * H, g_all, 0.0))
    if n_valid < C:
        gb = jnp.where(lax.broadcasted_iota(jnp.int32, bd.shape, 0) < n_valid, gb, 0.0)
    ri = lax.broadcasted_iota(jnp.int32, (C, C), 0)
    ci = lax.broadcasted_iota(jnp.int32, (C, C), 1)
    causal = ri >= ci
    strict = ri > ci
    ltri = jnp.where(causal, 1.0, 0.0).astype(BF16)
    gc_cols = sum(jnp.dot(ltri, piece, preferred_element_type=F32) for piece in _split3(gb))
    gc_rows = gc_cols.T
    eye = jnp.where(ri == ci, 1.0, 0.0).astype(F32)

    heads = []
    for h in range(H):
        q = qkv[:, h * GDN_HD:(h + 1) * GDN_HD]
        k = qkv[:, GDN_WIDTH + h * GDN_HD:GDN_WIDTH + (h + 1) * GDN_HD]
        v = qkv[:, 2 * GDN_WIDTH + h * GDN_HD:2 * GDN_WIDTH + (h + 1) * GDN_HD]
        q = q * lax.rsqrt(jnp.sum(q * q, axis=-1, keepdims=True) + L2_EPS) * (GDN_HD ** -0.5)
        k = k * lax.rsqrt(jnp.sum(k * k, axis=-1, keepdims=True) + L2_EPS)
        beta_c = _col(gb, h)
        gc_c = _col(gc_cols, H + h)
        gc_r = _row(gc_rows, H + h)
        g_last = gc_c[C - 1:C, :]
        decay = jnp.where(causal, jnp.exp(jnp.where(causal, gc_c - gc_r, 0.0)), 0.0)
        kb = k * beta_c
        kk = _dotg(jnp.concatenate([kb, q], axis=0), k, _NT, hp)
        a_kk = jnp.where(strict, kk[0:C] * decay, 0.0)
        a_qk = jnp.where(causal, kk[C:2 * C] * decay, 0.0)
        rhs = jnp.concatenate([v * beta_c, kb * jnp.exp(gc_c)], axis=1)
        heads.append(dict(q=q, k=k, gc_c=gc_c, g_last=g_last, a_qk=a_qk, rhs=rhs, nmat=-a_kk, tinv=eye - a_kk))
    for hd in heads:
        hd['ns'] = _split2(hd['nmat'])
    span = 2
    while span < C:
        for hd in heads:
            hd['ns'] = _split2(_dot3s(hd['ns'], hd['ns']))
        for hd in heads:
            hd['tinv'] = hd['tinv'] + _dot3s(_split2(hd['tinv']), hd['ns'])
        span *= 2
    for hd in heads:
        hd['sol'] = _dot3s(_split2(hd['tinv']), _split2(hd['rhs']))
    for h, hd in enumerate(heads):
        sl = slice(h * GDN_HD, (h + 1) * GDN_HD)
        q, k, gc_c, g_last = hd['q'], hd['k'], hd['gc_c'], hd['g_last']
        u = hd['sol'][:, 0:GDN_HD]
        w = hd['sol'][:, GDN_HD:2 * GDN_HD]
        s = s_ref[0, h]
        ws = _dotg(jnp.concatenate([w, q * jnp.exp(gc_c)], axis=0), s, _NN, hp)
        v_new = u - ws[0:C]
        o = ws[C:2 * C] + _dotg(hd['a_qk'], v_new, _NN, hp)
        s_ref[0, h] = s * jnp.exp(g_last) + _dotg(k * jnp.exp(g_last - gc_c), v_new, _TN, hp)
        z = cur_ref[:, W3 + h * GDN_HD:W3 + (h + 1) * GDN_HD]
        o = o * lax.rsqrt(jnp.mean(o * o, axis=-1, keepdims=True) + RMS_EPS) * gnorm_ref[...]
        o = o * (z * jax.nn.sigmoid(z))
        o_ref[:, sl] = o.astype(o_ref.dtype)


def gdn(u_a, u_bd, state8, s0, w_conv, a_log, dt_bias, gdn_norm, batch, seq, chunk, n_valid, hp):
    nc = seq // chunk
    rb = chunk // SUBLANES
    alog = jnp.zeros((1, LANES), F32).at[0, GDN_HEADS:2 * GDN_HEADS].set(a_log.astype(F32))
    dtb = jnp.zeros((1, LANES), F32).at[0, GDN_HEADS:2 * GDN_HEADS].set(dt_bias.astype(F32))
    W3 = 3 * GDN_WIDTH
    return pl.pallas_call(
        functools.partial(_gdn_body, chunk=chunk, n_valid=n_valid, hp=hp),
        grid=(batch, nc),
        in_specs=[
            pl.BlockSpec((chunk, 4 * GDN_WIDTH), lambda b, c: (b * nc + c, 0)),
            pl.BlockSpec((SUBLANES, 4 * GDN_WIDTH), lambda b, c: (jnp.maximum((b * nc + c) * rb - 1, 0), 0)),
            pl.BlockSpec((1, SUBLANES, W3), lambda b, c: (b, 0, 0)),
            pl.BlockSpec((chunk, LANES), lambda b, c: (b * nc + c, 0)),
            pl.BlockSpec((1, GDN_HEADS, GDN_HD, GDN_HD), lambda b, c: (b, 0, 0, 0)),
            pl.BlockSpec((CONV_W, W3), lambda b, c: (0, 0)),
            pl.BlockSpec((1, LANES), lambda b, c: (0, 0)),
            pl.BlockSpec((1, LANES), lambda b, c: (0, 0)),
            pl.BlockSpec((1, GDN_HD), lambda b, c: (0, 0)),
        ],
        out_specs=[
            pl.BlockSpec((chunk, GDN_WIDTH), lambda b, c: (b * nc + c, 0)),
            pl.BlockSpec((1, GDN_HEADS, GDN_HD, GDN_HD), lambda b, c: (b, 0, 0, 0)),
        ],
        out_shape=[
            jax.ShapeDtypeStruct((batch * seq, GDN_WIDTH), F32 if hp else BF16),
            jax.ShapeDtypeStruct((batch, GDN_HEADS, GDN_HD, GDN_HD), F32),
        ],
        scratch_shapes=[pltpu.VMEM((SUBLANES + chunk, W3), F32)],
        compiler_params=_cparams("parallel", "arbitrary"),
        name="gdn",
    )(u_a, u_a, state8, u_bd, s0, w_conv.astype(F32), alog, dtb, gdn_norm.reshape(1, GDN_HD).astype(F32))


def _flash_body(qi_ref, ki_ref, lam_ref, q_ref, k_ref, vt_ref, dn_ref, o_ref, qs_ref, m_ref, acc_ref, *, tq):
    step = pl.program_id(2)
    qi = qi_ref[step]
    ki = ki_ref[step]

    @pl.when(ki == 0)
    def _():
        q = q_ref[...]
        lane = lax.broadcasted_iota(jnp.int32, q.shape, 1)
        zero = jnp.zeros_like(q)
        qs_ref[0:tq, :] = jnp.where(lane < DIFF_HD, q, zero)
        qs_ref[tq:2 * tq, :] = jnp.where(lane >= DIFF_HD, q, zero)
        m_ref[...] = jnp.full(m_ref.shape, NEG_INIT, F32)
        acc_ref[...] = jnp.zeros(acc_ref.shape, F32)

    def update(masked):
        vt1 = jnp.concatenate([vt_ref[...], jnp.ones((ONES_ROWS, vt_ref.shape[1]), BF16)], axis=0)
        s = lax.dot_general(k_ref[...], qs_ref[...], _NT, preferred_element_type=F32)
        if masked:
            key = lax.broadcasted_iota(jnp.int32, s.shape, 0)
            qry = lax.rem(lax.broadcasted_iota(jnp.int32, s.shape, 1), tq)
            s = jnp.where(key <= qry, s, NEG_INIT)
        m_prev = m_ref[...]
        m_new = jnp.maximum(m_prev, jnp.max(s, axis=0, keepdims=True))
        alpha = jnp.exp2(m_prev - m_new)
        p = jnp.exp2(s - m_new).astype(BF16)
        acc_ref[...] = alpha * acc_ref[...] + jnp.dot(vt1, p, preferred_element_type=F32)
        m_ref[...] = m_new

    @pl.when(ki < qi)
    def _():
        update(False)

    @pl.when(ki == qi)
    def _():
        update(True)
        lam = lam_ref[0]
        inv = 1.0 / acc_ref[DIFF_VD:DIFF_VD + 1, :]
        o = (acc_ref[0:DIFF_VD, 0:tq] * inv[:, 0:tq]
             - lam * (acc_ref[0:DIFF_VD, tq:2 * tq] * inv[:, tq:2 * tq]))
        o = o * lax.rsqrt(jnp.mean(o * o, axis=0, keepdims=True) + RMS_EPS)
        o_ref[...] = (o.T * dn_ref[...] * (1.0 - LAMBDA_INIT)).astype(o_ref.dtype)


def diff_attn_prompt(q, k, vt, lam, diff_norm, batch, seq):
    tq = _tile(seq, FLASH_TILE)
    nq = seq // tq
    pairs =[(i, j) for i in range(nq) for j in range(i + 1)]
    qi_tab = jnp.asarray([p[0] for p in pairs], jnp.int32)
    ki_tab = jnp.asarray([p[1] for p in pairs], jnp.int32)
    grid_spec = pltpu.PrefetchScalarGridSpec(
        num_scalar_prefetch=3,
        grid=(batch, DIFF_HEADS, len(pairs)),
        in_specs=[
            pl.BlockSpec((tq, DIFF_VD), lambda b, h, s, qt, kt, lam: (b * nq + qt[s], h)),
            pl.BlockSpec((tq, DIFF_VD), lambda b, h, s, qt, kt, lam: (b * nq + kt[s], h)),
            pl.BlockSpec((DIFF_VD, tq), lambda b, h, s, qt, kt, lam: (h, b * nq + kt[s])),
            pl.BlockSpec((1, DIFF_VD), lambda b, h, s, qt, kt, lam: (0, 0)),
        ],
        out_specs=pl.BlockSpec((tq, DIFF_VD), lambda b, h, s, qt, kt, lam: (b * nq + qt[s], h)),
        scratch_shapes=[
            pltpu.VMEM((2 * tq, DIFF_VD), BF16),
            pltpu.VMEM((1, 2 * tq), F32),
            pltpu.VMEM((DIFF_VD + ONES_ROWS, 2 * tq), F32),
        ],
    )
    return pl.pallas_call(
        functools.partial(_flash_body, tq=tq),
        grid_spec=grid_spec,
        out_shape=jax.ShapeDtypeStruct((batch * seq, DIFF_WIDTH), BF16),
        compiler_params=_cparams("parallel", "parallel", "arbitrary"),
        name="diff_attn_prompt",
    )(qi_tab, ki_tab, lam.reshape(1), q, k, vt, diff_norm.reshape(1, DIFF_VD).astype(F32))


PAGE_ROWS = PAGE_SIZE * DIFF_HEADS


def _head_mask(shape, tp, extra_mask=None):
    r = lax.broadcasted_iota(jnp.int32, shape, 0)
    c = lax.broadcasted_iota(jnp.int32, shape, 1)
    ok = jnp.bitwise_and(c, DIFF_HEADS - 1) == jnp.right_shift(r, (2 * tp).bit_length() - 1)
    if extra_mask is not None:
        ok = jnp.logical_and(ok, extra_mask(r, c))
    return jnp.where(ok, 0.0, NEG_INIT).astype(F32)


def _paged_body(pt_ref, lam_ref, q_ref, *rest, n_tok, pp, tp):
    kp_refs = rest[0:pp]
    vp_refs = rest[pp:2 * pp]
    kn_ref, vn_ref, dn_ref, o_ref, m_ref, l_ref, acc_ref, bias_ref = rest[2 * pp:]
    p = pl.program_id(1)
    npg = pl.num_programs(1)
    kshift = DIFF_HEADS.bit_length() - 1

    @pl.when(p == 0)
    def _():
        m_ref[...] = jnp.full(m_ref.shape, NEG_INIT, F32)
        l_ref[...] = jnp.zeros(l_ref.shape, F32)
        acc_ref[...] = jnp.zeros(acc_ref.shape, F32)
        bias_ref[...] = _head_mask(bias_ref.shape, tp)

    qh, qm = _split2(q_ref[0])
    qcat = jnp.concatenate([qh, qh, qm], axis=1)

    def scores(kb, bias):
        kh, km = _split2(kb)
        kcat = jnp.concatenate([kh, km, kh], axis=1)
        return lax.dot_general(qcat, kcat, _NT, preferred_element_type=F32) + bias

    def weighted_values(pr, vb):
        ph, pm = _split2(pr)
        vh, vm = _split2(vb)
        wide = jnp.dot(ph, jnp.concatenate([vh, vm], axis=1), preferred_element_type=F32)
        return (wide[:, 0:DIFF_VD] + wide[:, DIFF_VD:2 * DIFF_VD]) + jnp.dot(pm, vh, preferred_element_type=F32)

    def update(ss, vbs):
        m_prev = m_ref[...]
        m_new = m_prev
        for s in ss:
            m_new = jnp.maximum(m_new, jnp.max(s, axis=1, keepdims=True))
        alpha = jnp.exp(m_prev - m_new)
        l_new = alpha * l_ref[...]
        acc = alpha * acc_ref[...]
        for s, vb in zip(ss, vbs):
            pr = jnp.exp(s - m_new)
            l_new = l_new + jnp.sum(pr, axis=1, keepdims=True)
            acc = acc + weighted_values(pr, vb)
        l_ref[...] = l_new
        acc_ref[...] = acc
        m_ref[...] = m_new

    bias = bias_ref[...]
    update([scores(kp_refs[j][0], bias) for j in range(pp)], [vp_refs[j][0] for j in range(pp)])

    @pl.when(p == npg - 1)
    def _():
        def causal(r, c):
            tok = jnp.right_shift(c, kshift)
            return jnp.logical_and(tok <= jnp.bitwise_and(r, tp - 1), tok < n_tok)

        update([scores(kn_ref[0], _head_mask((q_ref.shape[1], kn_ref.shape[1]), tp, causal))], [vn_ref[0]])
        lam = lam_ref[0]
        inv = 1.0 / l_ref[...]
        for h in range(DIFF_HEADS):
            r0 = h * 2 * tp
            if tp == SUBLANES:
                o1 = acc_ref[r0:r0 + tp, :] * inv[r0:r0 + tp]
                o2 = acc_ref[r0 + tp:r0 + 2 * tp, :] * inv[r0 + tp:r0 + 2 * tp]
            else:
                o1 = acc_ref[r0:r0 + SUBLANES, :] * inv[r0:r0 + SUBLANES]
                o2 = pltpu.roll(o1, tp, 0)
            o = o1 - lam * o2
            o = o * lax.rsqrt(jnp.mean(o * o, axis=-1, keepdims=True) + RMS_EPS) * dn_ref[...]
            o_ref[0, h * SUBLANES:(h + 1) * SUBLANES, :] = (o * (1.0 - LAMBDA_INIT)).astype(o_ref.dtype)


def diff_attn_sample(q, k_new, v_new, lam, diff_norm, cache_k, cache_v, page_table, n_tok):
    bd = q.shape[0]
    n_pool = cache_k.shape[0]
    n_pages = page_table.shape[1]
    assert n_tok <= SUBLANES
    tp = SUBLANES // 2 if n_tok <= SUBLANES // 2 else SUBLANES
    qrows_n = 2 * DIFF_HEADS * tp
    pp = PAGES_PER_STEP
    while n_pages % pp:
        pp //= 2
    qh = q.reshape(bd, n_tok, DIFF_HEADS, 2, DIFF_HD)
    qh = jnp.pad(qh, ((0, 0), (0, tp - n_tok), (0, 0), (0, 0), (0, 0)))
    qh = jnp.transpose(qh, (0, 2, 3, 1, 4))
    zeros = jnp.zeros_like(qh[:, :, 0])
    qrows = jnp.stack([jnp.concatenate([qh[:, :, 0], zeros], axis=-1),
                       jnp.concatenate([zeros, qh[:, :, 1]], axis=-1)], axis=2).reshape(bd, qrows_n, DIFF_VD)
    new_rows = tp * DIFF_HEADS
    out_rows = SUBLANES * DIFF_HEADS
    padn = ((0, 0), (0, new_rows - n_tok * DIFF_HEADS), (0, 0))
    kn = jnp.pad(k_new.reshape(bd, n_tok * DIFF_HEADS, DIFF_VD), padn)
    vn = jnp.pad(v_new.reshape(bd, n_tok * DIFF_HEADS, DIFF_VD), padn)
    ck = cache_k.reshape(n_pool, PAGE_ROWS, DIFF_VD)
    cv = cache_v.reshape(n_pool, PAGE_ROWS, DIFF_VD)

    def page_spec(j):
        return pl.BlockSpec((1, PAGE_ROWS, DIFF_VD), lambda b, p, pt, lam: (pt[b, p * pp + j], 0, 0))

    grid_spec = pltpu.PrefetchScalarGridSpec(
        num_scalar_prefetch=2,
        grid=(bd, n_pages // pp),
        in_specs=(
            [pl.BlockSpec((1, qrows_n, DIFF_VD), lambda b, p, pt, lam: (b, 0, 0))]
            + [page_spec(j) for j in range(pp)]
            + [page_spec(j) for j in range(pp)]
            + [pl.BlockSpec((1, new_rows, DIFF_VD), lambda b, p, pt, lam: (b, 0, 0)),
               pl.BlockSpec((1, new_rows, DIFF_VD), lambda b, p, pt, lam: (b, 0, 0)),
               pl.BlockSpec((1, DIFF_VD), lambda b, p, pt, lam: (0, 0))]
        ),
        out_specs=pl.BlockSpec((1, out_rows, DIFF_VD), lambda b, p, pt, lam: (b, 0, 0)),
        scratch_shapes=[
            pltpu.VMEM((qrows_n, 1), F32),
            pltpu.VMEM((qrows_n, 1), F32),
            pltpu.VMEM((qrows_n, DIFF_VD), F32),
            pltpu.VMEM((qrows_n, PAGE_ROWS), F32),
        ],
    )
    out = pl.pallas_call(
        functools.partial(_paged_body, n_tok=n_tok, pp=pp, tp=tp),
        grid_spec=grid_spec,
        out_shape=jax.ShapeDtypeStruct((bd, out_rows, DIFF_VD), F32),
        compiler_params=_cparams("parallel", "arbitrary"),
        name="diff_attn_sample",
    )(page_table, lam.reshape(1), qrows, *([ck] * pp), *([cv] * pp), kn, vn,
      diff_norm.reshape(1, DIFF_VD).astype(F32))
    out = out.reshape(bd, DIFF_HEADS, SUBLANES, DIFF_VD)[:, :, :n_tok]
    return jnp.transpose(out, (0, 2, 1, 3)).reshape(bd, n_tok, DIFF_WIDTH)


def _mem_body(q_ref, k_ref, v_ref, o_ref):
    hp = q_ref.dtype == F32
    for h in range(MEM_HEADS):
        sl = slice(h * MEM_HD, (h + 1) * MEM_HD)
        s = _dotg(q_ref[0, :, sl], k_ref[0, :, sl], _NT, hp) * (MEM_HD ** -0.5)
        s = s - jnp.max(s, axis=1, keepdims=True)
        e = jnp.exp(s)
        p = e / jnp.sum(e, axis=1, keepdims=True)
        o_ref[0, :, sl] = _dotg(p, v_ref[0, :, sl], _NN, hp).astype(o_ref.dtype)


def mem_attn(q, mem_k, mem_v):
    b, t, _ = q.shape
    mlen = mem_k.shape[1]
    tq = _tile(t, 512)
    return pl.pallas_call(
        _mem_body,
        grid=(b, t // tq),
        in_specs=[
            pl.BlockSpec((1, tq, MEM_WIDTH), lambda i, j: (i, j, 0)),
            pl.BlockSpec((1, mlen, MEM_WIDTH), lambda i, j: (i, 0, 0)),
            pl.BlockSpec((1, mlen, MEM_WIDTH), lambda i, j: (i, 0, 0)),
        ],
        out_specs=pl.BlockSpec((1, tq, MEM_WIDTH), lambda i, j: (i, j, 0)),
        out_shape=jax.ShapeDtypeStruct((b, t, MEM_WIDTH), q.dtype),
        compiler_params=_cparams("parallel", "arbitrary"),
        name="mem_attn",
    )(q, mem_k, mem_v)


def _norm_router_body(x_ref, g_ref, wr_ref, br_ref, h_ref, rt_ref):
    x = x_ref[...]
    hn = x * lax.rsqrt(jnp.mean(x * x, axis=-1, keepdims=True) + RMS_EPS) * g_ref[...]
    h_ref[...] = hn.astype(h_ref.dtype)
    lg = _dot3(hn, wr_ref[...]) + br_ref[...]
    lane = lax.broadcasted_iota(jnp.int32, lg.shape, 1)
    lanef = lane.astype(F32)
    big = float(LANES)

    def first_argmax(vals, mask):
        top = jnp.max(jnp.where(mask, vals, NEG_INIT), axis=1, keepdims=True)
        idx = jnp.min(jnp.where(jnp.logical_and(mask, vals == top), lanef, big), axis=1, keepdims=True)
        return top, idx

    gmask = lane < N_GROUPS
    gtop, gidx = first_argmax(lg, gmask)
    g_w = 1.0 / jnp.sum(jnp.where(gmask, jnp.exp(lg - gtop), 0.0), axis=1, keepdims=True)
    first = N_GROUPS + EXPERTS_PER_GROUP * gidx.astype(jnp.int32)
    emask = jnp.logical_and(lane >= first, lane < first + EXPERTS_PER_GROUP)
    e1, i1 = first_argmax(lg, emask)
    e2, i2 = first_argmax(lg, jnp.logical_and(emask, lanef != i1))
    ez = jnp.sum(jnp.where(emask, jnp.exp(lg - e1), 0.0), axis=1, keepdims=True)
    p1 = 1.0 / ez
    p2 = jnp.exp(e2 - e1) / ez
    psum = p1 + p2
    rt_ref[...] = jnp.where(lane == 0, i1 - N_GROUPS,
                            jnp.where(lane == 1, i2 - N_GROUPS,
                                      jnp.where(lane == 2, g_w * (p1 / psum),
                                                jnp.where(lane == 3, g_w * (p2 / psum), 0.0))))


def norm_router(x, g, w_router, b_router):
    m, d = x.shape
    tm = _tile(m, 512)
    return pl.pallas_call(
        _norm_router_body,
        grid=(m // tm,),
        in_specs=[
            pl.BlockSpec((tm, d), lambda i: (i, 0)),
            pl.BlockSpec((1, d), lambda i: (0, 0)),
            pl.BlockSpec((d, LANES), lambda i: (0, 0)),
            pl.BlockSpec((1, LANES), lambda i: (0, 0)),
        ],
        out_specs=[pl.BlockSpec((tm, d), lambda i: (i, 0)), pl.BlockSpec((tm, LANES), lambda i: (i, 0))],
        out_shape=[jax.ShapeDtypeStruct((m, d), BF16), jax.ShapeDtypeStruct((m, LANES), F32)],
        compiler_params=_cparams("parallel"),
        name="norm_router",
    )(x, g.reshape(1, d).astype(F32), w_router, b_router)


def _expert_body(te_ref, tv_ref, x_ref, wg_ref, wu_ref, wd_ref, o_ref):
    i = pl.program_id(0)

    @pl.when(tv_ref[i] != 0)
    def _():
        x = x_ref[...]
        a = jnp.dot(x, wg_ref[0].astype(BF16), preferred_element_type=F32)
        b = jnp.dot(x, wu_ref[0].astype(BF16), preferred_element_type=F32)
        hid = (a * jax.nn.sigmoid(a)) * b
        o_ref[...] = jnp.dot(hid.astype(BF16), wd_ref[0].astype(BF16), preferred_element_type=F32)

    @pl.when(tv_ref[i] == 0)
    def _():
        o_ref[...] = jnp.zeros(o_ref.shape, o_ref.dtype)


def grouped_experts(xs, tile_expert, tile_valid, w_gate, w_up, w_down, tm):
    rows, d = xs.shape
    nt = rows // tm
    ff = w_gate.shape[2]
    grid_spec = pltpu.PrefetchScalarGridSpec(
        num_scalar_prefetch=2,
        grid=(nt,),
        in_specs=[
            pl.BlockSpec((tm, d), lambda i, te, tv: (i, 0)),
            pl.BlockSpec((1, d, ff), lambda i, te, tv: (te[i], 0, 0)),
            pl.BlockSpec((1, d, ff), lambda i, te, tv: (te[i], 0, 0)),
            pl.BlockSpec((1, ff, d), lambda i, te, tv: (te[i], 0, 0)),
        ],
        out_specs=pl.BlockSpec((tm, d), lambda i, te, tv: (i, 0)),
    )
    return pl.pallas_call(
        _expert_body,
        grid_spec=grid_spec,
        out_shape=jax.ShapeDtypeStruct((rows, d), F32),
        compiler_params=_cparams("arbitrary"),
        name="grouped_experts",
    )(tile_expert, tile_valid, xs, w_gate, w_up, w_down)


def _final_body(x_ref, ya_ref, yb_ref, rt_ref, g_ref, o_ref):
    rt = rt_ref[...]
    x = x_ref[...] + (_col(rt, 2) * ya_ref[...] + _col(rt, 3) * yb_ref[...])
    o_ref[...] = x * lax.rsqrt(jnp.mean(x * x, axis=-1, keepdims=True) + RMS_EPS) * g_ref[...]


def final_norm(x, ya, yb, row0, route, g):
    m, d = x.shape
    tm = _tile(m, 512)
    assert row0 % tm == 0
    blk0 = row0 // tm
    spec = pl.BlockSpec((tm, d), lambda i: (i, 0))
    yspec = pl.BlockSpec((tm, d), lambda i: (i + blk0, 0))
    return pl.pallas_call(
        _final_body,
        grid=(m // tm,),
        in_specs=[spec, yspec, yspec, pl.BlockSpec((tm, LANES), lambda i: (i, 0)),
                  pl.BlockSpec((1, d), lambda i: (0, 0))],
        out_specs=spec,
        out_shape=jax.ShapeDtypeStruct((m, d), F32),
        compiler_params=_cparams("parallel"),
        name="final_norm",
    )(x, ya, yb, route, g.reshape(1, d).astype(F32))


MOE_TILE = 256


def moe_and_final(x_list, norm_ffn, w_rg, b_rg, w_re, b_re, w_gate, w_up, w_down, norm_final):
    d = x_list[0].shape[1]
    w_router = jnp.zeros((d, LANES), F32).at[:, 0:N_GROUPS].set(w_rg.astype(F32))
    w_router = w_router.at[:, N_GROUPS:N_GROUPS + N_EXPERTS].set(w_re.astype(F32))
    b_router = jnp.zeros((1, LANES), F32).at[0, 0:N_GROUPS].set(b_rg.astype(F32))
    b_router = b_router.at[0, N_GROUPS:N_GROUPS + N_EXPERTS].set(b_re.astype(F32))
    hs, rts = [], []
    for x in x_list:
        hn, rt = norm_router(x, norm_ffn, w_router, b_router)
        hs.append(hn)
        rts.append(rt)
    hn = jnp.concatenate(hs, axis=0)
    n = hn.shape[0]
    expert_id = jnp.concatenate([rt[:, 0:TOP_K_INNER] for rt in rts], axis=0).astype(jnp.int32)
    tm = min(MOE_TILE, max(SUBLANES * 2, n * TOP_K_INNER // N_EXPERTS * 4))
    flat_e = expert_id.reshape(-1)
    na = flat_e.shape[0]
    onehot = (flat_e[:, None] == jnp.arange(N_EXPERTS, dtype=jnp.int32)[None, :]).astype(jnp.int32)
    csum = jnp.cumsum(onehot, axis=0)
    counts = csum[-1]
    padded = ((counts + tm - 1) // tm) * tm
    pad_end = jnp.cumsum(padded)
    pad_start = pad_end - padded
    dest = jnp.sum(onehot * (csum - 1 + pad_start[None, :]), axis=1)
    nt = (na + tm - 1) // tm + N_EXPERTS
    rows = nt * tm
    row_tok = jnp.zeros((rows,), jnp.int32).at[dest].set(jnp.arange(na, dtype=jnp.int32) // TOP_K_INNER)
    tile_start = jnp.arange(nt, dtype=jnp.int32) * tm
    tile_expert = jnp.minimum(jnp.sum((pad_end[None, :] <= tile_start[:, None]).astype(jnp.int32), axis=1),
                              N_EXPERTS - 1)
    tile_valid = (tile_start < pad_end[-1]).astype(jnp.int32)
    xs = hn.at[row_tok].get(mode='promise_in_bounds')
    ys = grouped_experts(xs, tile_expert, tile_valid, w_gate, w_up, w_down, tm)
    dest2 = dest.reshape(n, TOP_K_INNER)
    ya = ys.at[dest2[:, 0]].get(mode='promise_in_bounds')
    yb = ys.at[dest2[:, 1]].get(mode='promise_in_bounds')
    outs, o = [], 0
    for x, rt in zip(x_list, rts):
        outs.append(final_norm(x, ya, yb, o, rt, norm_final))
        o += x.shape[0]
    return outs


def _rope_tables(pos, rows):
    half = ROT_DIM // 2
    inv_freq = ROPE_THETA ** (-jnp.arange(0, ROT_DIM, 2, dtype=F32) / ROT_DIM)
    ang = pos.astype(F32)[:, None] * inv_freq[None, :]
    cos, sin = jnp.cos(ang), jnp.sin(ang)
    t = pos.shape[0]
    one = jnp.ones((t, DIFF_HD - ROT_DIM), F32)
    zero = jnp.zeros((t, DIFF_HD - ROT_DIM), F32)
    zh = jnp.zeros((t, half), F32)
    c = jnp.concatenate([cos, cos, one], axis=1)
    s1 = jnp.concatenate([-sin, zh, zero], axis=1)
    s2 = jnp.concatenate([zh, sin, zero], axis=1)
    rep = rows // t
    tile = lambda a: jnp.tile(jnp.concatenate([a, a], axis=1), (rep, 1))
    return tile(c), tile(s1), tile(s2)


def _prep_weights(p, dtype):
    w_in = p['w_in']
    o = 4 * GDN_WIDTH
    ob = o + 2 * GDN_HEADS
    d = w_in.shape[0]
    w = {}
    w['a'] = w_in[:, 0:o].astype(dtype)
    w['bd'] = jnp.zeros((d, LANES), dtype).at[:, 0:2 * GDN_HEADS].set(w_in[:, o:ob].astype(dtype))
    w['dq'] = w_in[:, ob:ob + DIFF_WIDTH].astype(dtype)
    w['dk'] = w_in[:, ob + DIFF_WIDTH:ob + 2 * DIFF_WIDTH].astype(dtype)
    w['dv'] = w_in[:, ob + 2 * DIFF_WIDTH:ob + 3 * DIFF_WIDTH].astype(dtype)
    w['g'] = w_in[:, ob + 3 * DIFF_WIDTH:].astype(dtype)
    for name in ('w_branch_a', 'w_branch_b', 'w_out', 'w_mq', 'w_mk', 'w_mv', 'w_mo'):
        w[name] = p[name].astype(dtype)
    return w


def _mixers(x, batch, seq, pos, conv_state, delta_state, mem_k, mem_v, p, w, lam, sample_ctx):
    m, d = x.shape
    hp = sample_ctx is not None
    od = F32 if hp else BF16
    h = rmsnorm_rows(x, p['norm_mix'], out_dtype=od)
    u_a = matmul(h, w['a'])
    u_bd = matmul(h, w['bd'])
    u_g = matmul(h, w['g'])
    tm = _tile(m, 1024)
    tables = _rope_tables(pos, max(tm, seq))
    d_v = matmul(h, w['dv'])
    d_k = matmul(h, w['dk'], rope=tables)
    chunk = min(GDN_CHUNK, seq)
    seq_pad = seq
    if chunk % SUBLANES:
        chunk = SUBLANES
        seq_pad = SUBLANES
        padr = lambda a: jnp.pad(a.reshape(batch, seq, -1), ((0, 0), (0, seq_pad - seq), (0, 0))).reshape(
            batch * seq_pad, -1)
        u_a_g, u_bd_g = padr(u_a), padr(u_bd)
    else:
        u_a_g, u_bd_g = u_a, u_bd
    state8 = jnp.pad(conv_state.astype(F32), ((0, 0), (SUBLANES - (CONV_W - 1), 0), (0, 0)))
    o_a, new_delta = gdn(u_a_g, u_bd_g, state8, delta_state.astype(F32), p['w_conv'], p['a_log'], p['dt_bias'],
                         p['gdn_norm'], batch, seq_pad, chunk, min(seq, chunk), hp)
    if seq_pad != seq:
        o_a = o_a.reshape(batch, seq_pad, GDN_WIDTH)[:, :seq].reshape(m, GDN_WIDTH)
    keep = min(seq, CONV_W - 1)
    tail = u_a.reshape(batch, seq, -1)[:, seq - keep:, 0:3 * GDN_WIDTH]
    new_conv = jnp.concatenate([conv_state.astype(F32)[:, keep:], tail], axis=1)
    if sample_ctx is None:
        d_q = matmul(h, w['dq'], out_dtype=BF16, rope=tables, scale=DIFF_HD ** -0.5 * LOG2E)
        o_b = diff_attn_prompt(d_q, d_k.astype(BF16), d_v.astype(BF16).T, lam, p['diff_norm'], batch, seq)
    else:
        cache_k, cache_v, page_table = sample_ctx
        d_q = matmul(h, w['dq'], rope=tables, scale=DIFF_HD ** -0.5)
        o_b = diff_attn_sample(d_q.reshape(batch, seq, -1), d_k.reshape(batch, seq, -1), d_v.reshape(batch, seq, -1),
                               lam, p['diff_norm'], cache_k, cache_v, page_table, seq)
        o_b = o_b[:, :seq].reshape(m, DIFF_WIDTH)
    mixed = merge_branches(o_a, o_b, w['w_branch_a'], w['w_branch_b'], u_g)
    x = matmul(mixed, w['w_out'], residual=x)
    hc = rmsnorm_rows(x, p['norm_cross'], out_dtype=od)
    mq = matmul(hc, w['w_mq'], out_dtype=od).reshape(batch, seq, MEM_WIDTH)
    if seq % SUBLANES:
        mq = jnp.pad(mq, ((0, 0), (0, SUBLANES - seq), (0, 0)))
    mo = mem_attn(mq, mem_k, mem_v)[:, :seq].reshape(m, MEM_WIDTH)
    x = matmul(mo, w['w_mo'], residual=x)
    return x, new_conv, new_delta, d_k, d_v


def kernel(x_prompt, x_sample, cache_k, cache_v, cache_mem_k, cache_mem_v, state_delta, state_conv, page_table, mem_prompt, norm_mix, w_in, w_conv, a_log, dt_bias, gdn_norm, lambda_q1, lambda_k1, lambda_q2, lambda_k2, diff_norm, w_branch_a, w_branch_b, w_out, norm_cross, norm_mem, w_mq, w_mk, w_mv, w_mo, norm_ffn, w_router_group, b_router_group, w_router_expert, b_router_expert, w_gate, w_up, w_down, norm_final):
    p = dict(norm_mix=norm_mix, w_in=w_in, w_conv=w_conv, a_log=a_log, dt_bias=dt_bias, gdn_norm=gdn_norm,
             diff_norm=diff_norm, w_branch_a=w_branch_a, w_branch_b=w_branch_b, w_out=w_out,
             norm_cross=norm_cross, w_mq=w_mq, w_mk=w_mk, w_mv=w_mv, w_mo=w_mo)
    bp, tp, d = x_prompt.shape
    bs, ts, _ = x_sample.shape
    past_len = page_table.shape[1] * PAGE_SIZE
    w = _prep_weights(p, BF16)
    w_hp = _prep_weights(p, F32)
    lam =(jnp.exp(jnp.sum(lambda_q1.astype(F32) * lambda_k1.astype(F32)))
           - jnp.exp(jnp.sum(lambda_q2.astype(F32) * lambda_k2.astype(F32))) + LAMBDA_INIT)
    mlen = mem_prompt.shape[1]
    hm = rmsnorm_rows(mem_prompt.reshape(bp * mlen, d), norm_mem)
    mem_k_p = matmul(hm, w['w_mk'])
    mem_v_p = matmul(hm, w['w_mv'])
    conv0 = jnp.zeros((bp, CONV_W - 1, 3 * GDN_WIDTH), F32)
    delta0 = jnp.zeros((bp, GDN_HEADS, GDN_HD, GDN_HD), F32)
    xp, conv_p, delta_p, k_p, v_p = _mixers(
        x_prompt.reshape(bp * tp, d), bp, tp, jnp.arange(tp), conv0, delta0,
        mem_k_p.astype(BF16).reshape(bp, mlen, MEM_WIDTH), mem_v_p.astype(BF16).reshape(bp, mlen, MEM_WIDTH),
        p, w, lam, None)
    moe_w = (norm_ffn, w_router_group, b_router_group, w_router_expert, b_router_expert, w_gate, w_up, w_down,
             norm_final)
    yp, = moe_and_final([xp], *moe_w)
    xs, conv_s, delta_s, k_s, v_s = _mixers(
        x_sample.reshape(bs * ts, d), bs, ts, past_len + jnp.arange(ts), state_conv, state_delta,
        cache_mem_k.astype(F32).reshape(bs, -1, MEM_WIDTH), cache_mem_v.astype(F32).reshape(bs, -1, MEM_WIDTH),
        p, w_hp, lam, (cache_k, cache_v, page_table))
    ys, = moe_and_final([xs], *moe_w)
    return (yp.reshape(bp, tp, d), ys.reshape(bs, ts, d),
            k_p.reshape(bp, tp, DIFF_HEADS, DIFF_VD), v_p.reshape(bp, tp, DIFF_HEADS, DIFF_VD),
            mem_k_p.reshape(bp, mlen, MEM_HEADS, MEM_HD), mem_v_p.reshape(bp, mlen, MEM_HEADS, MEM_HD),
            delta_p.astype(state_delta.dtype), conv_p.astype(x_prompt.dtype),
            k_s.reshape(bs, ts, DIFF_HEADS, DIFF_VD), v_s.reshape(bs, ts, DIFF_HEADS, DIFF_VD),
            delta_s.astype(state_delta.dtype), conv_s.astype(state_conv.dtype))
```

```python
import functools
import math

import jax
import jax.numpy as jnp
from jax import lax
from jax.experimental import pallas as pl
from jax.experimental.pallas import tpu as pltpu

F32 = jnp.float32
BF16 = jnp.bfloat16

GDN_HEADS = 8
GDN_HD = 128
GDN_WIDTH = GDN_HEADS * GDN_HD
CONV_W = 4
GDN_CHUNK = 128
DIFF_HEADS = 8
DIFF_HD = 64
DIFF_VD = 2 * DIFF_HD
DIFF_WIDTH = DIFF_HEADS * DIFF_VD
ROT_DIM = DIFF_HD // 4
ROPE_THETA = 500000.0
LAMBDA_INIT = 0.2
PAGE_SIZE = 128
MEM_HEADS = 4
MEM_HD = 128
MEM_WIDTH = MEM_HEADS * MEM_HD
N_GROUPS = 4
EXPERTS_PER_GROUP = 8
N_EXPERTS = N_GROUPS * EXPERTS_PER_GROUP
TOP_K_INNER = 2
RMS_EPS = 1e-6
L2_EPS = 1e-6
NEG_INIT = -1e30

LANES = 128
SUBLANES = 8
VMEM_LIMIT = 48 * 1024 * 1024
FLASH_TILE = 1024
ONES_ROWS = 16
PAGES_PER_STEP = 8
LOG2E = math.log2(math.e)

def _cparams(*sem):
    return pltpu.CompilerParams(dimension_semantics=sem, vmem_limit_bytes=VMEM_LIMIT)


def _tile(n, pref):
    if n <= pref:
        return n
    t = pref
    while n % t:
        t //= 2
    return t


_NN = (((1,), (0,)), ((), ()))
_NT = (((1,), (1,)), ((), ()))
_TN = (((0,), (0,)), ((), ()))


def _dotg(a, b, dims=_NN, hp=False):
    d = lambda x, y: lax.dot_general(x, y, dims, preferred_element_type=F32)
    if not hp:
        return d(a.astype(BF16), b.astype(BF16))
    (ah, am), (bh, bm) = _split2(a), _split2(b)
    return d(ah, bh) + (d(ah, bm) + d(am, bh))


def _mm(a, b):
    return _dotg(a, b, _NN, hp=(a.dtype == F32 and b.dtype == F32))


def _split3(x):
    hi = x.astype(BF16)
    r = x - hi.astype(F32)
    mid = r.astype(BF16)
    lo = (r - mid.astype(F32)).astype(BF16)
    return hi, mid, lo


def _split2(x):
    hi = x.astype(BF16)
    return hi, (x - hi.astype(F32)).astype(BF16)


def _dot3s(a2, b2):
    (ah, am), (bh, bm) = a2, b2
    d = lambda x, y: jnp.dot(x, y, preferred_element_type=F32)
    return d(ah, bh) + (d(ah, bm) + d(am, bh))


def _dot3(a, b):
    return _dot3s(_split2(a), _split2(b))


def _rmsnorm_body(x_ref, g_ref, o_ref):
    x = x_ref[...]
    ms = jnp.mean(x * x, axis=-1, keepdims=True)
    o_ref[...] = (x * lax.rsqrt(ms + RMS_EPS) * g_ref[...]).astype(o_ref.dtype)


def rmsnorm_rows(x, g, out_dtype=BF16):
    m, d = x.shape
    tm = _tile(m, 512)
    return pl.pallas_call(
        _rmsnorm_body,
        grid=(m // tm,),
        in_specs=[pl.BlockSpec((tm, d), lambda i: (i, 0)), pl.BlockSpec((1, d), lambda i: (0, 0))],
        out_specs=pl.BlockSpec((tm, d), lambda i: (i, 0)),
        out_shape=jax.ShapeDtypeStruct((m, d), out_dtype),
        compiler_params=_cparams("parallel"),
        name="rmsnorm_rows",
    )(x, g.reshape(1, d).astype(F32))


def _mm_body(a_ref, w_ref, o_ref):
    o_ref[...] = _mm(a_ref[...], w_ref[...]).astype(o_ref.dtype)


def _mm_res_body(a_ref, w_ref, r_ref, o_ref):
    o_ref[...] = r_ref[...] + _mm(a_ref[...], w_ref[...])


def _mm_rope_body(a_ref, w_ref, c_ref, s1_ref, s2_ref, o_ref, *, scale, reps):
    o = _mm(a_ref[...], w_ref[...])
    tn = o.shape[1]
    c = jnp.tile(c_ref[...], (1, reps))
    s1 = jnp.tile(s1_ref[...], (1, reps))
    s2 = jnp.tile(s2_ref[...], (1, reps))
    r = o * c + pltpu.roll(o, tn - ROT_DIM // 2, 1) * s1 + pltpu.roll(o, ROT_DIM // 2, 1) * s2
    if scale != 1.0:
        r = r * scale
    o_ref[...] = r.astype(o_ref.dtype)


def matmul(a, w, out_dtype=F32, residual=None, rope=None, scale=1.0):
    m, k = a.shape
    n = w.shape[1]
    tm = _tile(m, 1024)
    tn = _tile(n, 1024 if a.dtype == BF16 else 512)
    in_specs = [pl.BlockSpec((tm, k), lambda i, j: (i, 0)), pl.BlockSpec((k, tn), lambda i, j: (0, j))]
    args = [a, w]
    if residual is not None:
        body = _mm_res_body
        in_specs.append(pl.BlockSpec((tm, tn), lambda i, j: (i, j)))
        args.append(residual)
    elif rope is not None:
        c, s1, s2 = rope
        nt = c.shape[0] // tm
        body = functools.partial(_mm_rope_body, scale=scale, reps=tn // LANES)
        tspec = pl.BlockSpec((tm, LANES), lambda i, j: (i % nt, 0))
        in_specs += [tspec, tspec, tspec]
        args += [c, s1, s2]
    else:
        body = _mm_body
    return pl.pallas_call(
        body,
        grid=(m // tm, n // tn),
        in_specs=in_specs,
        out_specs=pl.BlockSpec((tm, tn), lambda i, j: (i, j)),
        out_shape=jax.ShapeDtypeStruct((m, n), out_dtype),
        compiler_params=_cparams("parallel", "arbitrary"),
        name="matmul",
    )(*args)


def _merge_body(oa_ref, ob_ref, wa_ref, wb_ref, ga_ref, gb_ref, o_ref):
    br_a = _mm(oa_ref[...], wa_ref[...])
    br_b = _mm(ob_ref[...], wb_ref[...])
    o_ref[...] = (jax.nn.sigmoid(ga_ref[...]) * br_a + jax.nn.sigmoid(gb_ref[...]) * br_b).astype(o_ref.dtype)


def merge_branches(oa, ob, wa, wb, gates):
    m, ka = oa.shape
    n = wa.shape[1]
    tm = _tile(m, 1024)
    tn = _tile(n, 512)
    nb = n // tn
    return pl.pallas_call(
        _merge_body,
        grid=(m // tm, nb),
        in_specs=[
            pl.BlockSpec((tm, ka), lambda i, j: (i, 0)),
            pl.BlockSpec((tm, ob.shape[1]), lambda i, j: (i, 0)),
            pl.BlockSpec((ka, tn), lambda i, j: (0, j)),
            pl.BlockSpec((wb.shape[0], tn), lambda i, j: (0, j)),
            pl.BlockSpec((tm, tn), lambda i, j: (i, j)),
            pl.BlockSpec((tm, tn), lambda i, j: (i, j + nb)),
        ],
        out_specs=pl.BlockSpec((tm, tn), lambda i, j: (i, j)),
        out_shape=jax.ShapeDtypeStruct((m, n), oa.dtype),
        compiler_params=_cparams("parallel", "arbitrary"),
        name="merge_branches",
    )(oa, ob, wa, wb, gates, gates)


def _col(arr, idx):
    lane = lax.broadcasted_iota(jnp.int32, arr.shape, 1)
    return jnp.sum(jnp.where(lane == idx, arr, 0.0), axis=1, keepdims=True)


def _row(arr, idx):
    sub = lax.broadcasted_iota(jnp.int32, arr.shape, 0)
    return jnp.sum(jnp.where(sub == idx, arr, 0.0), axis=0, keepdims=True)


def _gdn_body(cur_ref, prev_ref, st8_ref, bd_ref, s0_ref, wconv_ref, alog_ref, dtb_ref, gnorm_ref,
              o_ref, s_ref, ext_ref, *, chunk, n_valid, hp):
    c = pl.program_id(1)
    C = chunk
    H = GDN_HEADS
    W3 = 3 * GDN_WIDTH

    @pl.when(c == 0)
    def _():
        s_ref[...] = s0_ref[...]
        ext_ref[0:SUBLANES, :] = st8_ref[0]

    @pl.when(c != 0)
    def _():
        ext_ref[0:SUBLANES, :] = prev_ref[:, 0:W3]

    ext_ref[SUBLANES:SUBLANES + C, :] = cur_ref[:, 0:W3]
    base = SUBLANES - (CONV_W - 1)
    acc = ext_ref[base:base + C, :] * wconv_ref[0:1, :]
    for i in range(1, CONV_W):
        acc = acc + ext_ref[base + i:base + i + C, :] * wconv_ref[i:i + 1, :]
    qkv = acc * jax.nn.sigmoid(acc)

    bd = bd_ref[...]
    lane = lax.broadcasted_iota(jnp.int32, bd.shape, 1)
    beta_all = jax.nn.sigmoid(bd)
    xg = bd + dtb_ref[...]
    softplus = jnp.maximum(xg, 0.0) + jnp.log1p(jnp.exp(-jnp.abs(xg)))
    g_all = -jnp.exp(alog_ref[...]) * softplus
    gb = jnp.where(lane < H, beta_all, jnp.where(lane < 2 * H, g_all, 0.0))
    if n_valid < C:
        gb = jnp.where(lax.broadcasted_iota(jnp.int32, bd.shape, 0) < n_valid, gb, 0.0)
    ri = lax.broadcasted_iota(jnp.int32, (C, C), 0)
    ci = lax.broadcasted_iota(jnp.int32, (C, C), 1)
    causal = ri >= ci
    strict = ri > ci
    ltri = jnp.where(causal, 1.0, 0.0).astype(BF16)
    gc_cols = sum(jnp.dot(ltri, piece, preferred_element_type=F32) for piece in _split3(gb))
    gc_rows = gc_cols.T
    eye = jnp.where(ri == ci, 1.0, 0.0).astype(F32)

    heads = []
    for h in range(H):
        q = qkv[:, h * GDN_HD:(h + 1) * GDN_HD]
        k = qkv[:, GDN_WIDTH + h * GDN_HD:GDN_WIDTH + (h + 1) * GDN_HD]
        v = qkv[:, 2 * GDN_WIDTH + h * GDN_HD:2 * GDN_WIDTH + (h + 1) * GDN_HD]
        q = q * lax.rsqrt(jnp.sum(q * q, axis=-1, keepdims=True) + L2_EPS) * (GDN_HD ** -0.5)
        k = k * lax.rsqrt(jnp.sum(k * k, axis=-1, keepdims=True) + L2_EPS)
        beta_c = _col(gb, h)
        gc_c = _col(gc_cols, H + h)
        gc_r = _row(gc_rows, H + h)
        g_last = gc_c[C - 1:C, :]
        decay = jnp.where(causal, jnp.exp(jnp.where(causal, gc_c - gc_r, 0.0)), 0.0)
        kb = k * beta_c
        kk = _dotg(jnp.concatenate([kb, q], axis=0), k, _NT, hp)
        a_kk = jnp.where(strict, kk[0:C] * decay, 0.0)
        a_qk = jnp.where(causal, kk[C:2 * C] * decay, 0.0)
        rhs = jnp.concatenate([v * beta_c, kb * jnp.exp(gc_c)], axis=1)
        heads.append(dict(q=q, k=k, gc_c=gc_c, g_last=g_last, a_qk=a_qk, rhs=rhs, nmat=-a_kk, tinv=eye - a_kk))
    for hd in heads:
        hd['ns'] = _split2(hd['nmat'])
    span = 2
    while span < C:
        for hd in heads:
            hd['ns'] = _split2(_dot3s(hd['ns'], hd['ns']))
        for hd in heads:
            hd['tinv'] = hd['tinv'] + _dot3s(_split2(hd['tinv']), hd['ns'])
        span *= 2
    for hd in heads:
        hd['sol'] = _dot3s(_split2(hd['tinv']), _split2(hd['rhs']))
    for h, hd in enumerate(heads):
        sl = slice(h * GDN_HD, (h + 1) * GDN_HD)
        q, k, gc_c, g_last = hd['q'], hd['k'], hd['gc_c'], hd['g_last']
        u = hd['sol'][:, 0:GDN_HD]
        w = hd['sol'][:, GDN_HD:2 * GDN_HD]
        s = s_ref[0, h]
        ws = _dotg(jnp.concatenate([w, q * jnp.exp(gc_c)], axis=0), s, _NN, hp)
        v_new = u - ws[0:C]
        o = ws[C:2 * C] + _dotg(hd['a_qk'], v_new, _NN, hp)
        s_ref[0, h] = s * jnp.exp(g_last) + _dotg(k * jnp.exp(g_last - gc_c), v_new, _TN, hp)
        z = cur_ref[:, W3 + h * GDN_HD:W3 + (h + 1) * GDN_HD]
        o = o * lax.rsqrt(jnp.mean(o * o, axis=-1, keepdims=True) + RMS_EPS) * gnorm_ref[...]
        o = o * (z * jax.nn.sigmoid(z))
        o_ref[:, sl] = o.astype(o_ref.dtype)


def gdn(u_a, u_bd, state8, s0, w_conv, a_log, dt_bias, gdn_norm, batch, seq, chunk, n_valid, hp):
    nc = seq // chunk
    rb = chunk // SUBLANES
    alog = jnp.zeros((1, LANES), F32).at[0, GDN_HEADS:2 * GDN_HEADS].set(a_log.astype(F32))
    dtb = jnp.zeros((1, LANES), F32).at[0, GDN_HEADS:2 * GDN_HEADS].set(dt_bias.astype(F32))
    W3 = 3 * GDN_WIDTH
    return pl.pallas_call(
        functools.partial(_gdn_body, chunk=chunk, n_valid=n_valid, hp=hp),
        grid=(batch, nc),
        in_specs=[
            pl.BlockSpec((chunk, 4 * GDN_WIDTH), lambda b, c: (b * nc + c, 0)),
            pl.BlockSpec((SUBLANES, 4 * GDN_WIDTH), lambda b, c: (jnp.maximum((b * nc + c) * rb - 1, 0), 0)),
            pl.BlockSpec((1, SUBLANES, W3), lambda b, c: (b, 0, 0)),
            pl.BlockSpec((chunk, LANES), lambda b, c: (b * nc + c, 0)),
            pl.BlockSpec((1, GDN_HEADS, GDN_HD, GDN_HD), lambda b, c: (b, 0, 0, 0)),
            pl.BlockSpec((CONV_W, W3), lambda b, c: (0, 0)),
            pl.BlockSpec((1, LANES), lambda b, c: (0, 0)),
            pl.BlockSpec((1, LANES), lambda b, c: (0, 0)),
            pl.BlockSpec((1, GDN_HD), lambda b, c: (0, 0)),
        ],
        out_specs=[
            pl.BlockSpec((chunk, GDN_WIDTH), lambda b, c: (b * nc + c, 0)),
            pl.BlockSpec((1, GDN_HEADS, GDN_HD, GDN_HD), lambda b, c: (b, 0, 0, 0)),
        ],
        out_shape=[
            jax.ShapeDtypeStruct((batch * seq, GDN_WIDTH), F32 if hp else BF16),
            jax.ShapeDtypeStruct((batch, GDN_HEADS, GDN_HD, GDN_HD), F32),
        ],
        scratch_shapes=[pltpu.VMEM((SUBLANES + chunk, W3), F32)],
        compiler_params=_cparams("parallel", "arbitrary"),
        name="gdn",
    )(u_a, u_a, state8, u_bd, s0, w_conv.astype(F32), alog, dtb, gdn_norm.reshape(1, GDN_HD).astype(F32))


def _flash_body(qi_ref, ki_ref, lam_ref, q_ref, k_ref, vt_ref, dn_ref, o_ref, qs_ref, m_ref, acc_ref, *, tq):
    step = pl.program_id(2)
    qi = qi_ref[step]
    ki = ki_ref[step]

    @pl.when(ki == 0)
    def _():
        q = q_ref[...]
        lane = lax.broadcasted_iota(jnp.int32, q.shape, 1)
        zero = jnp.zeros_like(q)
        qs_ref[0:tq, :] = jnp.where(lane < DIFF_HD, q, zero)
        qs_ref[tq:2 * tq, :] = jnp.where(lane >= DIFF_HD, q, zero)
        m_ref[...] = jnp.full(m_ref.shape, NEG_INIT, F32)
        acc_ref[...] = jnp.zeros(acc_ref.shape, F32)

    def update(masked):
        vt1 = jnp.concatenate([vt_ref[...], jnp.ones((ONES_ROWS, vt_ref.shape[1]), BF16)], axis=0)
        s = lax.dot_general(k_ref[...], qs_ref[...], _NT, preferred_element_type=F32)
        if masked:
            key = lax.broadcasted_iota(jnp.int32, s.shape, 0)
            qry = lax.rem(lax.broadcasted_iota(jnp.int32, s.shape, 1), tq)
            s = jnp.where(key <= qry, s, NEG_INIT)
        m_prev = m_ref[...]
        m_new = jnp.maximum(m_prev, jnp.max(s, axis=0, keepdims=True))
        alpha = jnp.exp2(m_prev - m_new)
        p = jnp.exp2(s - m_new).astype(BF16)
        acc_ref[...] = alpha * acc_ref[...] + jnp.dot(vt1, p, preferred_element_type=F32)
        m_ref[...] = m_new

    @pl.when(ki < qi)
    def _():
        update(False)

    @pl.when(ki == qi)
    def _():
        update(True)
        lam = lam_ref[0]
        inv = 1.0 / acc_ref[DIFF_VD:DIFF_VD + 1, :]
        o = (acc_ref[0:DIFF_VD, 0:tq] * inv[:, 0:tq]
             - lam * (acc_ref[0:DIFF_VD, tq:2 * tq] * inv[:, tq:2 * tq]))
        o = o * lax.rsqrt(jnp.mean(o * o, axis=0, keepdims=True) + RMS_EPS)
        o_ref[...] = (o.T * dn_ref[...] * (1.0 - LAMBDA_INIT)).astype(o_ref.dtype)


def diff_attn_prompt(q, k, vt, lam, diff_norm, batch, seq):
    tq = _tile(seq, FLASH_TILE)
    nq = seq // tq
    pairs =[(i, j) for i in range(nq) for j in range(i + 1)]
    qi_tab = jnp.asarray([p[0] for p in pairs], jnp.int32)
    ki_tab = jnp.asarray([p[1] for p in pairs], jnp.int32)
    grid_spec = pltpu.PrefetchScalarGridSpec(
        num_scalar_prefetch=3,
        grid=(batch, DIFF_HEADS, len(pairs)),
        in_specs=[
            pl.BlockSpec((tq, DIFF_VD), lambda b, h, s, qt, kt, lam: (b * nq + qt[s], h)),
            pl.BlockSpec((tq, DIFF_VD), lambda b, h, s, qt, kt, lam: (b * nq + kt[s], h)),
            pl.BlockSpec((DIFF_VD, tq), lambda b, h, s, qt, kt, lam: (h, b * nq + kt[s])),
            pl.BlockSpec((1, DIFF_VD), lambda b, h, s, qt, kt, lam: (0, 0)),
        ],
        out_specs=pl.BlockSpec((tq, DIFF_VD), lambda b, h, s, qt, kt, lam: (b * nq + qt[s], h)),
        scratch_shapes=[
            pltpu.VMEM((2 * tq, DIFF_VD), BF16),
            pltpu.VMEM((1, 2 * tq), F32),
            pltpu.VMEM((DIFF_VD + ONES_ROWS, 2 * tq), F32),
        ],
    )
    return pl.pallas_call(
        functools.partial(_flash_body, tq=tq),
        grid_spec=grid_spec,
        out_shape=jax.ShapeDtypeStruct((batch * seq, DIFF_WIDTH), BF16),
        compiler_params=_cparams("parallel", "parallel", "arbitrary"),
        name="diff_attn_prompt",
    )(qi_tab, ki_tab, lam.reshape(1), q, k, vt, diff_norm.reshape(1, DIFF_VD).astype(F32))


PAGE_ROWS = PAGE_SIZE * DIFF_HEADS


def _head_mask(shape, tp, extra_mask=None):
    r = lax.broadcasted_iota(jnp.int32, shape, 0)
    c = lax.broadcasted_iota(jnp.int32, shape, 1)
    ok = jnp.bitwise_and(c, DIFF_HEADS - 1) == jnp.right_shift(r, (2 * tp).bit_length() - 1)
    if extra_mask is not None:
        ok = jnp.logical_and(ok, extra_mask(r, c))
    return jnp.where(ok, 0.0, NEG_INIT).astype(F32)


def _paged_body(pt_ref, lam_ref, q_ref, *rest, n_tok, pp, tp):
    kp_refs = rest[0:pp]
    vp_refs = rest[pp:2 * pp]
    kn_ref, vn_ref, dn_ref, o_ref, m_ref, l_ref, acc_ref, bias_ref = rest[2 * pp:]
    p = pl.program_id(1)
    npg = pl.num_programs(1)
    kshift = DIFF_HEADS.bit_length() - 1

    @pl.when(p == 0)
    def _():
        m_ref[...] = jnp.full(m_ref.shape, NEG_INIT, F32)
        l_ref[...] = jnp.zeros(l_ref.shape, F32)
        acc_ref[...] = jnp.zeros(acc_ref.shape, F32)
        bias_ref[...] = _head_mask(bias_ref.shape, tp)

    qh, qm = _split2(q_ref[0])
    qcat = jnp.concatenate([qh, qh, qm], axis=1)

    def scores(kb, bias):
        kh, km = _split2(kb)
        kcat = jnp.concatenate([kh, km, kh], axis=1)
        return lax.dot_general(qcat, kcat, _NT, preferred_element_type=F32) + bias

    def weighted_values(pr, vb):
        ph, pm = _split2(pr)
        vh, vm = _split2(vb)
        wide = jnp.dot(ph, jnp.concatenate([vh, vm], axis=1), preferred_element_type=F32)
        return (wide[:, 0:DIFF_VD] + wide[:, DIFF_VD:2 * DIFF_VD]) + jnp.dot(pm, vh, preferred_element_type=F32)

    def update(ss, vbs):
        m_prev = m_ref[...]
        m_new = m_prev
        for s in ss:
            m_new = jnp.maximum(m_new, jnp.max(s, axis=1, keepdims=True))
        alpha = jnp.exp(m_prev - m_new)
        l_new = alpha * l_ref[...]
        acc = alpha * acc_ref[...]
        for s, vb in zip(ss, vbs):
            pr = jnp.exp(s - m_new)
            l_new = l_new + jnp.sum(pr, axis=1, keepdims=True)
            acc = acc + weighted_values(pr, vb)
        l_ref[...] = l_new
        acc_ref[...] = acc
        m_ref[...] = m_new

    bias = bias_ref[...]
    update([scores(kp_refs[j][0], bias) for j in range(pp)], [vp_refs[j][0] for j in range(pp)])

    @pl.when(p == npg - 1)
    def _():
        def causal(r, c):
            tok = jnp.right_shift(c, kshift)
            return jnp.logical_and(tok <= jnp.bitwise_and(r, tp - 1), tok < n_tok)

        update([scores(kn_ref[0], _head_mask((q_ref.shape[1], kn_ref.shape[1]), tp, causal))], [vn_ref[0]])
        lam = lam_ref[0]
        inv = 1.0 / l_ref[...]
        for h in range(DIFF_HEADS):
            r0 = h * 2 * tp
            if tp == SUBLANES:
                o1 = acc_ref[r0:r0 + tp, :] * inv[r0:r0 + tp]
                o2 = acc_ref[r0 + tp:r0 + 2 * tp, :] * inv[r0 + tp:r0 + 2 * tp]
            else:
                o1 = acc_ref[r0:r0 + SUBLANES, :] * inv[r0:r0 + SUBLANES]
                o2 = pltpu.roll(o1, tp, 0)
            o = o1 - lam * o2
            o = o * lax.rsqrt(jnp.mean(o * o, axis=-1, keepdims=True) + RMS_EPS) * dn_ref[...]
            o_ref[0, h * SUBLANES:(h + 1) * SUBLANES, :] = (o * (1.0 - LAMBDA_INIT)).astype(o_ref.dtype)


def diff_attn_sample(q, k_new, v_new, lam, diff_norm, cache_k, cache_v, page_table, n_tok):
    bd = q.shape[0]
    n_pool = cache_k.shape[0]
    n_pages = page_table.shape[1]
    assert n_tok <= SUBLANES
    tp = SUBLANES // 2 if n_tok <= SUBLANES // 2 else SUBLANES
    qrows_n = 2 * DIFF_HEADS * tp
    pp = PAGES_PER_STEP
    while n_pages % pp:
        pp //= 2
    qh = q.reshape(bd, n_tok, DIFF_HEADS, 2, DIFF_HD)
    qh = jnp.pad(qh, ((0, 0), (0, tp - n_tok), (0, 0), (0, 0), (0, 0)))
    qh = jnp.transpose(qh, (0, 2, 3, 1, 4))
    zeros = jnp.zeros_like(qh[:, :, 0])
    qrows = jnp.stack([jnp.concatenate([qh[:, :, 0], zeros], axis=-1),
                       jnp.concatenate([zeros, qh[:, :, 1]], axis=-1)], axis=2).reshape(bd, qrows_n, DIFF_VD)
    new_rows = tp * DIFF_HEADS
    out_rows = SUBLANES * DIFF_HEADS
    padn = ((0, 0), (0, new_rows - n_tok * DIFF_HEADS), (0, 0))
    kn = jnp.pad(k_new.reshape(bd, n_tok * DIFF_HEADS, DIFF_VD), padn)
    vn = jnp.pad(v_new.reshape(bd, n_tok * DIFF_HEADS, DIFF_VD), padn)
    ck = cache_k.reshape(n_pool, PAGE_ROWS, DIFF_VD)
    cv = cache_v.reshape(n_pool, PAGE_ROWS, DIFF_VD)

    def page_spec(j):
        return pl.BlockSpec((1, PAGE_ROWS, DIFF_VD), lambda b, p, pt, lam: (pt[b, p * pp + j], 0, 0))

    grid_spec = pltpu.PrefetchScalarGridSpec(
        num_scalar_prefetch=2,
        grid=(bd, n_pages // pp),
        in_specs=(
            [pl.BlockSpec((1, qrows_n, DIFF_VD), lambda b, p, pt, lam: (b, 0, 0))]
            + [page_spec(j) for j in range(pp)]
            + [page_spec(j) for j in range(pp)]
            + [pl.BlockSpec((1, new_rows, DIFF_VD), lambda b, p, pt, lam: (b, 0, 0)),
               pl.BlockSpec((1, new_rows, DIFF_VD), lambda b, p, pt, lam: (b, 0, 0)),
               pl.BlockSpec((1, DIFF_VD), lambda b, p, pt, lam: (0, 0))]
        ),
        out_specs=pl.BlockSpec((1, out_rows, DIFF_VD), lambda b, p, pt, lam: (b, 0, 0)),
        scratch_shapes=[
            pltpu.VMEM((qrows_n, 1), F32),
            pltpu.VMEM((qrows_n, 1), F32),
            pltpu.VMEM((qrows_n, DIFF_VD), F32),
            pltpu.VMEM((qrows_n, PAGE_ROWS), F32),
        ],
    )
    out = pl.pallas_call(
        functools.partial(_paged_body, n_tok=n_tok, pp=pp, tp=tp),
        grid_spec=grid_spec,
        out_shape=jax.ShapeDtypeStruct((bd, out_rows, DIFF_VD), F32),
        compiler_params=_cparams("parallel", "arbitrary"),
        name="diff_attn_sample",
    )(page_table, lam.reshape(1), qrows, *([ck] * pp), *([cv] * pp), kn, vn,
      diff_norm.reshape(1, DIFF_VD).astype(F32))
    out = out.reshape(bd, DIFF_HEADS, SUBLANES, DIFF_VD)[:, :, :n_tok]
    return jnp.transpose(out, (0, 2, 1, 3)).reshape(bd, n_tok, DIFF_WIDTH)


def _mem_body(q_ref, k_ref, v_ref, o_ref):
    hp = q_ref.dtype == F32
    for h in range(MEM_HEADS):
        sl = slice(h * MEM_HD, (h + 1) * MEM_HD)
        s = _dotg(q_ref[0, :, sl], k_ref[0, :, sl], _NT, hp) * (MEM_HD ** -0.5)
        s = s - jnp.max(s, axis=1, keepdims=True)
        e = jnp.exp(s)
        p = e / jnp.sum(e, axis=1, keepdims=True)
        o_ref[0, :, sl] = _dotg(p, v_ref[0, :, sl], _NN, hp).astype(o_ref.dtype)


def mem_attn(q, mem_k, mem_v):
    b, t, _ = q.shape
    mlen = mem_k.shape[1]
    tq = _tile(t, 512)
    return pl.pallas_call(
        _mem_body,
        grid=(b, t // tq),
        in_specs=[
            pl.BlockSpec((1, tq, MEM_WIDTH), lambda i, j: (i, j, 0)),
            pl.BlockSpec((1, mlen, MEM_WIDTH), lambda i, j: (i, 0, 0)),
            pl.BlockSpec((1, mlen, MEM_WIDTH), lambda i, j: (i, 0, 0)),
        ],
        out_specs=pl.BlockSpec((1, tq, MEM_WIDTH), lambda i, j: (i, j, 0)),
        out_shape=jax.ShapeDtypeStruct((b, t, MEM_WIDTH), q.dtype),
        compiler_params=_cparams("parallel", "arbitrary"),
        name="mem_attn",
    )(q, mem_k, mem_v)


def _norm_router_body(x_ref, g_ref, wr_ref, br_ref, h_ref, rt_ref):
    x = x_ref[...]
    hn = x * lax.rsqrt(jnp.mean(x * x, axis=-1, keepdims=True) + RMS_EPS) * g_ref[...]
    h_ref[...] = hn.astype(h_ref.dtype)
    lg = _dot3(hn, wr_ref[...]) + br_ref[...]
    lane = lax.broadcasted_iota(jnp.int32, lg.shape, 1)
    lanef = lane.astype(F32)
    big = float(LANES)

    def first_argmax(vals, mask):
        top = jnp.max(jnp.where(mask, vals, NEG_INIT), axis=1, keepdims=True)
        idx = jnp.min(jnp.where(jnp.logical_and(mask, vals == top), lanef, big), axis=1, keepdims=True)
        return top, idx

    gmask = lane < N_GROUPS
    gtop, gidx = first_argmax(lg, gmask)
    g_w = 1.0 / jnp.sum(jnp.where(gmask, jnp.exp(lg - gtop), 0.0), axis=1, keepdims=True)
    first = N_GROUPS + EXPERTS_PER_GROUP * gidx.astype(jnp.int32)
    emask = jnp.logical_and(lane >= first, lane < first + EXPERTS_PER_GROUP)
    e1, i1 = first_argmax(lg, emask)
    e2, i2 = first_argmax(lg, jnp.logical_and(emask, lanef != i1))
    ez = jnp.sum(jnp.where(emask, jnp.exp(lg - e1), 0.0), axis=1, keepdims=True)
    p1 = 1.0 / ez
    p2 = jnp.exp(e2 - e1) / ez
    psum = p1 + p2
    rt_ref[...] = jnp.where(lane == 0, i1 - N_GROUPS,
                            jnp.where(lane == 1, i2 - N_GROUPS,
                                      jnp.where(lane == 2, g_w * (p1 / psum),
                                                jnp.where(lane == 3, g_w * (p2 / psum), 0.0))))


def norm_router(x, g, w_router, b_router):
    m, d = x.shape
    tm = _tile(m, 512)
    return pl.pallas_call(
        _norm_router_body,
        grid=(m // tm,),
        in_specs=[
            pl.BlockSpec((tm, d), lambda i: (i, 0)),
            pl.BlockSpec((1, d), lambda i: (0, 0)),
            pl.BlockSpec((d, LANES), lambda i: (0, 0)),
            pl.BlockSpec((1, LANES), lambda i: (0, 0)),
        ],
        out_specs=[pl.BlockSpec((tm, d), lambda i: (i, 0)), pl.BlockSpec((tm, LANES), lambda i: (i, 0))],
        out_shape=[jax.ShapeDtypeStruct((m, d), F32), jax.ShapeDtypeStruct((m, LANES), F32)],
        compiler_params=_cparams("parallel"),
        name="norm_router",
    )(x, g.reshape(1, d).astype(F32), w_router, b_router)


def _expert_body(te_ref, tv_ref, x_ref, wg_ref, wu_ref, wd_ref, o_ref):
    i = pl.program_id(0)

    @pl.when(tv_ref[i] != 0)
    def _():
        x = x_ref[...].astype(BF16)
        a = jnp.dot(x, wg_ref[0].astype(BF16), preferred_element_type=F32)
        b = jnp.dot(x, wu_ref[0].astype(BF16), preferred_element_type=F32)
        hid = (a * jax.nn.sigmoid(a)) * b
        o_ref[...] = jnp.dot(hid.astype(BF16), wd_ref[0].astype(BF16), preferred_element_type=F32)

    @pl.when(tv_ref[i] == 0)
    def _():
        o_ref[...] = jnp.zeros(o_ref.shape, o_ref.dtype)


def grouped_experts(xs, tile_expert, tile_valid, w_gate, w_up, w_down, tm):
    rows, d = xs.shape
    nt = rows // tm
    ff = w_gate.shape[2]
    grid_spec = pltpu.PrefetchScalarGridSpec(
        num_scalar_prefetch=2,
        grid=(nt,),
        in_specs=[
            pl.BlockSpec((tm, d), lambda i, te, tv: (i, 0)),
            pl.BlockSpec((1, d, ff), lambda i, te, tv: (te[i], 0, 0)),
            pl.BlockSpec((1, d, ff), lambda i, te, tv: (te[i], 0, 0)),
            pl.BlockSpec((1, ff, d), lambda i, te, tv: (te[i], 0, 0)),
        ],
        out_specs=pl.BlockSpec((tm, d), lambda i, te, tv: (i, 0)),
    )
    return pl.pallas_call(
        _expert_body,
        grid_spec=grid_spec,
        out_shape=jax.ShapeDtypeStruct((rows, d), F32),
        compiler_params=_cparams("arbitrary"),
        name="grouped_experts",
    )(tile_expert, tile_valid, xs, w_gate, w_up, w_down)


def _final_body(x_ref, ya_ref, yb_ref, rt_ref, g_ref, o_ref):
    rt = rt_ref[...]
    x = x_ref[...] + (_col(rt, 2) * ya_ref[...] + _col(rt, 3) * yb_ref[...])
    o_ref[...] = x * lax.rsqrt(jnp.mean(x * x, axis=-1, keepdims=True) + RMS_EPS) * g_ref[...]


def final_norm(x, ya, yb, row0, route, g):
    m, d = x.shape
    tm = _tile(m, 512)
    assert row0 % tm == 0
    blk0 = row0 // tm
    spec = pl.BlockSpec((tm, d), lambda i: (i, 0))
    yspec = pl.BlockSpec((tm, d), lambda i: (i + blk0, 0))
    return pl.pallas_call(
        _final_body,
        grid=(m // tm,),
        in_specs=[spec, yspec, yspec, pl.BlockSpec((tm, LANES), lambda i: (i, 0)),
                  pl.BlockSpec((1, d), lambda i: (0, 0))],
        out_specs=spec,
        out_shape=jax.ShapeDtypeStruct((m, d), F32),
        compiler_params=_cparams("parallel"),
        name="final_norm",
    )(x, ya, yb, route, g.reshape(1, d).astype(F32))


MOE_TILE = 256


def moe_and_final(x_list, norm_ffn, w_rg, b_rg, w_re, b_re, w_gate, w_up, w_down, norm_final):
    d = x_list[0].shape[1]
    w_router = jnp.zeros((d, LANES), F32).at[:, 0:N_GROUPS].set(w_rg.astype(F32))
    w_router = w_router.at[:, N_GROUPS:N_GROUPS + N_EXPERTS].set(w_re.astype(F32))
    b_router = jnp.zeros((1, LANES), F32).at[0, 0:N_GROUPS].set(b_rg.astype(F32))
    b_router = b_router.at[0, N_GROUPS:N_GROUPS + N_EXPERTS].set(b_re.astype(F32))
    hs, rts = [], []
    for x in x_list:
        hn, rt = norm_router(x, norm_ffn, w_router, b_router)
        hs.append(hn)
        rts.append(rt)
    hn = jnp.concatenate(hs, axis=0)
    n = hn.shape[0]
    expert_id = jnp.concatenate([rt[:, 0:TOP_K_INNER] for rt in rts], axis=0).astype(jnp.int32)
    tm = min(MOE_TILE, max(SUBLANES * 2, n * TOP_K_INNER // N_EXPERTS * 4))
    flat_e = expert_id.reshape(-1)
    na = flat_e.shape[0]
    onehot = (flat_e[:, None] == jnp.arange(N_EXPERTS, dtype=jnp.int32)[None, :]).astype(jnp.int32)
    csum = jnp.cumsum(onehot, axis=0)
    counts = csum[-1]
    padded = ((counts + tm - 1) // tm) * tm
    pad_end = jnp.cumsum(padded)
    pad_start = pad_end - padded
    dest = jnp.sum(onehot * (csum - 1 + pad_start[None, :]), axis=1)
    nt = (na + tm - 1) // tm + N_EXPERTS
    rows = nt * tm
    row_tok = jnp.zeros((rows,), jnp.int32).at[dest].set(jnp.arange(na, dtype=jnp.int32) // TOP_K_INNER)
    tile_start = jnp.arange(nt, dtype=jnp.int32) * tm
    tile_expert = jnp.minimum(jnp.sum((pad_end[None, :] <= tile_start[:, None]).astype(jnp.int32), axis=1),
                              N_EXPERTS - 1)
    tile_valid = (tile_start < pad_end[-1]).astype(jnp.int32)
    xs = hn.at[row_tok].get(mode='promise_in_bounds')
    ys = grouped_experts(xs, tile_expert, tile_valid, w_gate, w_up, w_down, tm)
    dest2 = dest.reshape(n, TOP_K_INNER)
    ya = ys.at[dest2[:, 0]].get(mode='promise_in_bounds')
    yb = ys.at[dest2[:, 1]].get(mode='promise_in_bounds')
    outs, o = [], 0
    for x, rt in zip(x_list, rts):
        outs.append(final_norm(x, ya, yb, o, rt, norm_final))
        o += x.shape[0]
    return outs


def _rope_tables(pos, rows):
    half = ROT_DIM // 2
    inv_freq = ROPE_THETA ** (-jnp.arange(0, ROT_DIM, 2, dtype=F32) / ROT_DIM)
    ang = pos.astype(F32)[:, None] * inv_freq[None, :]
    cos, sin = jnp.cos(ang), jnp.sin(ang)
    t = pos.shape[0]
    one = jnp.ones((t, DIFF_HD - ROT_DIM), F32)
    zero = jnp.zeros((t, DIFF_HD - ROT_DIM), F32)
    zh = jnp.zeros((t, half), F32)
    c = jnp.concatenate([cos, cos, one], axis=1)
    s1 = jnp.concatenate([-sin, zh, zero], axis=1)
    s2 = jnp.concatenate([zh, sin, zero], axis=1)
    rep = rows // t
    tile = lambda a: jnp.tile(jnp.concatenate([a, a], axis=1), (rep, 1))
    return tile(c), tile(s1), tile(s2)


def _prep_weights(p, dtype):
    w_in = p['w_in']
    o = 4 * GDN_WIDTH
    ob = o + 2 * GDN_HEADS
    d = w_in.shape[0]
    w = {}
    w['a'] = w_in[:, 0:o].astype(dtype)
    w['bd'] = jnp.zeros((d, LANES), dtype).at[:, 0:2 * GDN_HEADS].set(w_in[:, o:ob].astype(dtype))
    w['dq'] = w_in[:, ob:ob + DIFF_WIDTH].astype(dtype)
    w['dk'] = w_in[:, ob + DIFF_WIDTH:ob + 2 * DIFF_WIDTH].astype(dtype)
    w['dv'] = w_in[:, ob + 2 * DIFF_WIDTH:ob + 3 * DIFF_WIDTH].astype(dtype)
    w['g'] = w_in[:, ob + 3 * DIFF_WIDTH:].astype(dtype)
    for name in ('w_branch_a', 'w_branch_b', 'w_out', 'w_mq', 'w_mk', 'w_mv', 'w_mo'):
        w[name] = p[name].astype(dtype)
    return w


def _mixers(x, batch, seq, pos, conv_state, delta_state, mem_k, mem_v, p, w, lam, sample_ctx):
    m, d = x.shape
    hp = sample_ctx is not None
    od = F32 if hp else BF16
    h = rmsnorm_rows(x, p['norm_mix'], out_dtype=od)
    u_a = matmul(h, w['a'])
    u_bd = matmul(h, w['bd'])
    u_g = matmul(h, w['g'])
    tm = _tile(m, 1024)
    tables = _rope_tables(pos, max(tm, seq))
    d_v = matmul(h, w['dv'])
    d_k = matmul(h, w['dk'], rope=tables)
    chunk = min(GDN_CHUNK, seq)
    seq_pad = seq
    if chunk % SUBLANES:
        chunk = SUBLANES
        seq_pad = SUBLANES
        padr = lambda a: jnp.pad(a.reshape(batch, seq, -1), ((0, 0), (0, seq_pad - seq), (0, 0))).reshape(
            batch * seq_pad, -1)
        u_a_g, u_bd_g = padr(u_a), padr(u_bd)
    else:
        u_a_g, u_bd_g = u_a, u_bd
    state8 = jnp.pad(conv_state.astype(F32), ((0, 0), (SUBLANES - (CONV_W - 1), 0), (0, 0)))
    o_a, new_delta = gdn(u_a_g, u_bd_g, state8, delta_state.astype(F32), p['w_conv'], p['a_log'], p['dt_bias'],
                         p['gdn_norm'], batch, seq_pad, chunk, min(seq, chunk), hp)
    if seq_pad != seq:
        o_a = o_a.reshape(batch, seq_pad, GDN_WIDTH)[:, :seq].reshape(m, GDN_WIDTH)
    keep = min(seq, CONV_W - 1)
    tail = u_a.reshape(batch, seq, -1)[:, seq - keep:, 0:3 * GDN_WIDTH]
    new_conv = jnp.concatenate([conv_state.astype(F32)[:, keep:], tail], axis=1)
    if sample_ctx is None:
        d_q = matmul(h, w['dq'], out_dtype=BF16, rope=tables, scale=DIFF_HD ** -0.5 * LOG2E)
        o_b = diff_attn_prompt(d_q, d_k.astype(BF16), d_v.astype(BF16).T, lam, p['diff_norm'], batch, seq)
    else:
        cache_k, cache_v, page_table = sample_ctx
        d_q = matmul(h, w['dq'], rope=tables, scale=DIFF_HD ** -0.5)
        o_b = diff_attn_sample(d_q.reshape(batch, seq, -1), d_k.reshape(batch, seq, -1), d_v.reshape(batch, seq, -1),
                               lam, p['diff_norm'], cache_k, cache_v, page_table, seq)
        o_b = o_b[:, :seq].reshape(m, DIFF_WIDTH)
    mixed = merge_branches(o_a, o_b, w['w_branch_a'], w['w_branch_b'], u_g)
    x = matmul(mixed, w['w_out'], residual=x)
    hc = rmsnorm_rows(x, p['norm_cross'], out_dtype=od)
    mq = matmul(hc, w['w_mq'], out_dtype=od).reshape(batch, seq, MEM_WIDTH)
    if seq % SUBLANES:
        mq = jnp.pad(mq, ((0, 0), (0, SUBLANES - seq), (0, 0)))
    mo = mem_attn(mq, mem_k, mem_v)[:, :seq].reshape(m, MEM_WIDTH)
    x = matmul(mo, w['w_mo'], residual=x)
    return x, new_conv, new_delta, d_k, d_v


def kernel(x_prompt, x_sample, cache_k, cache_v, cache_mem_k, cache_mem_v, state_delta, state_conv, page_table, mem_prompt, norm_mix, w_in, w_conv, a_log, dt_bias, gdn_norm, lambda_q1, lambda_k1, lambda_q2, lambda_k2, diff_norm, w_branch_a, w_branch_b, w_out, norm_cross, norm_mem, w_mq, w_mk, w_mv, w_mo, norm_ffn, w_router_group, b_router_group, w_router_expert, b_router_expert, w_gate, w_up, w_down, norm_final):
    p = dict(norm_mix=norm_mix, w_in=w_in, w_conv=w_conv, a_log=a_log, dt_bias=dt_bias, gdn_norm=gdn_norm,
             diff_norm=diff_norm, w_branch_a=w_branch_a, w_branch_b=w_branch_b, w_out=w_out,
             norm_cross=norm_cross, w_mq=w_mq, w_mk=w_mk, w_mv=w_mv, w_mo=w_mo)
    bp, tp, d = x_prompt.shape
    bs, ts, _ = x_sample.shape
    past_len = page_table.shape[1] * PAGE_SIZE
    w = _prep_weights(p, BF16)
    w_hp = _prep_weights(p, F32)
    lam =(jnp.exp(jnp.sum(lambda_q1.astype(F32) * lambda_k1.astype(F32)))
           - jnp.exp(jnp.sum(lambda_q2.astype(F32) * lambda_k2.astype(F32))) + LAMBDA_INIT)
    mlen = mem_prompt.shape[1]
    hm = rmsnorm_rows(mem_prompt.reshape(bp * mlen, d), norm_mem)
    mem_k_p = matmul(hm, w['w_mk'])
    mem_v_p = matmul(hm, w['w_mv'])
    conv0 = jnp.zeros((bp, CONV_W - 1, 3 * GDN_WIDTH), F32)
    delta0 = jnp.zeros((bp, GDN_HEADS, GDN_HD, GDN_HD), F32)
    xp, conv_p, delta_p, k_p, v_p = _mixers(
        x_prompt.reshape(bp * tp, d), bp, tp, jnp.arange(tp), conv0, delta0,
        mem_k_p.astype(BF16).reshape(bp, mlen, MEM_WIDTH), mem_v_p.astype(BF16).reshape(bp, mlen, MEM_WIDTH),
        p, w, lam, None)
    moe_w = (norm_ffn, w_router_group, b_router_group, w_router_expert, b_router_expert, w_gate, w_up, w_down,
             norm_final)
    yp, = moe_and_final([xp], *moe_w)
    xs, conv_s, delta_s, k_s, v_s = _mixers(
        x_sample.reshape(bs * ts, d), bs, ts, past_len + jnp.arange(ts), state_conv, state_delta,
        cache_mem_k.astype(F32).reshape(bs, -1, MEM_WIDTH), cache_mem_v.astype(F32).reshape(bs, -1, MEM_WIDTH),
        p, w_hp, lam, (cache_k, cache_v, page_table))
    ys, = moe_and_final([xs], *moe_w)
    return (yp.reshape(bp, tp, d), ys.reshape(bs, ts, d),
            k_p.reshape(bp, tp, DIFF_HEADS, DIFF_VD), v_p.reshape(bp, tp, DIFF_HEADS, DIFF_VD),
            mem_k_p.reshape(bp, mlen, MEM_HEADS, MEM_HD), mem_v_p.reshape(bp, mlen, MEM_HEADS, MEM_HD),
            delta_p.astype(state_delta.dtype), conv_p.astype(x_prompt.dtype),
            k_s.reshape(bs, ts, DIFF_HEADS, DIFF_VD), v_s.reshape(bs, ts, DIFF_HEADS, DIFF_VD),
            delta_s.astype(state_delta.dtype), conv_s.astype(state_conv.dtype))
```

```python
import functools
import math

import jax
import jax.numpy as jnp
from jax import lax
from jax.experimental import pallas as pl
from jax.experimental.pallas import tpu as pltpu

F32 = jnp.float32
BF16 = jnp.bfloat16

GDN_HEADS = 8
GDN_HD = 128
GDN_WIDTH = GDN_HEADS * GDN_HD
CONV_W = 4
GDN_CHUNK = 128
DIFF_HEADS = 8
DIFF_HD = 64
DIFF_VD = 2 * DIFF_HD
DIFF_WIDTH = DIFF_HEADS * DIFF_VD
ROT_DIM = DIFF_HD // 4
ROPE_THETA = 500000.0
LAMBDA_INIT = 0.2
PAGE_SIZE = 128
MEM_HEADS = 4
MEM_HD = 128
MEM_WIDTH = MEM_HEADS * MEM_HD
N_GROUPS = 4
EXPERTS_PER_GROUP = 8
N_EXPERTS = N_GROUPS * EXPERTS_PER_GROUP
TOP_K_INNER = 2
RMS_EPS = 1e-6
L2_EPS = 1e-6
NEG_INIT = -1e30

LANES = 128
SUBLANES = 8
VMEM_LIMIT = 48 * 1024 * 1024
FLASH_TILE = 1024
ONES_ROWS = 16
PAGES_PER_STEP = 8
LOG2E = math.log2(math.e)

def _cparams(*sem):
    return pltpu.CompilerParams(dimension_semantics=sem, vmem_limit_bytes=VMEM_LIMIT)


def _tile(n, pref):
    if n <= pref:
        return n
    t = pref
    while n % t:
        t //= 2
    return t


_NN = (((1,), (0,)), ((), ()))
_NT = (((1,), (1,)), ((), ()))
_TN = (((0,), (0,)), ((), ()))


def _dotg(a, b, dims=_NN, hp=False):
    d = lambda x, y: lax.dot_general(x, y, dims, preferred_element_type=F32)
    if not hp:
        return d(a.astype(BF16), b.astype(BF16))
    (ah, am), (bh, bm) = _split2(a), _split2(b)
    return d(ah, bh) + (d(ah, bm) + d(am, bh))


def _mm(a, b):
    return _dotg(a, b, _NN, hp=(a.dtype == F32 and b.dtype == F32))


def _split3(x):
    hi = x.astype(BF16)
    r = x - hi.astype(F32)
    mid = r.astype(BF16)
    lo = (r - mid.astype(F32)).astype(BF16)
    return hi, mid, lo


def _split2(x):
    hi = x.astype(BF16)
    return hi, (x - hi.astype(F32)).astype(BF16)


def _dot3s(a2, b2):
    (ah, am), (bh, bm) = a2, b2
    d = lambda x, y: jnp.dot(x, y, preferred_element_type=F32)
    return d(ah, bh) + (d(ah, bm) + d(am, bh))


def _dot3(a, b):
    return _dot3s(_split2(a), _split2(b))


def _rmsnorm_body(x_ref, g_ref, o_ref):
    x = x_ref[...]
    ms = jnp.mean(x * x, axis=-1, keepdims=True)
    o_ref[...] = (x * lax.rsqrt(ms + RMS_EPS) * g_ref[...]).astype(o_ref.dtype)


def rmsnorm_rows(x, g, out_dtype=BF16):
    m, d = x.shape
    tm = _tile(m, 512)
    return pl.pallas_call(
        _rmsnorm_body,
        grid=(m // tm,),
        in_specs=[pl.BlockSpec((tm, d), lambda i: (i, 0)), pl.BlockSpec((1, d), lambda i: (0, 0))],
        out_specs=pl.BlockSpec((tm, d), lambda i: (i, 0)),
        out_shape=jax.ShapeDtypeStruct((m, d), out_dtype),
        compiler_params=_cparams("parallel"),
        name="rmsnorm_rows",
    )(x, g.reshape(1, d).astype(F32))


def _mm_body(a_ref, w_ref, o_ref):
    o_ref[...] = _mm(a_ref[...], w_ref[...]).astype(o_ref.dtype)


def _mm_res_body(a_ref, w_ref, r_ref, o_ref):
    o_ref[...] = r_ref[...] + _mm(a_ref[...], w_ref[...])


def _mm_rope_body(a_ref, w_ref, c_ref, s1_ref, s2_ref, o_ref, *, scale, reps):
    o = _mm(a_ref[...], w_ref[...])
    tn = o.shape[1]
    c = jnp.tile(c_ref[...], (1, reps))
    s1 = jnp.tile(s1_ref[...], (1, reps))
    s2 = jnp.tile(s2_ref[...], (1, reps))
    r = o * c + pltpu.roll(o, tn - ROT_DIM // 2, 1) * s1 + pltpu.roll(o, ROT_DIM // 2, 1) * s2
    if scale != 1.0:
        r = r * scale
    o_ref[...] = r.astype(o_ref.dtype)


def matmul(a, w, out_dtype=F32, residual=None, rope=None, scale=1.0):
    m, k = a.shape
    n = w.shape[1]
    tm = _tile(m, 1024)
    tn = _tile(n, 1024 if a.dtype == BF16 else 512)
    in_specs = [pl.BlockSpec((tm, k), lambda i, j: (i, 0)), pl.BlockSpec((k, tn), lambda i, j: (0, j))]
    args = [a, w]
    if residual is not None:
        body = _mm_res_body
        in_specs.append(pl.BlockSpec((tm, tn), lambda i, j: (i, j)))
        args.append(residual)
    elif rope is not None:
        c, s1, s2 = rope
        nt = c.shape[0] // tm
        body = functools.partial(_mm_rope_body, scale=scale, reps=tn // LANES)
        tspec = pl.BlockSpec((tm, LANES), lambda i, j: (i % nt, 0))
        in_specs += [tspec, tspec, tspec]
        args += [c, s1, s2]
    else:
        body = _mm_body
    return pl.pallas_call(
        body,
        grid=(m // tm, n // tn),
        in_specs=in_specs,
        out_specs=pl.BlockSpec((tm, tn), lambda i, j: (i, j)),
        out_shape=jax.ShapeDtypeStruct((m, n), out_dtype),
        compiler_params=_cparams("parallel", "arbitrary"),
        name="matmul",
    )(*args)


def _merge_body(oa_ref, ob_ref, wa_ref, wb_ref, ga_ref, gb_ref, o_ref):
    br_a = _mm(oa_ref[...], wa_ref[...])
    br_b = _mm(ob_ref[...], wb_ref[...])
    o_ref[...] = (jax.nn.sigmoid(ga_ref[...]) * br_a + jax.nn.sigmoid(gb_ref[...]) * br_b).astype(o_ref.dtype)


def merge_branches(oa, ob, wa, wb, gates):
    m, ka = oa.shape
    n = wa.shape[1]
    tm = _tile(m, 1024)
    tn = _tile(n, 512)
    nb = n // tn
    return pl.pallas_call(
        _merge_body,
        grid=(m // tm, nb),
        in_specs=[
            pl.BlockSpec((tm, ka), lambda i, j: (i, 0)),
            pl.BlockSpec((tm, ob.shape[1]), lambda i, j: (i, 0)),
            pl.BlockSpec((ka, tn), lambda i, j: (0, j)),
            pl.BlockSpec((wb.shape[0], tn), lambda i, j: (0, j)),
            pl.BlockSpec((tm, tn), lambda i, j: (i, j)),
            pl.BlockSpec((tm, tn), lambda i, j: (i, j + nb)),
        ],
        out_specs=pl.BlockSpec((tm, tn), lambda i, j: (i, j)),
        out_shape=jax.ShapeDtypeStruct((m, n), oa.dtype),
        compiler_params=_cparams("parallel", "arbitrary"),
        name="merge_branches",
    )(oa, ob, wa, wb, gates, gates)


def _col(arr, idx):
    lane = lax.broadcasted_iota(jnp.int32, arr.shape, 1)
    return jnp.sum(jnp.where(lane == idx, arr, 0.0), axis=1, keepdims=True)


def _row(arr, idx):
    sub = lax.broadcasted_iota(jnp.int32, arr.shape, 0)
    return jnp.sum(jnp.where(sub == idx, arr, 0.0), axis=0, keepdims=True)


def _gdn_body(cur_ref, prev_ref, st8_ref, bd_ref, s0_ref, wconv_ref, alog_ref, dtb_ref, gnorm_ref,
              o_ref, s_ref, ext_ref, *, chunk, n_valid, hp):
    c = pl.program_id(1)
    C = chunk
    H = GDN_HEADS
    W3 = 3 * GDN_WIDTH

    @pl.when(c == 0)
    def _():
        s_ref[...] = s0_ref[...]
        ext_ref[0:SUBLANES, :] = st8_ref[0]

    @pl.when(c != 0)
    def _():
        ext_ref[0:SUBLANES, :] = prev_ref[:, 0:W3]

    ext_ref[SUBLANES:SUBLANES + C, :] = cur_ref[:, 0:W3]
    base = SUBLANES - (CONV_W - 1)
    acc = ext_ref[base:base + C, :] * wconv_ref[0:1, :]
    for i in range(1, CONV_W):
        acc = acc + ext_ref[base + i:base + i + C, :] * wconv_ref[i:i + 1, :]
    qkv = acc * jax.nn.sigmoid(acc)

    bd = bd_ref[...]
    lane = lax.broadcasted_iota(jnp.int32, bd.shape, 1)
    beta_all = jax.nn.sigmoid(bd)
    xg = bd + dtb_ref[...]
    softplus = jnp.maximum(xg, 0.0) + jnp.log1p(jnp.exp(-jnp.abs(xg)))
    g_all = -jnp.exp(alog_ref[...]) * softplus
    gb = jnp.where(lane < H, beta_all, jnp.where(lane < 2 * H, g_all, 0.0))
    if n_valid < C:
        gb = jnp.where(lax.broadcasted_iota(jnp.int32, bd.shape, 0) < n_valid, gb, 0.0)
    ri = lax.broadcasted_iota(jnp.int32, (C, C), 0)
    ci = lax.broadcasted_iota(jnp.int32, (C, C), 1)
    causal = ri >= ci
    strict = ri > ci
    ltri = jnp.where(causal, 1.0, 0.0).astype(BF16)
    gc_cols = sum(jnp.dot(ltri, piece, preferred_element_type=F32) for piece in _split3(gb))
    gc_rows = gc_cols.T
    eye = jnp.where(ri == ci, 1.0, 0.0).astype(F32)

    heads = []
    for h in range(H):
        q = qkv[:, h * GDN_HD:(h + 1) * GDN_HD]
        k = qkv[:, GDN_WIDTH + h * GDN_HD:GDN_WIDTH + (h + 1) * GDN_HD]
        v = qkv[:, 2 * GDN_WIDTH + h * GDN_HD:2 * GDN_WIDTH + (h + 1) * GDN_HD]
        q = q * lax.rsqrt(jnp.sum(q * q, axis=-1, keepdims=True) + L2_EPS) * (GDN_HD ** -0.5)
        k = k * lax.rsqrt(jnp.sum(k * k, axis=-1, keepdims=True) + L2_EPS)
        beta_c = _col(gb, h)
        gc_c = _col(gc_cols, H + h)
        gc_r = _row(gc_rows, H + h)
        g_last = gc_c[C - 1:C, :]
        decay = jnp.where(causal, jnp.exp(jnp.where(causal, gc_c - gc_r, 0.0)), 0.0)
        kb = k * beta_c
        kk = _dotg(jnp.concatenate([kb, q], axis=0), k, _NT, hp)
        a_kk = jnp.where(strict, kk[0:C] * decay, 0.0)
        a_qk = jnp.where(causal, kk[C:2 * C] * decay, 0.0)
        rhs = jnp.concatenate([v * beta_c, kb * jnp.exp(gc_c)], axis=1)
        heads.append(dict(q=q, k=k, gc_c=gc_c, g_last=g_last, a_qk=a_qk, rhs=rhs, nmat=-a_kk, tinv=eye - a_kk))
    for hd in heads:
        hd['ns'] = _split2(hd['nmat'])
    span = 2
    while span < C:
        for hd in heads:
            hd['ns'] = _split2(_dot3s(hd['ns'], hd['ns']))
        for hd in heads:
            hd['tinv'] = hd['tinv'] + _dot3s(_split2(hd['tinv']), hd['ns'])
        span *= 2
    for hd in heads:
        hd['sol'] = _dot3s(_split2(hd['tinv']), _split2(hd['rhs']))
    for h, hd in enumerate(heads):
        sl = slice(h * GDN_HD, (h + 1) * GDN_HD)
        q, k, gc_c, g_last = hd['q'], hd['k'], hd['gc_c'], hd['g_last']
        u = hd['sol'][:, 0:GDN_HD]
        w = hd['sol'][:, GDN_HD:2 * GDN_HD]
        s = s_ref[0, h]
        ws = _dotg(jnp.concatenate([w, q * jnp.exp(gc_c)], axis=0), s, _NN, hp)
        v_new = u - ws[0:C]
        o = ws[C:2 * C] + _dotg(hd['a_qk'], v_new, _NN, hp)
        s_ref[0, h] = s * jnp.exp(g_last) + _dotg(k * jnp.exp(g_last - gc_c), v_new, _TN, hp)
        z = cur_ref[:, W3 + h * GDN_HD:W3 + (h + 1) * GDN_HD]
        o = o * lax.rsqrt(jnp.mean(o * o, axis=-1, keepdims=True) + RMS_EPS) * gnorm_ref[...]
        o = o * (z * jax.nn.sigmoid(z))
        o_ref[:, sl] = o.astype(o_ref.dtype)


def gdn(u_a, u_bd, state8, s0, w_conv, a_log, dt_bias, gdn_norm, batch, seq, chunk, n_valid, hp):
    nc = seq // chunk
    rb = chunk // SUBLANES
    alog = jnp.zeros((1, LANES), F32).at[0, GDN_HEADS:2 * GDN_HEADS].set(a_log.astype(F32))
    dtb = jnp.zeros((1, LANES), F32).at[0, GDN_HEADS:2 * GDN_HEADS].set(dt_bias.astype(F32))
    W3 = 3 * GDN_WIDTH
    return pl.pallas_call(
        functools.partial(_gdn_body, chunk=chunk, n_valid=n_valid, hp=hp),
        grid=(batch, nc),
        in_specs=[
            pl.BlockSpec((chunk, 4 * GDN_WIDTH), lambda b, c: (b * nc + c, 0)),
            pl.BlockSpec((SUBLANES, 4 * GDN_WIDTH), lambda b, c: (jnp.maximum((b * nc + c) * rb - 1, 0), 0)),
            pl.BlockSpec((1, SUBLANES, W3), lambda b, c: (b, 0, 0)),
            pl.BlockSpec((chunk, LANES), lambda b, c: (b * nc + c, 0)),
            pl.BlockSpec((1, GDN_HEADS, GDN_HD, GDN_HD), lambda b, c: (b, 0, 0, 0)),
            pl.BlockSpec((CONV_W, W3), lambda b, c: (0, 0)),
            pl.BlockSpec((1, LANES), lambda b, c: (0, 0)),
            pl.BlockSpec((1, LANES), lambda b, c: (0, 0)),
            pl.BlockSpec((1, GDN_HD), lambda b, c: (0, 0)),
        ],
        out_specs=[
            pl.BlockSpec((chunk, GDN_WIDTH), lambda b, c: (b * nc + c, 0)),
            pl.BlockSpec((1, GDN_HEADS, GDN_HD, GDN_HD), lambda b, c: (b, 0, 0, 0)),
        ],
        out_shape=[
            jax.ShapeDtypeStruct((batch * seq, GDN_WIDTH), F32 if hp else BF16),
            jax.ShapeDtypeStruct((batch, GDN_HEADS, GDN_HD, GDN_HD), F32),
        ],
        scratch_shapes=[pltpu.VMEM((SUBLANES + chunk, W3), F32)],
        compiler_params=_cparams("parallel", "arbitrary"),
        name="gdn",
    )(u_a, u_a, state8, u_bd, s0, w_conv.astype(F32), alog, dtb, gdn_norm.reshape(1, GDN_HD).astype(F32))


def _flash_body(qi_ref, ki_ref, lam_ref, q_ref, k_ref, vt_ref, dn_ref, o_ref, qs_ref, m_ref, acc_ref, *, tq):
    step = pl.program_id(2)
    qi = qi_ref[step]
    ki = ki_ref[step]

    @pl.when(ki == 0)
    def _():
        q = q_ref[...]
        lane = lax.broadcasted_iota(jnp.int32, q.shape, 1)
        zero = jnp.zeros_like(q)
        qs_ref[0:tq, :] = jnp.where(lane < DIFF_HD, q, zero)
        qs_ref[tq:2 * tq, :] = jnp.where(lane >= DIFF_HD, q, zero)
        m_ref[...] = jnp.full(m_ref.shape, NEG_INIT, F32)
        acc_ref[...] = jnp.zeros(acc_ref.shape, F32)

    def update(masked):
        vt1 = jnp.concatenate([vt_ref[...], jnp.ones((ONES_ROWS, vt_ref.shape[1]), BF16)], axis=0)
        s = lax.dot_general(k_ref[...], qs_ref[...], _NT, preferred_element_type=F32)
        if masked:
            key = lax.broadcasted_iota(jnp.int32, s.shape, 0)
            qry = lax.rem(lax.broadcasted_iota(jnp.int32, s.shape, 1), tq)
            s = jnp.where(key <= qry, s, NEG_INIT)
        m_prev = m_ref[...]
        m_new = jnp.maximum(m_prev, jnp.max(s, axis=0, keepdims=True))
        alpha = jnp.exp2(m_prev - m_new)
        p = jnp.exp2(s - m_new).astype(BF16)
        acc_ref[...] = alpha * acc_ref[...] + jnp.dot(vt1, p, preferred_element_type=F32)
        m_ref[...] = m_new

    @pl.when(ki < qi)
    def _():
        update(False)

    @pl.when(ki == qi)
    def _():
        update(True)
        lam = lam_ref[0]
        inv = 1.0 / acc_ref[DIFF_VD:DIFF_VD + 1, :]
        o = (acc_ref[0:DIFF_VD, 0:tq] * inv[:, 0:tq]
             - lam * (acc_ref[0:DIFF_VD, tq:2 * tq] * inv[:, tq:2 * tq]))
        o = o * lax.rsqrt(jnp.mean(o * o, axis=0, keepdims=True) + RMS_EPS)
        o_ref[...] = (o.T * dn_ref[...] * (1.0 - LAMBDA_INIT)).astype(o_ref.dtype)


def diff_attn_prompt(q, k, vt, lam, diff_norm, batch, seq):
    tq = _tile(seq, FLASH_TILE)
    nq = seq // tq
    pairs =[(i, j) for i in range(nq) for j in range(i + 1)]
    qi_tab = jnp.asarray([p[0] for p in pairs], jnp.int32)
    ki_tab = jnp.asarray([p[1] for p in pairs], jnp.int32)
    grid_spec = pltpu.PrefetchScalarGridSpec(
        num_scalar_prefetch=3,
        grid=(batch, DIFF_HEADS, len(pairs)),
        in_specs=[
            pl.BlockSpec((tq, DIFF_VD), lambda b, h, s, qt, kt, lam: (b * nq + qt[s], h)),
            pl.BlockSpec((tq, DIFF_VD), lambda b, h, s, qt, kt, lam: (b * nq + kt[s], h)),
            pl.BlockSpec((DIFF_VD, tq), lambda b, h, s, qt, kt, lam: (h, b * nq + kt[s])),
            pl.BlockSpec((1, DIFF_VD), lambda b, h, s, qt, kt, lam: (0, 0)),
        ],
        out_specs=pl.BlockSpec((tq, DIFF_VD), lambda b, h, s, qt, kt, lam: (b * nq + qt[s], h)),
        scratch_shapes=[
            pltpu.VMEM((2 * tq, DIFF_VD), BF16),
            pltpu.VMEM((1, 2 * tq), F32),
            pltpu.VMEM((DIFF_VD + ONES_ROWS, 2 * tq), F32),
        ],
    )
    return pl.pallas_call(
        functools.partial(_flash_body, tq=tq),
        grid_spec=grid_spec,
        out_shape=jax.ShapeDtypeStruct((batch * seq, DIFF_WIDTH), BF16),
        compiler_params=_cparams("parallel", "parallel", "arbitrary"),
        name="diff_attn_prompt",
    )(qi_tab, ki_tab, lam.reshape(1), q, k, vt, diff_norm.reshape(1, DIFF_VD).astype(F32))


PAGE_ROWS = PAGE_SIZE * DIFF_HEADS


def _head_mask(shape, tp, extra_mask=None):
    r = lax.broadcasted_iota(jnp.int32, shape, 0)
    c = lax.broadcasted_iota(jnp.int32, shape, 1)
    ok = jnp.bitwise_and(c, DIFF_HEADS - 1) == jnp.right_shift(r, (2 * tp).bit_length() - 1)
    if extra_mask is not None:
        ok = jnp.logical_and(ok, extra_mask(r, c))
    return jnp.where(ok, 0.0, NEG_INIT).astype(F32)


def _paged_body(pt_ref, lam_ref, q_ref, *rest, n_tok, pp, tp):
    kp_refs = rest[0:pp]
    vp_refs = rest[pp:2 * pp]
    kn_ref, vn_ref, dn_ref, o_ref, m_ref, l_ref, acc_ref, bias_ref = rest[2 * pp:]
    p = pl.program_id(1)
    npg = pl.num_programs(1)
    kshift = DIFF_HEADS.bit_length() - 1

    @pl.when(p == 0)
    def _():
        m_ref[...] = jnp.full(m_ref.shape, NEG_INIT, F32)
        l_ref[...] = jnp.zeros(l_ref.shape, F32)
        acc_ref[...] = jnp.zeros(acc_ref.shape, F32)
        bias_ref[...] = _head_mask(bias_ref.shape, tp)

    qh, qm = _split2(q_ref[0])
    qcat = jnp.concatenate([qh, qh, qm], axis=1)

    def scores(kb, bias):
        kh, km = _split2(kb)
        kcat = jnp.concatenate([kh, km, kh], axis=1)
        return lax.dot_general(qcat, kcat, _NT, preferred_element_type=F32) + bias

    def weighted_values(pr, vb):
        ph, pm = _split2(pr)
        vh, vm = _split2(vb)
        wide = jnp.dot(ph, jnp.concatenate([vh, vm], axis=1), preferred_element_type=F32)
        return (wide[:, 0:DIFF_VD] + wide[:, DIFF_VD:2 * DIFF_VD]) + jnp.dot(pm, vh, preferred_element_type=F32)

    def update(ss, vbs):
        m_prev = m_ref[...]
        m_new = m_prev
        for s in ss:
            m_new = jnp.maximum(m_new, jnp.max(s, axis=1, keepdims=True))
        alpha = jnp.exp(m_prev - m_new)
        l_new = alpha * l_ref[...]
        acc = alpha * acc_ref[...]
        for s, vb in zip(ss, vbs):
            pr = jnp.exp(s - m_new)
            l_new = l_new + jnp.sum(pr, axis=1, keepdims=True)
            acc = acc + weighted_values(pr, vb)
        l_ref[...] = l_new
        acc_ref[...] = acc
        m_ref[...] = m_new

    bias = bias_ref[...]
    update([scores(kp_refs[j][0], bias) for j in range(pp)], [vp_refs[j][0] for j in range(pp)])

    @pl.when(p == npg - 1)
    def _():
        def causal(r, c):
            tok = jnp.right_shift(c, kshift)
            return jnp.logical_and(tok <= jnp.bitwise_and(r, tp - 1), tok < n_tok)

        update([scores(kn_ref[0], _head_mask((q_ref.shape[1], kn_ref.shape[1]), tp, causal))], [vn_ref[0]])
        lam = lam_ref[0]
        inv = 1.0 / l_ref[...]
        for h in range(DIFF_HEADS):
            r0 = h * 2 * tp
            if tp == SUBLANES:
                o1 = acc_ref[r0:r0 + tp, :] * inv[r0:r0 + tp]
                o2 = acc_ref[r0 + tp:r0 + 2 * tp, :] * inv[r0 + tp:r0 + 2 * tp]
            else:
                o1 = acc_ref[r0:r0 + SUBLANES, :] * inv[r0:r0 + SUBLANES]
                o2 = pltpu.roll(o1, tp, 0)
            o = o1 - lam * o2
            o = o * lax.rsqrt(jnp.mean(o * o, axis=-1, keepdims=True) + RMS_EPS) * dn_ref[...]
            o_ref[0, h * SUBLANES:(h + 1) * SUBLANES, :] = (o * (1.0 - LAMBDA_INIT)).astype(o_ref.dtype)


def diff_attn_sample(q, k_new, v_new, lam, diff_norm, cache_k, cache_v, page_table, n_tok):
    bd = q.shape[0]
    n_pool = cache_k.shape[0]
    n_pages = page_table.shape[1]
    assert n_tok <= SUBLANES
    tp = SUBLANES // 2 if n_tok <= SUBLANES // 2 else SUBLANES
    qrows_n = 2 * DIFF_HEADS * tp
    pp = PAGES_PER_STEP
    while n_pages % pp:
        pp //= 2
    qh = q.reshape(bd, n_tok, DIFF_HEADS, 2, DIFF_HD)
    qh = jnp.pad(qh, ((0, 0), (0, tp - n_tok), (0, 0), (0, 0), (0, 0)))
    qh = jnp.transpose(qh, (0, 2, 3, 1, 4))
    zeros = jnp.zeros_like(qh[:, :, 0])
    qrows = jnp.stack([jnp.concatenate([qh[:, :, 0], zeros], axis=-1),
                       jnp.concatenate([zeros, qh[:, :, 1]], axis=-1)], axis=2).reshape(bd, qrows_n, DIFF_VD)
    new_rows = tp * DIFF_HEADS
    out_rows = SUBLANES * DIFF_HEADS
    padn = ((0, 0), (0, new_rows - n_tok * DIFF_HEADS), (0, 0))
    kn = jnp.pad(k_new.reshape(bd, n_tok * DIFF_HEADS, DIFF_VD), padn)
    vn = jnp.pad(v_new.reshape(bd, n_tok * DIFF_HEADS, DIFF_VD), padn)
    ck = cache_k.reshape(n_pool, PAGE_ROWS, DIFF_VD)
    cv = cache_v.reshape(n_pool, PAGE_ROWS, DIFF_VD)

    def page_spec(j):
        return pl.BlockSpec((1, PAGE_ROWS, DIFF_VD), lambda b, p, pt, lam: (pt[b, p * pp + j], 0, 0))

    grid_spec = pltpu.PrefetchScalarGridSpec(
        num_scalar_prefetch=2,
        grid=(bd, n_pages // pp),
        in_specs=(
            [pl.BlockSpec((1, qrows_n, DIFF_VD), lambda b, p, pt, lam: (b, 0, 0))]
            + [page_spec(j) for j in range(pp)]
            + [page_spec(j) for j in range(pp)]
            + [pl.BlockSpec((1, new_rows, DIFF_VD), lambda b, p, pt, lam: (b, 0, 0)),
               pl.BlockSpec((1, new_rows, DIFF_VD), lambda b, p, pt, lam: (b, 0, 0)),
               pl.BlockSpec((1, DIFF_VD), lambda b, p, pt, lam: (0, 0))]
        ),
        out_specs=pl.BlockSpec((1, out_rows, DIFF_VD), lambda b, p, pt, lam: (b, 0, 0)),
        scratch_shapes=[
            pltpu.VMEM((qrows_n, 1), F32),
            pltpu.VMEM((qrows_n, 1), F32),
            pltpu.VMEM((qrows_n, DIFF_VD), F32),
            pltpu.VMEM((qrows_n, PAGE_ROWS), F32),
        ],
    )
    out = pl.pallas_call(
        functools.partial(_paged_body, n_tok=n_tok, pp=pp, tp=tp),
        grid_spec=grid_spec,
        out_shape=jax.ShapeDtypeStruct((bd, out_rows, DIFF_VD), F32),
        compiler_params=_cparams("parallel", "arbitrary"),
        cost_estimate=pl.CostEstimate(
            flops=2 * 5 * bd * n_pages * qrows_n * PAGE_ROWS * DIFF_VD,
            transcendentals=bd * n_pages * qrows_n * PAGE_ROWS,
            bytes_accessed=2 * bd * n_pages * PAGE_ROWS * DIFF_VD * 4),
        name="diff_attn_sample",
    )(page_table, lam.reshape(1), qrows, *([ck] * pp), *([cv] * pp), kn, vn,
      diff_norm.reshape(1, DIFF_VD).astype(F32))
    out = out.reshape(bd, DIFF_HEADS, SUBLANES, DIFF_VD)[:, :, :n_tok]
    return jnp.transpose(out, (0, 2, 1, 3)).reshape(bd, n_tok, DIFF_WIDTH)


def _mem_body(q_ref, k_ref, v_ref, o_ref):
    hp = q_ref.dtype == F32
    for h in range(MEM_HEADS):
        sl = slice(h * MEM_HD, (h + 1) * MEM_HD)
        s = _dotg(q_ref[0, :, sl], k_ref[0, :, sl], _NT, hp) * (MEM_HD ** -0.5)
        s = s - jnp.max(s, axis=1, keepdims=True)
        e = jnp.exp(s)
        p = e / jnp.sum(e, axis=1, keepdims=True)
        o_ref[0, :, sl] = _dotg(p, v_ref[0, :, sl], _NN, hp).astype(o_ref.dtype)


def mem_attn(q, mem_k, mem_v):
    b, t, _ = q.shape
    mlen = mem_k.shape[1]
    tq = _tile(t, 512)
    return pl.pallas_call(
        _mem_body,
        grid=(b, t // tq),
        in_specs=[
            pl.BlockSpec((1, tq, MEM_WIDTH), lambda i, j: (i, j, 0)),
            pl.BlockSpec((1, mlen, MEM_WIDTH), lambda i, j: (i, 0, 0)),
            pl.BlockSpec((1, mlen, MEM_WIDTH), lambda i, j: (i, 0, 0)),
        ],
        out_specs=pl.BlockSpec((1, tq, MEM_WIDTH), lambda i, j: (i, j, 0)),
        out_shape=jax.ShapeDtypeStruct((b, t, MEM_WIDTH), q.dtype),
        compiler_params=_cparams("parallel", "arbitrary"),
        name="mem_attn",
    )(q, mem_k, mem_v)


def _norm_router_body(x_ref, g_ref, wr_ref, br_ref, h_ref, rt_ref):
    x = x_ref[...]
    hn = x * lax.rsqrt(jnp.mean(x * x, axis=-1, keepdims=True) + RMS_EPS) * g_ref[...]
    h_ref[...] = hn.astype(h_ref.dtype)
    lg = _dot3(hn, wr_ref[...]) + br_ref[...]
    lane = lax.broadcasted_iota(jnp.int32, lg.shape, 1)
    lanef = lane.astype(F32)
    big = float(LANES)

    def first_argmax(vals, mask):
        top = jnp.max(jnp.where(mask, vals, NEG_INIT), axis=1, keepdims=True)
        idx = jnp.min(jnp.where(jnp.logical_and(mask, vals == top), lanef, big), axis=1, keepdims=True)
        return top, idx

    gmask = lane < N_GROUPS
    gtop, gidx = first_argmax(lg, gmask)
    g_w = 1.0 / jnp.sum(jnp.where(gmask, jnp.exp(lg - gtop), 0.0), axis=1, keepdims=True)
    first = N_GROUPS + EXPERTS_PER_GROUP * gidx.astype(jnp.int32)
    emask = jnp.logical_and(lane >= first, lane < first + EXPERTS_PER_GROUP)
    e1, i1 = first_argmax(lg, emask)
    e2, i2 = first_argmax(lg, jnp.logical_and(emask, lanef != i1))
    ez = jnp.sum(jnp.where(emask, jnp.exp(lg - e1), 0.0), axis=1, keepdims=True)
    p1 = 1.0 / ez
    p2 = jnp.exp(e2 - e1) / ez
    psum = p1 + p2
    rt_ref[...] = jnp.where(lane == 0, i1 - N_GROUPS,
                            jnp.where(lane == 1, i2 - N_GROUPS,
                                      jnp.where(lane == 2, g_w * (p1 / psum),
                                                jnp.where(lane == 3, g_w * (p2 / psum), 0.0))))


def norm_router(x, g, w_router, b_router):
    m, d = x.shape
    tm = _tile(m, 512)
    return pl.pallas_call(
        _norm_router_body,
        grid=(m // tm,),
        in_specs=[
            pl.BlockSpec((tm, d), lambda i: (i, 0)),
            pl.BlockSpec((1, d), lambda i: (0, 0)),
            pl.BlockSpec((d, LANES), lambda i: (0, 0)),
            pl.BlockSpec((1, LANES), lambda i: (0, 0)),
        ],
        out_specs=[pl.BlockSpec((tm, d), lambda i: (i, 0)), pl.BlockSpec((tm, LANES), lambda i: (i, 0))],
        out_shape=[jax.ShapeDtypeStruct((m, d), F32), jax.ShapeDtypeStruct((m, LANES), F32)],
        compiler_params=_cparams("parallel"),
        name="norm_router",
    )(x, g.reshape(1, d).astype(F32), w_router, b_router)


def _expert_body(te_ref, tv_ref, after_ref, x_ref, wg_ref, wu_ref, wd_ref, o_ref, wg16, wu16, wd16):
    del after_ref
    i = pl.program_id(0)
    new_expert = jnp.logical_or(i == 0, te_ref[i] != te_ref[jnp.maximum(i - 1, 0)])

    @pl.when(jnp.logical_and(tv_ref[i] != 0, new_expert))
    def _():
        wg16[...] = wg_ref[0].astype(BF16)
        wu16[...] = wu_ref[0].astype(BF16)
        wd16[...] = wd_ref[0].astype(BF16)

    @pl.when(tv_ref[i] != 0)
    def _():
        x = x_ref[...].astype(BF16)
        a = jnp.dot(x, wg16[...], preferred_element_type=F32)
        b = jnp.dot(x, wu16[...], preferred_element_type=F32)
        hid = (a * jax.nn.sigmoid(a)) * b
        o_ref[...] = jnp.dot(hid.astype(BF16), wd16[...], preferred_element_type=F32)

    @pl.when(tv_ref[i] == 0)
    def _():
        o_ref[...] = jnp.zeros(o_ref.shape, o_ref.dtype)


def grouped_experts(xs, tile_expert, tile_valid, after, w_gate, w_up, w_down, tm):
    rows, d = xs.shape
    nt = rows // tm
    ff = w_gate.shape[2]
    grid_spec = pltpu.PrefetchScalarGridSpec(
        num_scalar_prefetch=3,
        grid=(nt,),
        in_specs=[
            pl.BlockSpec((tm, d), lambda i, te, tv, af: (i, 0)),
            pl.BlockSpec((1, d, ff), lambda i, te, tv, af: (te[i], 0, 0)),
            pl.BlockSpec((1, d, ff), lambda i, te, tv, af: (te[i], 0, 0)),
            pl.BlockSpec((1, ff, d), lambda i, te, tv, af: (te[i], 0, 0)),
        ],
        out_specs=pl.BlockSpec((tm, d), lambda i, te, tv, af: (i, 0)),
        scratch_shapes=[pltpu.VMEM((d, ff), BF16), pltpu.VMEM((d, ff), BF16), pltpu.VMEM((ff, d), BF16)],
    )
    return pl.pallas_call(
        _expert_body,
        grid_spec=grid_spec,
        out_shape=jax.ShapeDtypeStruct((rows, d), F32),
        compiler_params=_cparams("arbitrary"),
        name="grouped_experts",
    )(tile_expert, tile_valid, after, xs, w_gate, w_up, w_down)


def _final_body(x_ref, ya_ref, yb_ref, rt_ref, g_ref, o_ref):
    rt = rt_ref[...]
    x = x_ref[...] + (_col(rt, 2) * ya_ref[...] + _col(rt, 3) * yb_ref[...])
    o_ref[...] = x * lax.rsqrt(jnp.mean(x * x, axis=-1, keepdims=True) + RMS_EPS) * g_ref[...]


def final_norm(x, ya, yb, row0, route, g):
    m, d = x.shape
    tm = _tile(m, 512)
    assert row0 % tm == 0
    blk0 = row0 // tm
    spec = pl.BlockSpec((tm, d), lambda i: (i, 0))
    yspec = pl.BlockSpec((tm, d), lambda i: (i + blk0, 0))
    return pl.pallas_call(
        _final_body,
        grid=(m // tm,),
        in_specs=[spec, yspec, yspec, pl.BlockSpec((tm, LANES), lambda i: (i, 0)),
                  pl.BlockSpec((1, d), lambda i: (0, 0))],
        out_specs=spec,
        out_shape=jax.ShapeDtypeStruct((m, d), F32),
        compiler_params=_cparams("parallel"),
        name="final_norm",
    )(x, ya, yb, route, g.reshape(1, d).astype(F32))


MOE_TILE = 256


def moe_and_final(x_list, norm_ffn, w_rg, b_rg, w_re, b_re, w_gate, w_up, w_down, norm_final, after=None):
    if after is None:
        after = jnp.zeros((1,), jnp.int32)
    else:
        after = (after.reshape(-1)[0:1] > jnp.finfo(F32).max).astype(jnp.int32)
    d = x_list[0].shape[1]
    w_router = jnp.zeros((d, LANES), F32).at[:, 0:N_GROUPS].set(w_rg.astype(F32))
    w_router = w_router.at[:, N_GROUPS:N_GROUPS + N_EXPERTS].set(w_re.astype(F32))
    b_router = jnp.zeros((1, LANES), F32).at[0, 0:N_GROUPS].set(b_rg.astype(F32))
    b_router = b_router.at[0, N_GROUPS:N_GROUPS + N_EXPERTS].set(b_re.astype(F32))
    hs, rts = [], []
    for x in x_list:
        hn, rt = norm_router(x, norm_ffn, w_router, b_router)
        hs.append(hn)
        rts.append(rt)
    hn = jnp.concatenate(hs, axis=0)
    n = hn.shape[0]
    expert_id = jnp.concatenate([rt[:, 0:TOP_K_INNER] for rt in rts], axis=0).astype(jnp.int32)
    tm = min(MOE_TILE, max(SUBLANES * 2, n * TOP_K_INNER // N_EXPERTS * 4))
    flat_e = expert_id.reshape(-1)
    na = flat_e.shape[0]
    onehot = (flat_e[:, None] == jnp.arange(N_EXPERTS, dtype=jnp.int32)[None, :]).astype(jnp.int32)
    csum = jnp.cumsum(onehot, axis=0)
    counts = csum[-1]
    padded = ((counts + tm - 1) // tm) * tm
    pad_end = jnp.cumsum(padded)
    pad_start = pad_end - padded
    dest = jnp.sum(onehot * (csum - 1 + pad_start[None, :]), axis=1)
    nt = (na + tm - 1) // tm + N_EXPERTS
    rows = nt * tm
    row_tok = jnp.zeros((rows,), jnp.int32).at[dest].set(jnp.arange(na, dtype=jnp.int32) // TOP_K_INNER)
    tile_start = jnp.arange(nt, dtype=jnp.int32) * tm
    tile_expert = jnp.minimum(jnp.sum((pad_end[None, :] <= tile_start[:, None]).astype(jnp.int32), axis=1),
                              N_EXPERTS - 1)
    tile_valid = (tile_start < pad_end[-1]).astype(jnp.int32)
    xs = hn.at[row_tok].get(mode='promise_in_bounds')
    ys = grouped_experts(xs, tile_expert, tile_valid, after, w_gate, w_up, w_down, tm)
    dest2 = dest.reshape(n, TOP_K_INNER)
    ya = ys.at[dest2[:, 0]].get(mode='promise_in_bounds')
    yb = ys.at[dest2[:, 1]].get(mode='promise_in_bounds')
    outs, o = [], 0
    for x, rt in zip(x_list, rts):
        outs.append(final_norm(x, ya, yb, o, rt, norm_final))
        o += x.shape[0]
    return outs


def _rope_tables(pos, rows):
    half = ROT_DIM // 2
    inv_freq = ROPE_THETA ** (-jnp.arange(0, ROT_DIM, 2, dtype=F32) / ROT_DIM)
    ang = pos.astype(F32)[:, None] * inv_freq[None, :]
    cos, sin = jnp.cos(ang), jnp.sin(ang)
    t = pos.shape[0]
    one = jnp.ones((t, DIFF_HD - ROT_DIM), F32)
    zero = jnp.zeros((t, DIFF_HD - ROT_DIM), F32)
    zh = jnp.zeros((t, half), F32)
    c = jnp.concatenate([cos, cos, one], axis=1)
    s1 = jnp.concatenate([-sin, zh, zero], axis=1)
    s2 = jnp.concatenate([zh, sin, zero], axis=1)
    rep = rows // t
    tile = lambda a: jnp.tile(jnp.concatenate([a, a], axis=1), (rep, 1))
    return tile(c), tile(s1), tile(s2)


def _prep_weights(p, dtype):
    w_in = p['w_in']
    o = 4 * GDN_WIDTH
    ob = o + 2 * GDN_HEADS
    d = w_in.shape[0]
    w = {}
    w['a'] = w_in[:, 0:o].astype(dtype)
    w['bd'] = jnp.zeros((d, LANES), dtype).at[:, 0:2 * GDN_HEADS].set(w_in[:, o:ob].astype(dtype))
    w['dq'] = w_in[:, ob:ob + DIFF_WIDTH].astype(dtype)
    w['dk'] = w_in[:, ob + DIFF_WIDTH:ob + 2 * DIFF_WIDTH].astype(dtype)
    w['dv'] = w_in[:, ob + 2 * DIFF_WIDTH:ob + 3 * DIFF_WIDTH].astype(dtype)
    w['g'] = w_in[:, ob + 3 * DIFF_WIDTH:].astype(dtype)
    for name in ('w_branch_a', 'w_branch_b', 'w_out', 'w_mq', 'w_mk', 'w_mv', 'w_mo'):
        w[name] = p[name].astype(dtype)
    return w


def _mixers(x, batch, seq, pos, conv_state, delta_state, mem_k, mem_v, p, w, lam, sample_ctx):
    m, d = x.shape
    hp = sample_ctx is not None
    od = F32 if hp else BF16
    h = rmsnorm_rows(x, p['norm_mix'], out_dtype=od)
    u_a = matmul(h, w['a'])
    u_bd = matmul(h, w['bd'])
    u_g = matmul(h, w['g'])
    tm = _tile(m, 1024)
    tables = _rope_tables(pos, max(tm, seq))
    d_v = matmul(h, w['dv'])
    d_k = matmul(h, w['dk'], rope=tables)
    chunk = min(GDN_CHUNK, seq)
    seq_pad = seq
    if chunk % SUBLANES:
        chunk = SUBLANES
        seq_pad = SUBLANES
        padr = lambda a: jnp.pad(a.reshape(batch, seq, -1), ((0, 0), (0, seq_pad - seq), (0, 0))).reshape(
            batch * seq_pad, -1)
        u_a_g, u_bd_g = padr(u_a), padr(u_bd)
    else:
        u_a_g, u_bd_g = u_a, u_bd
    state8 = jnp.pad(conv_state.astype(F32), ((0, 0), (SUBLANES - (CONV_W - 1), 0), (0, 0)))
    o_a, new_delta = gdn(u_a_g, u_bd_g, state8, delta_state.astype(F32), p['w_conv'], p['a_log'], p['dt_bias'],
                         p['gdn_norm'], batch, seq_pad, chunk, min(seq, chunk), hp)
    if seq_pad != seq:
        o_a = o_a.reshape(batch, seq_pad, GDN_WIDTH)[:, :seq].reshape(m, GDN_WIDTH)
    keep = min(seq, CONV_W - 1)
    tail = u_a.reshape(batch, seq, -1)[:, seq - keep:, 0:3 * GDN_WIDTH]
    new_conv = jnp.concatenate([conv_state.astype(F32)[:, keep:], tail], axis=1)
    if sample_ctx is None:
        d_q = matmul(h, w['dq'], out_dtype=BF16, rope=tables, scale=DIFF_HD ** -0.5 * LOG2E)
        o_b = diff_attn_prompt(d_q, d_k.astype(BF16), d_v.astype(BF16).T, lam, p['diff_norm'], batch, seq)
    else:
        cache_k, cache_v, page_table = sample_ctx
        d_q = matmul(h, w['dq'], rope=tables, scale=DIFF_HD ** -0.5)
        o_b = diff_attn_sample(d_q.reshape(batch, seq, -1), d_k.reshape(batch, seq, -1), d_v.reshape(batch, seq, -1),
                               lam, p['diff_norm'], cache_k, cache_v, page_table, seq)
        o_b = o_b[:, :seq].reshape(m, DIFF_WIDTH)
    mixed = merge_branches(o_a, o_b, w['w_branch_a'], w['w_branch_b'], u_g)
    x = matmul(mixed, w['w_out'], residual=x)
    hc = rmsnorm_rows(x, p['norm_cross'], out_dtype=od)
    mq = matmul(hc, w['w_mq'], out_dtype=od).reshape(batch, seq, MEM_WIDTH)
    if seq % SUBLANES:
        mq = jnp.pad(mq, ((0, 0), (0, SUBLANES - seq), (0, 0)))
    mo = mem_attn(mq, mem_k, mem_v)[:, :seq].reshape(m, MEM_WIDTH)
    x = matmul(mo, w['w_mo'], residual=x)
    return x, new_conv, new_delta, d_k, d_v, o_b


def kernel(x_prompt, x_sample, cache_k, cache_v, cache_mem_k, cache_mem_v, state_delta, state_conv, page_table, mem_prompt, norm_mix, w_in, w_conv, a_log, dt_bias, gdn_norm, lambda_q1, lambda_k1, lambda_q2, lambda_k2, diff_norm, w_branch_a, w_branch_b, w_out, norm_cross, norm_mem, w_mq, w_mk, w_mv, w_mo, norm_ffn, w_router_group, b_router_group, w_router_expert, b_router_expert, w_gate, w_up, w_down, norm_final):
    p = dict(norm_mix=norm_mix, w_in=w_in, w_conv=w_conv, a_log=a_log, dt_bias=dt_bias, gdn_norm=gdn_norm,
             diff_norm=diff_norm, w_branch_a=w_branch_a, w_branch_b=w_branch_b, w_out=w_out,
             norm_cross=norm_cross, w_mq=w_mq, w_mk=w_mk, w_mv=w_mv, w_mo=w_mo)
    bp, tp, d = x_prompt.shape
    bs, ts, _ = x_sample.shape
    past_len = page_table.shape[1] * PAGE_SIZE
    w = _prep_weights(p, BF16)
    w_hp = _prep_weights(p, F32)
    lam =(jnp.exp(jnp.sum(lambda_q1.astype(F32) * lambda_k1.astype(F32)))
           - jnp.exp(jnp.sum(lambda_q2.astype(F32) * lambda_k2.astype(F32))) + LAMBDA_INIT)
    mlen = mem_prompt.shape[1]
    hm = rmsnorm_rows(mem_prompt.reshape(bp * mlen, d), norm_mem)
    mem_k_p = matmul(hm, w['w_mk'])
    mem_v_p = matmul(hm, w['w_mv'])
    conv0 = jnp.zeros((bp, CONV_W - 1, 3 * GDN_WIDTH), F32)
    delta0 = jnp.zeros((bp, GDN_HEADS, GDN_HD, GDN_HD), F32)
    xp, conv_p, delta_p, k_p, v_p, _ = _mixers(
        x_prompt.reshape(bp * tp, d), bp, tp, jnp.arange(tp), conv0, delta0,
        mem_k_p.astype(BF16).reshape(bp, mlen, MEM_WIDTH), mem_v_p.astype(BF16).reshape(bp, mlen, MEM_WIDTH),
        p, w, lam, None)
    xs, conv_s, delta_s, k_s, v_s, attn_s = _mixers(
        x_sample.reshape(bs * ts, d), bs, ts, past_len + jnp.arange(ts), state_conv, state_delta,
        cache_mem_k.astype(F32).reshape(bs, -1, MEM_WIDTH), cache_mem_v.astype(F32).reshape(bs, -1, MEM_WIDTH),
        p, w_hp, lam, (cache_k, cache_v, page_table))
    moe_w = (norm_ffn, w_router_group, b_router_group, w_router_expert, b_router_expert, w_gate, w_up, w_down,
             norm_final)
    yp, = moe_and_final([xp], *moe_w, after=attn_s)
    ys, = moe_and_final([xs], *moe_w)
    return (yp.reshape(bp, tp, d), ys.reshape(bs, ts, d),
            k_p.reshape(bp, tp, DIFF_HEADS, DIFF_VD), v_p.reshape(bp, tp, DIFF_HEADS, DIFF_VD),
            mem_k_p.reshape(bp, mlen, MEM_HEADS, MEM_HD), mem_v_p.reshape(bp, mlen, MEM_HEADS, MEM_HD),
            delta_p.astype(state_delta.dtype), conv_p.astype(x_prompt.dtype),
            k_s.reshape(bs, ts, DIFF_HEADS, DIFF_VD), v_s.reshape(bs, ts, DIFF_HEADS, DIFF_VD),
            delta_s.astype(state_delta.dtype), conv_s.astype(state_conv.dtype))
```

```python
import functools
import math

import jax
import jax.numpy as jnp
from jax import lax
from jax.experimental import pallas as pl
from jax.experimental.pallas import tpu as pltpu

F32 = jnp.float32
BF16 = jnp.bfloat16

GDN_HEADS = 8
GDN_HD = 128
GDN_WIDTH = GDN_HEADS * GDN_HD
CONV_W = 4
GDN_CHUNK = 128
DIFF_HEADS = 8
DIFF_HD = 64
DIFF_VD = 2 * DIFF_HD
DIFF_WIDTH = DIFF_HEADS * DIFF_VD
ROT_DIM = DIFF_HD // 4
ROPE_THETA = 500000.0
LAMBDA_INIT = 0.2
PAGE_SIZE = 128
MEM_HEADS = 4
MEM_HD = 128
MEM_WIDTH = MEM_HEADS * MEM_HD
N_GROUPS = 4
EXPERTS_PER_GROUP = 8
N_EXPERTS = N_GROUPS * EXPERTS_PER_GROUP
TOP_K_INNER = 2
RMS_EPS = 1e-6
L2_EPS = 1e-6
NEG_INIT = -1e30

LANES = 128
SUBLANES = 8
VMEM_LIMIT = 48 * 1024 * 1024
FLASH_TILE = 1024
ONES_ROWS = 16
PAGES_PER_STEP = 16
LOG2E = math.log2(math.e)

def _cparams(*sem):
    return pltpu.CompilerParams(dimension_semantics=sem, vmem_limit_bytes=VMEM_LIMIT)


def _tile(n, pref):
    if n <= pref:
        return n
    t = pref
    while n % t:
        t //= 2
    return t


_NN = (((1,), (0,)), ((), ()))
_NT = (((1,), (1,)), ((), ()))
_TN = (((0,), (0,)), ((), ()))


def _dotg(a, b, dims=_NN, hp=False):
    d = lambda x, y: lax.dot_general(x, y, dims, preferred_element_type=F32)
    if not hp:
        return d(a.astype(BF16), b.astype(BF16))
    (ah, am), (bh, bm) = _split2(a), _split2(b)
    return d(ah, bh) + (d(ah, bm) + d(am, bh))


def _mm(a, b):
    return _dotg(a, b, _NN, hp=(a.dtype == F32 and b.dtype == F32))


def _split3(x):
    hi = x.astype(BF16)
    r = x - hi.astype(F32)
    mid = r.astype(BF16)
    lo = (r - mid.astype(F32)).astype(BF16)
    return hi, mid, lo


def _split2(x):
    hi = x.astype(BF16)
    return hi, (x - hi.astype(F32)).astype(BF16)


def _dot3s(a2, b2):
    (ah, am), (bh, bm) = a2, b2
    d = lambda x, y: jnp.dot(x, y, preferred_element_type=F32)
    return d(ah, bh) + (d(ah, bm) + d(am, bh))


def _dot3(a, b):
    return _dot3s(_split2(a), _split2(b))


def _rmsnorm_body(x_ref, g_ref, o_ref):
    x = x_ref[...]
    ms = jnp.mean(x * x, axis=-1, keepdims=True)
    o_ref[...] = (x * lax.rsqrt(ms + RMS_EPS) * g_ref[...]).astype(o_ref.dtype)


def rmsnorm_rows(x, g, out_dtype=BF16):
    m, d = x.shape
    tm = _tile(m, 512)
    return pl.pallas_call(
        _rmsnorm_body,
        grid=(m // tm,),
        in_specs=[pl.BlockSpec((tm, d), lambda i: (i, 0)), pl.BlockSpec((1, d), lambda i: (0, 0))],
        out_specs=pl.BlockSpec((tm, d), lambda i: (i, 0)),
        out_shape=jax.ShapeDtypeStruct((m, d), out_dtype),
        compiler_params=_cparams("parallel"),
        name="rmsnorm_rows",
    )(x, g.reshape(1, d).astype(F32))


def _mm_body(a_ref, w_ref, o_ref):
    o_ref[...] = _mm(a_ref[...], w_ref[...]).astype(o_ref.dtype)


def _mm_res_body(a_ref, w_ref, r_ref, o_ref):
    o_ref[...] = r_ref[...] + _mm(a_ref[...], w_ref[...])


def _mm_rope_body(a_ref, w_ref, c_ref, s1_ref, s2_ref, o_ref, *, scale, reps):
    o = _mm(a_ref[...], w_ref[...])
    tn = o.shape[1]
    c = jnp.tile(c_ref[...], (1, reps))
    s1 = jnp.tile(s1_ref[...], (1, reps))
    s2 = jnp.tile(s2_ref[...], (1, reps))
    r = o * c + pltpu.roll(o, tn - ROT_DIM // 2, 1) * s1 + pltpu.roll(o, ROT_DIM // 2, 1) * s2
    if scale != 1.0:
        r = r * scale
    o_ref[...] = r.astype(o_ref.dtype)


def matmul(a, w, out_dtype=F32, residual=None, rope=None, scale=1.0):
    m, k = a.shape
    n = w.shape[1]
    tm = _tile(m, 1024)
    tn = _tile(n, 1024 if a.dtype == BF16 else 512)
    in_specs = [pl.BlockSpec((tm, k), lambda i, j: (i, 0)), pl.BlockSpec((k, tn), lambda i, j: (0, j))]
    args = [a, w]
    if residual is not None:
        body = _mm_res_body
        in_specs.append(pl.BlockSpec((tm, tn), lambda i, j: (i, j)))
        args.append(residual)
    elif rope is not None:
        c, s1, s2 = rope
        nt = c.shape[0] // tm
        body = functools.partial(_mm_rope_body, scale=scale, reps=tn // LANES)
        tspec = pl.BlockSpec((tm, LANES), lambda i, j: (i % nt, 0))
        in_specs += [tspec, tspec, tspec]
        args += [c, s1, s2]
    else:
        body = _mm_body
    return pl.pallas_call(
        body,
        grid=(m // tm, n // tn),
        in_specs=in_specs,
        out_specs=pl.BlockSpec((tm, tn), lambda i, j: (i, j)),
        out_shape=jax.ShapeDtypeStruct((m, n), out_dtype),
        compiler_params=_cparams("parallel", "arbitrary"),
        name="matmul",
    )(*args)


def _merge_body(oa_ref, ob_ref, wa_ref, wb_ref, ga_ref, gb_ref, o_ref):
    br_a = _mm(oa_ref[...], wa_ref[...])
    br_b = _mm(ob_ref[...], wb_ref[...])
    o_ref[...] = (jax.nn.sigmoid(ga_ref[...]) * br_a + jax.nn.sigmoid(gb_ref[...]) * br_b).astype(o_ref.dtype)


def merge_branches(oa, ob, wa, wb, gates):
    m, ka = oa.shape
    n = wa.shape[1]
    tm = _tile(m, 1024)
    tn = _tile(n, 512)
    nb = n // tn
    return pl.pallas_call(
        _merge_body,
        grid=(m // tm, nb),
        in_specs=[
            pl.BlockSpec((tm, ka), lambda i, j: (i, 0)),
            pl.BlockSpec((tm, ob.shape[1]), lambda i, j: (i, 0)),
            pl.BlockSpec((ka, tn), lambda i, j: (0, j)),
            pl.BlockSpec((wb.shape[0], tn), lambda i, j: (0, j)),
            pl.BlockSpec((tm, tn), lambda i, j: (i, j)),
            pl.BlockSpec((tm, tn), lambda i, j: (i, j + nb)),
        ],
        out_specs=pl.BlockSpec((tm, tn), lambda i, j: (i, j)),
        out_shape=jax.ShapeDtypeStruct((m, n), oa.dtype),
        compiler_params=_cparams("parallel", "arbitrary"),
        name="merge_branches",
    )(oa, ob, wa, wb, gates, gates)


def _col(arr, idx):
    lane = lax.broadcasted_iota(jnp.int32, arr.shape, 1)
    return jnp.sum(jnp.where(lane == idx, arr, 0.0), axis=1, keepdims=True)


def _row(arr, idx):
    sub = lax.broadcasted_iota(jnp.int32, arr.shape, 0)
    return jnp.sum(jnp.where(sub == idx, arr, 0.0), axis=0, keepdims=True)


def _gdn_body(cur_ref, prev_ref, st8_ref, bd_ref, s0_ref, wconv_ref, alog_ref, dtb_ref, gnorm_ref,
              o_ref, s_ref, ext_ref, *, chunk, n_valid, hp):
    c = pl.program_id(1)
    C = chunk
    H = GDN_HEADS
    W3 = 3 * GDN_WIDTH

    @pl.when(c == 0)
    def _():
        s_ref[...] = s0_ref[...]
        ext_ref[0:SUBLANES, :] = st8_ref[0]

    @pl.when(c != 0)
    def _():
        ext_ref[0:SUBLANES, :] = prev_ref[:, 0:W3]

    ext_ref[SUBLANES:SUBLANES + C, :] = cur_ref[:, 0:W3]
    base = SUBLANES - (CONV_W - 1)
    acc = ext_ref[base:base + C, :] * wconv_ref[0:1, :]
    for i in range(1, CONV_W):
        acc = acc + ext_ref[base + i:base + i + C, :] * wconv_ref[i:i + 1, :]
    qkv = acc * jax.nn.sigmoid(acc)

    bd = bd_ref[...]
    lane = lax.broadcasted_iota(jnp.int32, bd.shape, 1)
    beta_all = jax.nn.sigmoid(bd)
    xg = bd + dtb_ref[...]
    softplus = jnp.maximum(xg, 0.0) + jnp.log1p(jnp.exp(-jnp.abs(xg)))
    g_all = -jnp.exp(alog_ref[...]) * softplus
    gb = jnp.where(lane < H, beta_all, jnp.where(lane < 2 * H, g_all, 0.0))
    if n_valid < C:
        gb = jnp.where(lax.broadcasted_iota(jnp.int32, bd.shape, 0) < n_valid, gb, 0.0)
    ri = lax.broadcasted_iota(jnp.int32, (C, C), 0)
    ci = lax.broadcasted_iota(jnp.int32, (C, C), 1)
    causal = ri >= ci
    strict = ri > ci
    ltri = jnp.where(causal, 1.0, 0.0).astype(BF16)
    gc_cols = sum(jnp.dot(ltri, piece, preferred_element_type=F32) for piece in _split3(gb))
    gc_rows = gc_cols.T
    eye = jnp.where(ri == ci, 1.0, 0.0).astype(F32)

    heads = []
    for h in range(H):
        q = qkv[:, h * GDN_HD:(h + 1) * GDN_HD]
        k = qkv[:, GDN_WIDTH + h * GDN_HD:GDN_WIDTH + (h + 1) * GDN_HD]
        v = qkv[:, 2 * GDN_WIDTH + h * GDN_HD:2 * GDN_WIDTH + (h + 1) * GDN_HD]
        q = q * lax.rsqrt(jnp.sum(q * q, axis=-1, keepdims=True) + L2_EPS) * (GDN_HD ** -0.5)
        k = k * lax.rsqrt(jnp.sum(k * k, axis=-1, keepdims=True) + L2_EPS)
        beta_c = _col(gb, h)
        gc_c = _col(gc_cols, H + h)
        gc_r = _row(gc_rows, H + h)
        g_last = gc_c[C - 1:C, :]
        decay = jnp.where(causal, jnp.exp(jnp.where(causal, gc_c - gc_r, 0.0)), 0.0)
        kb = k * beta_c
        kk = _dotg(jnp.concatenate([kb, q], axis=0), k, _NT, hp)
        a_kk = jnp.where(strict, kk[0:C] * decay, 0.0)
        a_qk = jnp.where(causal, kk[C:2 * C] * decay, 0.0)
        rhs = jnp.concatenate([v * beta_c, kb * jnp.exp(gc_c)], axis=1)
        heads.append(dict(q=q, k=k, gc_c=gc_c, g_last=g_last, a_qk=a_qk, rhs=rhs, nmat=-a_kk, tinv=eye - a_kk))
    for hd in heads:
        hd['ns'] = _split2(hd['nmat'])
    span = 2
    while span < C:
        for hd in heads:
            hd['ns'] = _split2(_dot3s(hd['ns'], hd['ns']))
        for hd in heads:
            hd['tinv'] = hd['tinv'] + _dot3s(_split2(hd['tinv']), hd['ns'])
        span *= 2
    for hd in heads:
        hd['sol'] = _dot3s(_split2(hd['tinv']), _split2(hd['rhs']))
    for h, hd in enumerate(heads):
        sl = slice(h * GDN_HD, (h + 1) * GDN_HD)
        q, k, gc_c, g_last = hd['q'], hd['k'], hd['gc_c'], hd['g_last']
        u = hd['sol'][:, 0:GDN_HD]
        w = hd['sol'][:, GDN_HD:2 * GDN_HD]
        s = s_ref[0, h]
        ws = _dotg(jnp.concatenate([w, q * jnp.exp(gc_c)], axis=0), s, _NN, hp)
        v_new = u - ws[0:C]
        o = ws[C:2 * C] + _dotg(hd['a_qk'], v_new, _NN, hp)
        s_ref[0, h] = s * jnp.exp(g_last) + _dotg(k * jnp.exp(g_last - gc_c), v_new, _TN, hp)
        z = cur_ref[:, W3 + h * GDN_HD:W3 + (h + 1) * GDN_HD]
        o = o * lax.rsqrt(jnp.mean(o * o, axis=-1, keepdims=True) + RMS_EPS) * gnorm_ref[...]
        o = o * (z * jax.nn.sigmoid(z))
        o_ref[:, sl] = o.astype(o_ref.dtype)


def gdn(u_a, u_bd, state8, s0, w_conv, a_log, dt_bias, gdn_norm, batch, seq, chunk, n_valid, hp):
    nc = seq // chunk
    rb = chunk // SUBLANES
    alog = jnp.zeros((1, LANES), F32).at[0, GDN_HEADS:2 * GDN_HEADS].set(a_log.astype(F32))
    dtb = jnp.zeros((1, LANES), F32).at[0, GDN_HEADS:2 * GDN_HEADS].set(dt_bias.astype(F32))
    W3 = 3 * GDN_WIDTH
    return pl.pallas_call(
        functools.partial(_gdn_body, chunk=chunk, n_valid=n_valid, hp=hp),
        grid=(batch, nc),
        in_specs=[
            pl.BlockSpec((chunk, 4 * GDN_WIDTH), lambda b, c: (b * nc + c, 0)),
            pl.BlockSpec((SUBLANES, 4 * GDN_WIDTH), lambda b, c: (jnp.maximum((b * nc + c) * rb - 1, 0), 0)),
            pl.BlockSpec((1, SUBLANES, W3), lambda b, c: (b, 0, 0)),
            pl.BlockSpec((chunk, LANES), lambda b, c: (b * nc + c, 0)),
            pl.BlockSpec((1, GDN_HEADS, GDN_HD, GDN_HD), lambda b, c: (b, 0, 0, 0)),
            pl.BlockSpec((CONV_W, W3), lambda b, c: (0, 0)),
            pl.BlockSpec((1, LANES), lambda b, c: (0, 0)),
            pl.BlockSpec((1, LANES), lambda b, c: (0, 0)),
            pl.BlockSpec((1, GDN_HD), lambda b, c: (0, 0)),
        ],
        out_specs=[
            pl.BlockSpec((chunk, GDN_WIDTH), lambda b, c: (b * nc + c, 0)),
            pl.BlockSpec((1, GDN_HEADS, GDN_HD, GDN_HD), lambda b, c: (b, 0, 0, 0)),
        ],
        out_shape=[
            jax.ShapeDtypeStruct((batch * seq, GDN_WIDTH), F32 if hp else BF16),
            jax.ShapeDtypeStruct((batch, GDN_HEADS, GDN_HD, GDN_HD), F32),
        ],
        scratch_shapes=[pltpu.VMEM((SUBLANES + chunk, W3), F32)],
        compiler_params=_cparams("parallel", "arbitrary"),
        name="gdn",
    )(u_a, u_a, state8, u_bd, s0, w_conv.astype(F32), alog, dtb, gdn_norm.reshape(1, GDN_HD).astype(F32))


def _flash_body(qi_ref, ki_ref, lam_ref, q_ref, k_ref, vt_ref, dn_ref, o_ref, qs_ref, m_ref, acc_ref, *, tq):
    step = pl.program_id(2)
    qi = qi_ref[step]
    ki = ki_ref[step]

    @pl.when(ki == 0)
    def _():
        q = q_ref[...]
        lane = lax.broadcasted_iota(jnp.int32, q.shape, 1)
        zero = jnp.zeros_like(q)
        qs_ref[0:tq, :] = jnp.where(lane < DIFF_HD, q, zero)
        qs_ref[tq:2 * tq, :] = jnp.where(lane >= DIFF_HD, q, zero)
        m_ref[...] = jnp.full(m_ref.shape, NEG_INIT, F32)
        acc_ref[...] = jnp.zeros(acc_ref.shape, F32)

    def update(masked):
        vt1 = jnp.concatenate([vt_ref[...], jnp.ones((ONES_ROWS, vt_ref.shape[1]), BF16)], axis=0)
        s = lax.dot_general(k_ref[...], qs_ref[...], _NT, preferred_element_type=F32)
        if masked:
            key = lax.broadcasted_iota(jnp.int32, s.shape, 0)
            qry = lax.rem(lax.broadcasted_iota(jnp.int32, s.shape, 1), tq)
            s = jnp.where(key <= qry, s, NEG_INIT)
        m_prev = m_ref[...]
        m_new = jnp.maximum(m_prev, jnp.max(s, axis=0, keepdims=True))
        alpha = jnp.exp2(m_prev - m_new)
        p = jnp.exp2(s - m_new).astype(BF16)
        acc_ref[...] = alpha * acc_ref[...] + jnp.dot(vt1, p, preferred_element_type=F32)
        m_ref[...] = m_new

    @pl.when(ki < qi)
    def _():
        update(False)

    @pl.when(ki == qi)
    def _():
        update(True)
        lam = lam_ref[0]
        inv = 1.0 / acc_ref[DIFF_VD:DIFF_VD + 1, :]
        o = (acc_ref[0:DIFF_VD, 0:tq] * inv[:, 0:tq]
             - lam * (acc_ref[0:DIFF_VD, tq:2 * tq] * inv[:, tq:2 * tq]))
        o = o * lax.rsqrt(jnp.mean(o * o, axis=0, keepdims=True) + RMS_EPS)
        o_ref[...] = (o.T * dn_ref[...] * (1.0 - LAMBDA_INIT)).astype(o_ref.dtype)


def diff_attn_prompt(q, k, vt, lam, diff_norm, batch, seq):
    tq = _tile(seq, FLASH_TILE)
    nq = seq // tq
    pairs =[(i, j) for i in range(nq) for j in range(i + 1)]
    qi_tab = jnp.asarray([p[0] for p in pairs], jnp.int32)
    ki_tab = jnp.asarray([p[1] for p in pairs], jnp.int32)
    grid_spec = pltpu.PrefetchScalarGridSpec(
        num_scalar_prefetch=3,
        grid=(batch, DIFF_HEADS, len(pairs)),
        in_specs=[
            pl.BlockSpec((tq, DIFF_VD), lambda b, h, s, qt, kt, lam: (b * nq + qt[s], h)),
            pl.BlockSpec((tq, DIFF_VD), lambda b, h, s, qt, kt, lam: (b * nq + kt[s], h)),
            pl.BlockSpec((DIFF_VD, tq), lambda b, h, s, qt, kt, lam: (h, b * nq + kt[s])),
            pl.BlockSpec((1, DIFF_VD), lambda b, h, s, qt, kt, lam: (0, 0)),
        ],
        out_specs=pl.BlockSpec((tq, DIFF_VD), lambda b, h, s, qt, kt, lam: (b * nq + qt[s], h)),
        scratch_shapes=[
            pltpu.VMEM((2 * tq, DIFF_VD), BF16),
            pltpu.VMEM((1, 2 * tq), F32),
            pltpu.VMEM((DIFF_VD + ONES_ROWS, 2 * tq), F32),
        ],
    )
    return pl.pallas_call(
        functools.partial(_flash_body, tq=tq),
        grid_spec=grid_spec,
        out_shape=jax.ShapeDtypeStruct((batch * seq, DIFF_WIDTH), BF16),
        compiler_params=_cparams("parallel", "parallel", "arbitrary"),
        name="diff_attn_prompt",
    )(qi_tab, ki_tab, lam.reshape(1), q, k, vt, diff_norm.reshape(1, DIFF_VD).astype(F32))


PAGE_ROWS = PAGE_SIZE * DIFF_HEADS


def _head_mask(shape, tp, extra_mask=None):
    r = lax.broadcasted_iota(jnp.int32, shape, 0)
    c = lax.broadcasted_iota(jnp.int32, shape, 1)
    ok = jnp.bitwise_and(c, DIFF_HEADS - 1) == jnp.right_shift(r, (2 * tp).bit_length() - 1)
    if extra_mask is not None:
        ok = jnp.logical_and(ok, extra_mask(r, c))
    return jnp.where(ok, 0.0, NEG_INIT).astype(F32)


def _paged_body(pt_ref, lam_ref, q_ref, *rest, n_tok, pp, tp):
    kp_refs = rest[0:pp]
    vp_refs = rest[pp:2 * pp]
    kn_ref, vn_ref, dn_ref, o_ref, m_ref, l_ref, acc_ref, bias_ref = rest[2 * pp:]
    p = pl.program_id(1)
    npg = pl.num_programs(1)
    kshift = DIFF_HEADS.bit_length() - 1

    @pl.when(p == 0)
    def _():
        m_ref[...] = jnp.full(m_ref.shape, NEG_INIT, F32)
        l_ref[...] = jnp.zeros(l_ref.shape, F32)
        acc_ref[...] = jnp.zeros(acc_ref.shape, F32)
        bias_ref[...] = _head_mask(bias_ref.shape, tp)

    qh, qm = _split2(q_ref[0])
    qcat = jnp.concatenate([qh, qh, qm], axis=1)

    def scores(kb, bias):
        kh, km = _split2(kb)
        kcat = jnp.concatenate([kh, km, kh], axis=1)
        return lax.dot_general(qcat, kcat, _NT, preferred_element_type=F32) + bias

    def weighted_values(pr, vb):
        ph, pm = _split2(pr)
        vh, vm = _split2(vb)
        wide = jnp.dot(ph, jnp.concatenate([vh, vm], axis=1), preferred_element_type=F32)
        return (wide[:, 0:DIFF_VD] + wide[:, DIFF_VD:2 * DIFF_VD]) + jnp.dot(pm, vh, preferred_element_type=F32)

    def update(ss, vbs):
        m_prev = m_ref[...]
        m_new = m_prev
        for s in ss:
            m_new = jnp.maximum(m_new, jnp.max(s, axis=1, keepdims=True))
        alpha = jnp.exp(m_prev - m_new)
        l_new = alpha * l_ref[...]
        acc = alpha * acc_ref[...]
        for s, vb in zip(ss, vbs):
            pr = jnp.exp(s - m_new)
            l_new = l_new + jnp.sum(pr, axis=1, keepdims=True)
            acc = acc + weighted_values(pr, vb)
        l_ref[...] = l_new
        acc_ref[...] = acc
        m_ref[...] = m_new

    bias = bias_ref[...]
    update([scores(kp_refs[j][0], bias) for j in range(pp)], [vp_refs[j][0] for j in range(pp)])

    @pl.when(p == npg - 1)
    def _():
        def causal(r, c):
            tok = jnp.right_shift(c, kshift)
            return jnp.logical_and(tok <= jnp.bitwise_and(r, tp - 1), tok < n_tok)

        update([scores(kn_ref[0], _head_mask((q_ref.shape[1], kn_ref.shape[1]), tp, causal))], [vn_ref[0]])
        lam = lam_ref[0]
        inv = 1.0 / l_ref[...]
        for h in range(DIFF_HEADS):
            r0 = h * 2 * tp
            if tp == SUBLANES:
                o1 = acc_ref[r0:r0 + tp, :] * inv[r0:r0 + tp]
                o2 = acc_ref[r0 + tp:r0 + 2 * tp, :] * inv[r0 + tp:r0 + 2 * tp]
            else:
                o1 = acc_ref[r0:r0 + SUBLANES, :] * inv[r0:r0 + SUBLANES]
                o2 = pltpu.roll(o1, tp, 0)
            o = o1 - lam * o2
            o = o * lax.rsqrt(jnp.mean(o * o, axis=-1, keepdims=True) + RMS_EPS) * dn_ref[...]
            o_ref[0, h * SUBLANES:(h + 1) * SUBLANES, :] = (o * (1.0 - LAMBDA_INIT)).astype(o_ref.dtype)


def diff_attn_sample(q, k_new, v_new, lam, diff_norm, cache_k, cache_v, page_table, n_tok):
    bd = q.shape[0]
    n_pool = cache_k.shape[0]
    n_pages = page_table.shape[1]
    assert n_tok <= SUBLANES
    tp = SUBLANES // 2 if n_tok <= SUBLANES // 2 else SUBLANES
    qrows_n = 2 * DIFF_HEADS * tp
    pp = PAGES_PER_STEP
    while n_pages % pp:
        pp //= 2
    qh = q.reshape(bd, n_tok, DIFF_HEADS, 2, DIFF_HD)
    qh = jnp.pad(qh, ((0, 0), (0, tp - n_tok), (0, 0), (0, 0), (0, 0)))
    qh = jnp.transpose(qh, (0, 2, 3, 1, 4))
    zeros = jnp.zeros_like(qh[:, :, 0])
    qrows = jnp.stack([jnp.concatenate([qh[:, :, 0], zeros], axis=-1),
                       jnp.concatenate([zeros, qh[:, :, 1]], axis=-1)], axis=2).reshape(bd, qrows_n, DIFF_VD)
    new_rows = tp * DIFF_HEADS
    out_rows = SUBLANES * DIFF_HEADS
    padn = ((0, 0), (0, new_rows - n_tok * DIFF_HEADS), (0, 0))
    kn = jnp.pad(k_new.reshape(bd, n_tok * DIFF_HEADS, DIFF_VD), padn)
    vn = jnp.pad(v_new.reshape(bd, n_tok * DIFF_HEADS, DIFF_VD), padn)
    ck = cache_k.reshape(n_pool, PAGE_ROWS, DIFF_VD)
    cv = cache_v.reshape(n_pool, PAGE_ROWS, DIFF_VD)

    def page_spec(j):
        return pl.BlockSpec((1, PAGE_ROWS, DIFF_VD), lambda b, p, pt, lam: (pt[b, p * pp + j], 0, 0))

    grid_spec = pltpu.PrefetchScalarGridSpec(
        num_scalar_prefetch=2,
        grid=(bd, n_pages // pp),
        in_specs=(
            [pl.BlockSpec((1, qrows_n, DIFF_VD), lambda b, p, pt, lam: (b, 0, 0))]
            + [page_spec(j) for j in range(pp)]
            + [page_spec(j) for j in range(pp)]
            + [pl.BlockSpec((1, new_rows, DIFF_VD), lambda b, p, pt, lam: (b, 0, 0)),
               pl.BlockSpec((1, new_rows, DIFF_VD), lambda b, p, pt, lam: (b, 0, 0)),
               pl.BlockSpec((1, DIFF_VD), lambda b, p, pt, lam: (0, 0))]
        ),
        out_specs=pl.BlockSpec((1, out_rows, DIFF_VD), lambda b, p, pt, lam: (b, 0, 0)),
        scratch_shapes=[
            pltpu.VMEM((qrows_n, 1), F32),
            pltpu.VMEM((qrows_n, 1), F32),
            pltpu.VMEM((qrows_n, DIFF_VD), F32),
            pltpu.VMEM((qrows_n, PAGE_ROWS), F32),
        ],
    )
    out = pl.pallas_call(
        functools.partial(_paged_body, n_tok=n_tok, pp=pp, tp=tp),
        grid_spec=grid_spec,
        out_shape=jax.ShapeDtypeStruct((bd, out_rows, DIFF_VD), F32),
        compiler_params=_cparams("parallel", "arbitrary"),
        cost_estimate=pl.CostEstimate(
            flops=2 * 5 * bd * n_pages * qrows_n * PAGE_ROWS * DIFF_VD,
            transcendentals=bd * n_pages * qrows_n * PAGE_ROWS,
            bytes_accessed=2 * bd * n_pages * PAGE_ROWS * DIFF_VD * 4),
        name="diff_attn_sample",
    )(page_table, lam.reshape(1), qrows, *([ck] * pp), *([cv] * pp), kn, vn,
      diff_norm.reshape(1, DIFF_VD).astype(F32))
    out = out.reshape(bd, DIFF_HEADS, SUBLANES, DIFF_VD)[:, :, :n_tok]
    return jnp.transpose(out, (0, 2, 1, 3)).reshape(bd, n_tok, DIFF_WIDTH)


def _mem_body(q_ref, k_ref, v_ref, o_ref):
    hp = q_ref.dtype == F32
    for h in range(MEM_HEADS):
        sl = slice(h * MEM_HD, (h + 1) * MEM_HD)
        s = _dotg(q_ref[0, :, sl], k_ref[0, :, sl], _NT, hp) * (MEM_HD ** -0.5)
        s = s - jnp.max(s, axis=1, keepdims=True)
        e = jnp.exp(s)
        p = e / jnp.sum(e, axis=1, keepdims=True)
        o_ref[0, :, sl] = _dotg(p, v_ref[0, :, sl], _NN, hp).astype(o_ref.dtype)


def mem_attn(q, mem_k, mem_v):
    b, t, _ = q.shape
    mlen = mem_k.shape[1]
    tq = _tile(t, 512)
    return pl.pallas_call(
        _mem_body,
        grid=(b, t // tq),
        in_specs=[
            pl.BlockSpec((1, tq, MEM_WIDTH), lambda i, j: (i, j, 0)),
            pl.BlockSpec((1, mlen, MEM_WIDTH), lambda i, j: (i, 0, 0)),
            pl.BlockSpec((1, mlen, MEM_WIDTH), lambda i, j: (i, 0, 0)),
        ],
        out_specs=pl.BlockSpec((1, tq, MEM_WIDTH), lambda i, j: (i, j, 0)),
        out_shape=jax.ShapeDtypeStruct((b, t, MEM_WIDTH), q.dtype),
        compiler_params=_cparams("parallel", "arbitrary"),
        name="mem_attn",
    )(q, mem_k, mem_v)


def _norm_router_body(x_ref, g_ref, wr_ref, br_ref, h_ref, rt_ref):
    x = x_ref[...]
    hn = x * lax.rsqrt(jnp.mean(x * x, axis=-1, keepdims=True) + RMS_EPS) * g_ref[...]
    h_ref[...] = hn.astype(h_ref.dtype)
    lg = _dot3(hn, wr_ref[...]) + br_ref[...]
    lane = lax.broadcasted_iota(jnp.int32, lg.shape, 1)
    lanef = lane.astype(F32)
    big = float(LANES)

    def first_argmax(vals, mask):
        top = jnp.max(jnp.where(mask, vals, NEG_INIT), axis=1, keepdims=True)
        idx = jnp.min(jnp.where(jnp.logical_and(mask, vals == top), lanef, big), axis=1, keepdims=True)
        return top, idx

    gmask = lane < N_GROUPS
    gtop, gidx = first_argmax(lg, gmask)
    g_w = 1.0 / jnp.sum(jnp.where(gmask, jnp.exp(lg - gtop), 0.0), axis=1, keepdims=True)
    first = N_GROUPS + EXPERTS_PER_GROUP * gidx.astype(jnp.int32)
    emask = jnp.logical_and(lane >= first, lane < first + EXPERTS_PER_GROUP)
    e1, i1 = first_argmax(lg, emask)
    e2, i2 = first_argmax(lg, jnp.logical_and(emask, lanef != i1))
    ez = jnp.sum(jnp.where(emask, jnp.exp(lg - e1), 0.0), axis=1, keepdims=True)
    p1 = 1.0 / ez
    p2 = jnp.exp(e2 - e1) / ez
    psum = p1 + p2
    rt_ref[...] = jnp.where(lane == 0, i1 - N_GROUPS,
                            jnp.where(lane == 1, i2 - N_GROUPS,
                                      jnp.where(lane == 2, g_w * (p1 / psum),
                                                jnp.where(lane == 3, g_w * (p2 / psum), 0.0))))


def norm_router(x, g, w_router, b_router):
    m, d = x.shape
    tm = _tile(m, 512)
    return pl.pallas_call(
        _norm_router_body,
        grid=(m // tm,),
        in_specs=[
            pl.BlockSpec((tm, d), lambda i: (i, 0)),
            pl.BlockSpec((1, d), lambda i: (0, 0)),
            pl.BlockSpec((d, LANES), lambda i: (0, 0)),
            pl.BlockSpec((1, LANES), lambda i: (0, 0)),
        ],
        out_specs=[pl.BlockSpec((tm, d), lambda i: (i, 0)), pl.BlockSpec((tm, LANES), lambda i: (i, 0))],
        out_shape=[jax.ShapeDtypeStruct((m, d), F32), jax.ShapeDtypeStruct((m, LANES), F32)],
        compiler_params=_cparams("parallel"),
        name="norm_router",
    )(x, g.reshape(1, d).astype(F32), w_router, b_router)


def _expert_body(te_ref, tv_ref, after_ref, x_ref, wg_ref, wu_ref, wd_ref, o_ref, wg16, wu16, wd16):
    del after_ref
    i = pl.program_id(0)
    new_expert = jnp.logical_or(i == 0, te_ref[i] != te_ref[jnp.maximum(i - 1, 0)])

    @pl.when(jnp.logical_and(tv_ref[i] != 0, new_expert))
    def _():
        wg16[...] = wg_ref[0].astype(BF16)
        wu16[...] = wu_ref[0].astype(BF16)
        wd16[...] = wd_ref[0].astype(BF16)

    @pl.when(tv_ref[i] != 0)
    def _():
        x = x_ref[...].astype(BF16)
        a = jnp.dot(x, wg16[...], preferred_element_type=F32)
        b = jnp.dot(x, wu16[...], preferred_element_type=F32)
        hid = (a * jax.nn.sigmoid(a)) * b
        o_ref[...] = jnp.dot(hid.astype(BF16), wd16[...], preferred_element_type=F32)

    @pl.when(tv_ref[i] == 0)
    def _():
        o_ref[...] = jnp.zeros(o_ref.shape, o_ref.dtype)


def grouped_experts(xs, tile_expert, tile_valid, after, w_gate, w_up, w_down, tm):
    rows, d = xs.shape
    nt = rows // tm
    ff = w_gate.shape[2]
    grid_spec = pltpu.PrefetchScalarGridSpec(
        num_scalar_prefetch=3,
        grid=(nt,),
        in_specs=[
            pl.BlockSpec((tm, d), lambda i, te, tv, af: (i, 0)),
            pl.BlockSpec((1, d, ff), lambda i, te, tv, af: (te[i], 0, 0)),
            pl.BlockSpec((1, d, ff), lambda i, te, tv, af: (te[i], 0, 0)),
            pl.BlockSpec((1, ff, d), lambda i, te, tv, af: (te[i], 0, 0)),
        ],
        out_specs=pl.BlockSpec((tm, d), lambda i, te, tv, af: (i, 0)),
        scratch_shapes=[pltpu.VMEM((d, ff), BF16), pltpu.VMEM((d, ff), BF16), pltpu.VMEM((ff, d), BF16)],
    )
    return pl.pallas_call(
        _expert_body,
        grid_spec=grid_spec,
        out_shape=jax.ShapeDtypeStruct((rows, d), F32),
        compiler_params=_cparams("arbitrary"),
        name="grouped_experts",
    )(tile_expert, tile_valid, after, xs, w_gate, w_up, w_down)


def _final_body(x_ref, ya_ref, yb_ref, rt_ref, g_ref, o_ref):
    rt = rt_ref[...]
    x = x_ref[...] + (_col(rt, 2) * ya_ref[...] + _col(rt, 3) * yb_ref[...])
    o_ref[...] = x * lax.rsqrt(jnp.mean(x * x, axis=-1, keepdims=True) + RMS_EPS) * g_ref[...]


def final_norm(x, ya, yb, row0, route, g):
    m, d = x.shape
    tm = _tile(m, 512)
    assert row0 % tm == 0
    blk0 = row0 // tm
    spec = pl.BlockSpec((tm, d), lambda i: (i, 0))
    yspec = pl.BlockSpec((tm, d), lambda i: (i + blk0, 0))
    return pl.pallas_call(
        _final_body,
        grid=(m // tm,),
        in_specs=[spec, yspec, yspec, pl.BlockSpec((tm, LANES), lambda i: (i, 0)),
                  pl.BlockSpec((1, d), lambda i: (0, 0))],
        out_specs=spec,
        out_shape=jax.ShapeDtypeStruct((m, d), F32),
        compiler_params=_cparams("parallel"),
        name="final_norm",
    )(x, ya, yb, route, g.reshape(1, d).astype(F32))


MOE_TILE = 256


def moe_and_final(x_list, norm_ffn, w_rg, b_rg, w_re, b_re, w_gate, w_up, w_down, norm_final, after=None):
    if after is None:
        after = jnp.zeros((1,), jnp.int32)
    else:
        after = (after.reshape(-1)[0:1] > jnp.finfo(F32).max).astype(jnp.int32)
    d = x_list[0].shape[1]
    w_router = jnp.zeros((d, LANES), F32).at[:, 0:N_GROUPS].set(w_rg.astype(F32))
    w_router = w_router.at[:, N_GROUPS:N_GROUPS + N_EXPERTS].set(w_re.astype(F32))
    b_router = jnp.zeros((1, LANES), F32).at[0, 0:N_GROUPS].set(b_rg.astype(F32))
    b_router = b_router.at[0, N_GROUPS:N_GROUPS + N_EXPERTS].set(b_re.astype(F32))
    hs, rts = [], []
    for x in x_list:
        hn, rt = norm_router(x, norm_ffn, w_router, b_router)
        hs.append(hn)
        rts.append(rt)
    hn = jnp.concatenate(hs, axis=0)
    n = hn.shape[0]
    expert_id = jnp.concatenate([rt[:, 0:TOP_K_INNER] for rt in rts], axis=0).astype(jnp.int32)
    tm = min(MOE_TILE, max(SUBLANES * 2, n * TOP_K_INNER // N_EXPERTS * 4))
    flat_e = expert_id.reshape(-1)
    na = flat_e.shape[0]
    onehot = (flat_e[:, None] == jnp.arange(N_EXPERTS, dtype=jnp.int32)[None, :]).astype(jnp.int32)
    csum = jnp.cumsum(onehot, axis=0)
    counts = csum[-1]
    padded = ((counts + tm - 1) // tm) * tm
    pad_end = jnp.cumsum(padded)
    pad_start = pad_end - padded
    dest = jnp.sum(onehot * (csum - 1 + pad_start[None, :]), axis=1)
    nt = (na + tm - 1) // tm + N_EXPERTS
    rows = nt * tm
    row_tok = jnp.zeros((rows,), jnp.int32).at[dest].set(jnp.arange(na, dtype=jnp.int32) // TOP_K_INNER)
    tile_start = jnp.arange(nt, dtype=jnp.int32) * tm
    tile_expert = jnp.minimum(jnp.sum((pad_end[None, :] <= tile_start[:, None]).astype(jnp.int32), axis=1),
                              N_EXPERTS - 1)
    tile_valid = (tile_start < pad_end[-1]).astype(jnp.int32)
    xs = hn.at[row_tok].get(mode='promise_in_bounds')
    ys = grouped_experts(xs, tile_expert, tile_valid, after, w_gate, w_up, w_down, tm)
    dest2 = dest.reshape(n, TOP_K_INNER)
    ya = ys.at[dest2[:, 0]].get(mode='promise_in_bounds')
    yb = ys.at[dest2[:, 1]].get(mode='promise_in_bounds')
    outs, o = [], 0
    for x, rt in zip(x_list, rts):
        outs.append(final_norm(x, ya, yb, o, rt, norm_final))
        o += x.shape[0]
    return outs


def _rope_tables(pos, rows):
    half = ROT_DIM // 2
    inv_freq = ROPE_THETA ** (-jnp.arange(0, ROT_DIM, 2, dtype=F32) / ROT_DIM)
    ang = pos.astype(F32)[:, None] * inv_freq[None, :]
    cos, sin = jnp.cos(ang), jnp.sin(ang)
    t = pos.shape[0]
    one = jnp.ones((t, DIFF_HD - ROT_DIM), F32)
    zero = jnp.zeros((t, DIFF_HD - ROT_DIM), F32)
    zh = jnp.zeros((t, half), F32)
    c = jnp.concatenate([cos, cos, one], axis=1)
    s1 = jnp.concatenate([-sin, zh, zero], axis=1)
    s2 = jnp.concatenate([zh, sin, zero], axis=1)
    rep = rows // t
    tile = lambda a: jnp.tile(jnp.concatenate([a, a], axis=1), (rep, 1))
    return tile(c), tile(s1), tile(s2)


def _prep_weights(p, dtype):
    w_in = p['w_in']
    o = 4 * GDN_WIDTH
    ob = o + 2 * GDN_HEADS
    d = w_in.shape[0]
    w = {}
    w['a'] = w_in[:, 0:o].astype(dtype)
    w['bd'] = jnp.zeros((d, LANES), dtype).at[:, 0:2 * GDN_HEADS].set(w_in[:, o:ob].astype(dtype))
    w['dq'] = w_in[:, ob:ob + DIFF_WIDTH].astype(dtype)
    w['dk'] = w_in[:, ob + DIFF_WIDTH:ob + 2 * DIFF_WIDTH].astype(dtype)
    w['dv'] = w_in[:, ob + 2 * DIFF_WIDTH:ob + 3 * DIFF_WIDTH].astype(dtype)
    w['g'] = w_in[:, ob + 3 * DIFF_WIDTH:].astype(dtype)
    for name in ('w_branch_a', 'w_branch_b', 'w_out', 'w_mq', 'w_mk', 'w_mv', 'w_mo'):
        w[name] = p[name].astype(dtype)
    return w


def _mixers(x, batch, seq, pos, conv_state, delta_state, mem_k, mem_v, p, w, lam, sample_ctx):
    m, d = x.shape
    hp = sample_ctx is not None
    od = F32 if hp else BF16
    h = rmsnorm_rows(x, p['norm_mix'], out_dtype=od)
    u_a = matmul(h, w['a'])
    u_bd = matmul(h, w['bd'])
    u_g = matmul(h, w['g'])
    tm = _tile(m, 1024)
    tables = _rope_tables(pos, max(tm, seq))
    d_v = matmul(h, w['dv'])
    d_k = matmul(h, w['dk'], rope=tables)
    chunk = min(GDN_CHUNK, seq)
    seq_pad = seq
    if chunk % SUBLANES:
        chunk = SUBLANES
        seq_pad = SUBLANES
        padr = lambda a: jnp.pad(a.reshape(batch, seq, -1), ((0, 0), (0, seq_pad - seq), (0, 0))).reshape(
            batch * seq_pad, -1)
        u_a_g, u_bd_g = padr(u_a), padr(u_bd)
    else:
        u_a_g, u_bd_g = u_a, u_bd
    state8 = jnp.pad(conv_state.astype(F32), ((0, 0), (SUBLANES - (CONV_W - 1), 0), (0, 0)))
    o_a, new_delta = gdn(u_a_g, u_bd_g, state8, delta_state.astype(F32), p['w_conv'], p['a_log'], p['dt_bias'],
                         p['gdn_norm'], batch, seq_pad, chunk, min(seq, chunk), hp)
    if seq_pad != seq:
        o_a = o_a.reshape(batch, seq_pad, GDN_WIDTH)[:, :seq].reshape(m, GDN_WIDTH)
    keep = min(seq, CONV_W - 1)
    tail = u_a.reshape(batch, seq, -1)[:, seq - keep:, 0:3 * GDN_WIDTH]
    new_conv = jnp.concatenate([conv_state.astype(F32)[:, keep:], tail], axis=1)
    if sample_ctx is None:
        d_q = matmul(h, w['dq'], out_dtype=BF16, rope=tables, scale=DIFF_HD ** -0.5 * LOG2E)
        o_b = diff_attn_prompt(d_q, d_k.astype(BF16), d_v.astype(BF16).T, lam, p['diff_norm'], batch, seq)
    else:
        cache_k, cache_v, page_table = sample_ctx
        d_q = matmul(h, w['dq'], rope=tables, scale=DIFF_HD ** -0.5)
        o_b = diff_attn_sample(d_q.reshape(batch, seq, -1), d_k.reshape(batch, seq, -1), d_v.reshape(batch, seq, -1),
                               lam, p['diff_norm'], cache_k, cache_v, page_table, seq)
        o_b = o_b[:, :seq].reshape(m, DIFF_WIDTH)
    mixed = merge_branches(o_a, o_b, w['w_branch_a'], w['w_branch_b'], u_g)
    x = matmul(mixed, w['w_out'], residual=x)
    hc = rmsnorm_rows(x, p['norm_cross'], out_dtype=od)
    mq = matmul(hc, w['w_mq'], out_dtype=od).reshape(batch, seq, MEM_WIDTH)
    if seq % SUBLANES:
        mq = jnp.pad(mq, ((0, 0), (0, SUBLANES - seq), (0, 0)))
    mo = mem_attn(mq, mem_k, mem_v)[:, :seq].reshape(m, MEM_WIDTH)
    x = matmul(mo, w['w_mo'], residual=x)
    return x, new_conv, new_delta, d_k, d_v, o_b


def kernel(x_prompt, x_sample, cache_k, cache_v, cache_mem_k, cache_mem_v, state_delta, state_conv, page_table, mem_prompt, norm_mix, w_in, w_conv, a_log, dt_bias, gdn_norm, lambda_q1, lambda_k1, lambda_q2, lambda_k2, diff_norm, w_branch_a, w_branch_b, w_out, norm_cross, norm_mem, w_mq, w_mk, w_mv, w_mo, norm_ffn, w_router_group, b_router_group, w_router_expert, b_router_expert, w_gate, w_up, w_down, norm_final):
    p = dict(norm_mix=norm_mix, w_in=w_in, w_conv=w_conv, a_log=a_log, dt_bias=dt_bias, gdn_norm=gdn_norm,
             diff_norm=diff_norm, w_branch_a=w_branch_a, w_branch_b=w_branch_b, w_out=w_out,
             norm_cross=norm_cross, w_mq=w_mq, w_mk=w_mk, w_mv=w_mv, w_mo=w_mo)
    bp, tp, d = x_prompt.shape
    bs, ts, _ = x_sample.shape
    past_len = page_table.shape[1] * PAGE_SIZE
    w = _prep_weights(p, BF16)
    w_hp = _prep_weights(p, F32)
    lam =(jnp.exp(jnp.sum(lambda_q1.astype(F32) * lambda_k1.astype(F32)))
           - jnp.exp(jnp.sum(lambda_q2.astype(F32) * lambda_k2.astype(F32))) + LAMBDA_INIT)
    mlen = mem_prompt.shape[1]
    hm = rmsnorm_rows(mem_prompt.reshape(bp * mlen, d), norm_mem)
    mem_k_p = matmul(hm, w['w_mk'])
    mem_v_p = matmul(hm, w['w_mv'])
    conv0 = jnp.zeros((bp, CONV_W - 1, 3 * GDN_WIDTH), F32)
    delta0 = jnp.zeros((bp, GDN_HEADS, GDN_HD, GDN_HD), F32)
    xp, conv_p, delta_p, k_p, v_p, _ = _mixers(
        x_prompt.reshape(bp * tp, d), bp, tp, jnp.arange(tp), conv0, delta0,
        mem_k_p.astype(BF16).reshape(bp, mlen, MEM_WIDTH), mem_v_p.astype(BF16).reshape(bp, mlen, MEM_WIDTH),
        p, w, lam, None)
    xs, conv_s, delta_s, k_s, v_s, attn_s = _mixers(
        x_sample.reshape(bs * ts, d), bs, ts, past_len + jnp.arange(ts), state_conv, state_delta,
        cache_mem_k.astype(F32).reshape(bs, -1, MEM_WIDTH), cache_mem_v.astype(F32).reshape(bs, -1, MEM_WIDTH),
        p, w_hp, lam, (cache_k, cache_v, page_table))
    moe_w = (norm_ffn, w_router_group, b_router_group, w_router_expert, b_router_expert, w_gate, w_up, w_down,
             norm_final)
    yp, = moe_and_final([xp], *moe_w, after=attn_s)
    ys, = moe_and_final([xs], *moe_w)
    return (yp.reshape(bp, tp, d), ys.reshape(bs, ts, d),
            k_p.reshape(bp, tp, DIFF_HEADS, DIFF_VD), v_p.reshape(bp, tp, DIFF_HEADS, DIFF_VD),
            mem_k_p.reshape(bp, mlen, MEM_HEADS, MEM_HD), mem_v_p.reshape(bp, mlen, MEM_HEADS, MEM_HD),
            delta_p.astype(state_delta.dtype), conv_p.astype(x_prompt.dtype),
            k_s.reshape(bs, ts, DIFF_HEADS, DIFF_VD), v_s.reshape(bs, ts, DIFF_HEADS, DIFF_VD),
            delta_s.astype(state_delta.dtype), conv_s.astype(state_conv.dtype))
```

```python
import functools
import math

import jax
import jax.numpy as jnp
from jax import lax
from jax.experimental import pallas as pl
from jax.experimental.pallas import tpu as pltpu

F32 = jnp.float32
BF16 = jnp.bfloat16

GDN_HEADS = 8
GDN_HD = 128
GDN_WIDTH = GDN_HEADS * GDN_HD
CONV_W = 4
GDN_CHUNK = 128
DIFF_HEADS = 8
DIFF_HD = 64
DIFF_VD = 2 * DIFF_HD
DIFF_WIDTH = DIFF_HEADS * DIFF_VD
ROT_DIM = DIFF_HD // 4
ROPE_THETA = 500000.0
LAMBDA_INIT = 0.2
PAGE_SIZE = 128
MEM_HEADS = 4
MEM_HD = 128
MEM_WIDTH = MEM_HEADS * MEM_HD
N_GROUPS = 4
EXPERTS_PER_GROUP = 8
N_EXPERTS = N_GROUPS * EXPERTS_PER_GROUP
TOP_K_INNER = 2
RMS_EPS = 1e-6
L2_EPS = 1e-6
NEG_INIT = -1e30

LANES = 128
SUBLANES = 8
VMEM_LIMIT = 48 * 1024 * 1024
VMEM_LIMIT_EXPERTS = 56 * 1024 * 1024
FLASH_TILE = 1024
ONES_ROWS = 16
PAGES_PER_STEP = 16
LOG2E = math.log2(math.e)

def _cparams(*sem, vmem=VMEM_LIMIT):
    return pltpu.CompilerParams(dimension_semantics=sem, vmem_limit_bytes=vmem)


def _tile(n, pref):
    if n <= pref:
        return n
    t = pref
    while n % t:
        t //= 2
    return t


_NN = (((1,), (0,)), ((), ()))
_NT = (((1,), (1,)), ((), ()))
_TN = (((0,), (0,)), ((), ()))


def _dotg(a, b, dims=_NN, hp=False):
    d = lambda x, y: lax.dot_general(x, y, dims, preferred_element_type=F32)
    if not hp:
        return d(a.astype(BF16), b.astype(BF16))
    (ah, am), (bh, bm) = _split2(a), _split2(b)
    return d(ah, bh) + (d(ah, bm) + d(am, bh))


def _mm(a, b):
    return _dotg(a, b, _NN, hp=(a.dtype == F32 and b.dtype == F32))


def _split3(x):
    hi = x.astype(BF16)
    r = x - hi.astype(F32)
    mid = r.astype(BF16)
    lo = (r - mid.astype(F32)).astype(BF16)
    return hi, mid, lo


def _split2(x):
    hi = x.astype(BF16)
    return hi, (x - hi.astype(F32)).astype(BF16)


def _dot3s(a2, b2):
    (ah, am), (bh, bm) = a2, b2
    d = lambda x, y: jnp.dot(x, y, preferred_element_type=F32)
    return d(ah, bh) + (d(ah, bm) + d(am, bh))


def _dot3(a, b):
    return _dot3s(_split2(a), _split2(b))


def _rmsnorm_body(x_ref, g_ref, o_ref):
    x = x_ref[...]
    ms = jnp.mean(x * x, axis=-1, keepdims=True)
    o_ref[...] = (x * lax.rsqrt(ms + RMS_EPS) * g_ref[...]).astype(o_ref.dtype)


def rmsnorm_rows(x, g, out_dtype=BF16):
    m, d = x.shape
    tm = _tile(m, 512)
    return pl.pallas_call(
        _rmsnorm_body,
        grid=(m // tm,),
        in_specs=[pl.BlockSpec((tm, d), lambda i: (i, 0)), pl.BlockSpec((1, d), lambda i: (0, 0))],
        out_specs=pl.BlockSpec((tm, d), lambda i: (i, 0)),
        out_shape=jax.ShapeDtypeStruct((m, d), out_dtype),
        compiler_params=_cparams("parallel"),
        name="rmsnorm_rows",
    )(x, g.reshape(1, d).astype(F32))


def _mm_body(a_ref, w_ref, o_ref):
    o_ref[...] = _mm(a_ref[...], w_ref[...]).astype(o_ref.dtype)


def _mm_res_body(a_ref, w_ref, r_ref, o_ref):
    o_ref[...] = r_ref[...] + _mm(a_ref[...], w_ref[...])


def _mm_rope_body(a_ref, w_ref, c_ref, s1_ref, s2_ref, o_ref, *, scale, reps):
    o = _mm(a_ref[...], w_ref[...])
    tn = o.shape[1]
    c = jnp.tile(c_ref[...], (1, reps))
    s1 = jnp.tile(s1_ref[...], (1, reps))
    s2 = jnp.tile(s2_ref[...], (1, reps))
    r = o * c + pltpu.roll(o, tn - ROT_DIM // 2, 1) * s1 + pltpu.roll(o, ROT_DIM // 2, 1) * s2
    if scale != 1.0:
        r = r * scale
    o_ref[...] = r.astype(o_ref.dtype)


def matmul(a, w, out_dtype=F32, residual=None, rope=None, scale=1.0):
    m, k = a.shape
    n = w.shape[1]
    tm = _tile(m, 1024)
    tn = _tile(n, 1024 if a.dtype == BF16 else 512)
    in_specs = [pl.BlockSpec((tm, k), lambda i, j: (i, 0)), pl.BlockSpec((k, tn), lambda i, j: (0, j))]
    args = [a, w]
    if residual is not None:
        body = _mm_res_body
        in_specs.append(pl.BlockSpec((tm, tn), lambda i, j: (i, j)))
        args.append(residual)
    elif rope is not None:
        c, s1, s2 = rope
        nt = c.shape[0] // tm
        body = functools.partial(_mm_rope_body, scale=scale, reps=tn // LANES)
        tspec = pl.BlockSpec((tm, LANES), lambda i, j: (i % nt, 0))
        in_specs += [tspec, tspec, tspec]
        args += [c, s1, s2]
    else:
        body = _mm_body
    return pl.pallas_call(
        body,
        grid=(m // tm, n // tn),
        in_specs=in_specs,
        out_specs=pl.BlockSpec((tm, tn), lambda i, j: (i, j)),
        out_shape=jax.ShapeDtypeStruct((m, n), out_dtype),
        compiler_params=_cparams("parallel", "arbitrary"),
        name="matmul",
    )(*args)


def _merge_body(oa_ref, ob_ref, wa_ref, wb_ref, ga_ref, gb_ref, o_ref):
    br_a = _mm(oa_ref[...], wa_ref[...])
    br_b = _mm(ob_ref[...], wb_ref[...])
    o_ref[...] = (jax.nn.sigmoid(ga_ref[...]) * br_a + jax.nn.sigmoid(gb_ref[...]) * br_b).astype(o_ref.dtype)


def merge_branches(oa, ob, wa, wb, gates):
    m, ka = oa.shape
    n = wa.shape[1]
    tm = _tile(m, 1024)
    tn = _tile(n, 512)
    nb = n // tn
    return pl.pallas_call(
        _merge_body,
        grid=(m // tm, nb),
        in_specs=[
            pl.BlockSpec((tm, ka), lambda i, j: (i, 0)),
            pl.BlockSpec((tm, ob.shape[1]), lambda i, j: (i, 0)),
            pl.BlockSpec((ka, tn), lambda i, j: (0, j)),
            pl.BlockSpec((wb.shape[0], tn), lambda i, j: (0, j)),
            pl.BlockSpec((tm, tn), lambda i, j: (i, j)),
            pl.BlockSpec((tm, tn), lambda i, j: (i, j + nb)),
        ],
        out_specs=pl.BlockSpec((tm, tn), lambda i, j: (i, j)),
        out_shape=jax.ShapeDtypeStruct((m, n), oa.dtype),
        compiler_params=_cparams("parallel", "arbitrary"),
        name="merge_branches",
    )(oa, ob, wa, wb, gates, gates)


def _col(arr, idx):
    lane = lax.broadcasted_iota(jnp.int32, arr.shape, 1)
    return jnp.sum(jnp.where(lane == idx, arr, 0.0), axis=1, keepdims=True)


def _row(arr, idx):
    sub = lax.broadcasted_iota(jnp.int32, arr.shape, 0)
    return jnp.sum(jnp.where(sub == idx, arr, 0.0), axis=0, keepdims=True)


def _gdn_body(cur_ref, prev_ref, st8_ref, bd_ref, s0_ref, wconv_ref, alog_ref, dtb_ref, gnorm_ref,
              o_ref, s_ref, ext_ref, *, chunk, n_valid, hp):
    c = pl.program_id(1)
    C = chunk
    H = GDN_HEADS
    W3 = 3 * GDN_WIDTH

    @pl.when(c == 0)
    def _():
        s_ref[...] = s0_ref[...]
        ext_ref[0:SUBLANES, :] = st8_ref[0]

    @pl.when(c != 0)
    def _():
        ext_ref[0:SUBLANES, :] = prev_ref[:, 0:W3]

    ext_ref[SUBLANES:SUBLANES + C, :] = cur_ref[:, 0:W3]
    base = SUBLANES - (CONV_W - 1)
    acc = ext_ref[base:base + C, :] * wconv_ref[0:1, :]
    for i in range(1, CONV_W):
        acc = acc + ext_ref[base + i:base + i + C, :] * wconv_ref[i:i + 1, :]
    qkv = acc * jax.nn.sigmoid(acc)

    bd = bd_ref[...]
    lane = lax.broadcasted_iota(jnp.int32, bd.shape, 1)
    beta_all = jax.nn.sigmoid(bd)
    xg = bd + dtb_ref[...]
    softplus = jnp.maximum(xg, 0.0) + jnp.log1p(jnp.exp(-jnp.abs(xg)))
    g_all = -jnp.exp(alog_ref[...]) * softplus
    gb = jnp.where(lane < H, beta_all, jnp.where(lane < 2 * H, g_all, 0.0))
    if n_valid < C:
        gb = jnp.where(lax.broadcasted_iota(jnp.int32, bd.shape, 0) < n_valid, gb, 0.0)
    ri = lax.broadcasted_iota(jnp.int32, (C, C), 0)
    ci = lax.broadcasted_iota(jnp.int32, (C, C), 1)
    causal = ri >= ci
    strict = ri > ci
    ltri = jnp.where(causal, 1.0, 0.0).astype(BF16)
    gc_cols = sum(jnp.dot(ltri, piece, preferred_element_type=F32) for piece in _split3(gb))
    gc_rows = gc_cols.T
    eye = jnp.where(ri == ci, 1.0, 0.0).astype(F32)

    heads = []
    for h in range(H):
        q = qkv[:, h * GDN_HD:(h + 1) * GDN_HD]
        k = qkv[:, GDN_WIDTH + h * GDN_HD:GDN_WIDTH + (h + 1) * GDN_HD]
        v = qkv[:, 2 * GDN_WIDTH + h * GDN_HD:2 * GDN_WIDTH + (h + 1) * GDN_HD]
        q = q * lax.rsqrt(jnp.sum(q * q, axis=-1, keepdims=True) + L2_EPS) * (GDN_HD ** -0.5)
        k = k * lax.rsqrt(jnp.sum(k * k, axis=-1, keepdims=True) + L2_EPS)
        beta_c = _col(gb, h)
        gc_c = _col(gc_cols, H + h)
        gc_r = _row(gc_rows, H + h)
        g_last = gc_c[C - 1:C, :]
        decay = jnp.where(causal, jnp.exp(jnp.where(causal, gc_c - gc_r, 0.0)), 0.0)
        kb = k * beta_c
        kk = _dotg(jnp.concatenate([kb, q], axis=0), k, _NT, hp)
        a_kk = jnp.where(strict, kk[0:C] * decay, 0.0)
        a_qk = jnp.where(causal, kk[C:2 * C] * decay, 0.0)
        rhs = jnp.concatenate([v * beta_c, kb * jnp.exp(gc_c)], axis=1)
        heads.append(dict(q=q, k=k, gc_c=gc_c, g_last=g_last, a_qk=a_qk, rhs=rhs, nmat=-a_kk, tinv=eye - a_kk))
    for hd in heads:
        hd['ns'] = _split2(hd['nmat'])
    span = 2
    while span < C:
        for hd in heads:
            hd['ns'] = _split2(_dot3s(hd['ns'], hd['ns']))
        for hd in heads:
            hd['tinv'] = hd['tinv'] + _dot3s(_split2(hd['tinv']), hd['ns'])
        span *= 2
    for hd in heads:
        hd['sol'] = _dot3s(_split2(hd['tinv']), _split2(hd['rhs']))
    for h, hd in enumerate(heads):
        sl = slice(h * GDN_HD, (h + 1) * GDN_HD)
        q, k, gc_c, g_last = hd['q'], hd['k'], hd['gc_c'], hd['g_last']
        u = hd['sol'][:, 0:GDN_HD]
        w = hd['sol'][:, GDN_HD:2 * GDN_HD]
        s = s_ref[0, h]
        ws = _dotg(jnp.concatenate([w, q * jnp.exp(gc_c)], axis=0), s, _NN, hp)
        v_new = u - ws[0:C]
        o = ws[C:2 * C] + _dotg(hd['a_qk'], v_new, _NN, hp)
        s_ref[0, h] = s * jnp.exp(g_last) + _dotg(k * jnp.exp(g_last - gc_c), v_new, _TN, hp)
        z = cur_ref[:, W3 + h * GDN_HD:W3 + (h + 1) * GDN_HD]
        o = o * lax.rsqrt(jnp.mean(o * o, axis=-1, keepdims=True) + RMS_EPS) * gnorm_ref[...]
        o = o * (z * jax.nn.sigmoid(z))
        o_ref[:, sl] = o.astype(o_ref.dtype)


def gdn(u_a, u_bd, state8, s0, w_conv, a_log, dt_bias, gdn_norm, batch, seq, chunk, n_valid, hp):
    nc = seq // chunk
    rb = chunk // SUBLANES
    alog = jnp.zeros((1, LANES), F32).at[0, GDN_HEADS:2 * GDN_HEADS].set(a_log.astype(F32))
    dtb = jnp.zeros((1, LANES), F32).at[0, GDN_HEADS:2 * GDN_HEADS].set(dt_bias.astype(F32))
    W3 = 3 * GDN_WIDTH
    return pl.pallas_call(
        functools.partial(_gdn_body, chunk=chunk, n_valid=n_valid, hp=hp),
        grid=(batch, nc),
        in_specs=[
            pl.BlockSpec((chunk, 4 * GDN_WIDTH), lambda b, c: (b * nc + c, 0)),
            pl.BlockSpec((SUBLANES, 4 * GDN_WIDTH), lambda b, c: (jnp.maximum((b * nc + c) * rb - 1, 0), 0)),
            pl.BlockSpec((1, SUBLANES, W3), lambda b, c: (b, 0, 0)),
            pl.BlockSpec((chunk, LANES), lambda b, c: (b * nc + c, 0)),
            pl.BlockSpec((1, GDN_HEADS, GDN_HD, GDN_HD), lambda b, c: (b, 0, 0, 0)),
            pl.BlockSpec((CONV_W, W3), lambda b, c: (0, 0)),
            pl.BlockSpec((1, LANES), lambda b, c: (0, 0)),
            pl.BlockSpec((1, LANES), lambda b, c: (0, 0)),
            pl.BlockSpec((1, GDN_HD), lambda b, c: (0, 0)),
        ],
        out_specs=[
            pl.BlockSpec((chunk, GDN_WIDTH), lambda b, c: (b * nc + c, 0)),
            pl.BlockSpec((1, GDN_HEADS, GDN_HD, GDN_HD), lambda b, c: (b, 0, 0, 0)),
        ],
        out_shape=[
            jax.ShapeDtypeStruct((batch * seq, GDN_WIDTH), F32 if hp else BF16),
            jax.ShapeDtypeStruct((batch, GDN_HEADS, GDN_HD, GDN_HD), F32),
        ],
        scratch_shapes=[pltpu.VMEM((SUBLANES + chunk, W3), F32)],
        compiler_params=_cparams("parallel", "arbitrary"),
        name="gdn",
    )(u_a, u_a, state8, u_bd, s0, w_conv.astype(F32), alog, dtb, gdn_norm.reshape(1, GDN_HD).astype(F32))


def _flash_body(qi_ref, ki_ref, lam_ref, q_ref, k_ref, vt_ref, dn_ref, o_ref, qs_ref, m_ref, acc_ref, *, tq):
    step = pl.program_id(2)
    qi = qi_ref[step]
    ki = ki_ref[step]

    @pl.when(ki == 0)
    def _():
        q = q_ref[...]
        lane = lax.broadcasted_iota(jnp.int32, q.shape, 1)
        zero = jnp.zeros_like(q)
        qs_ref[0:tq, :] = jnp.where(lane < DIFF_HD, q, zero)
        qs_ref[tq:2 * tq, :] = jnp.where(lane >= DIFF_HD, q, zero)
        m_ref[...] = jnp.full(m_ref.shape, NEG_INIT, F32)
        acc_ref[...] = jnp.zeros(acc_ref.shape, F32)

    def update(masked):
        vt1 = jnp.concatenate([vt_ref[...], jnp.ones((ONES_ROWS, vt_ref.shape[1]), BF16)], axis=0)
        s = lax.dot_general(k_ref[...], qs_ref[...], _NT, preferred_element_type=F32)
        if masked:
            key = lax.broadcasted_iota(jnp.int32, s.shape, 0)
            qry = lax.rem(lax.broadcasted_iota(jnp.int32, s.shape, 1), tq)
            s = jnp.where(key <= qry, s, NEG_INIT)
        m_prev = m_ref[...]
        m_new = jnp.maximum(m_prev, jnp.max(s, axis=0, keepdims=True))
        alpha = jnp.exp2(m_prev - m_new)
        p = jnp.exp2(s - m_new).astype(BF16)
        acc_ref[...] = alpha * acc_ref[...] + jnp.dot(vt1, p, preferred_element_type=F32)
        m_ref[...] = m_new

    @pl.when(ki < qi)
    def _():
        update(False)

    @pl.when(ki == qi)
    def _():
        update(True)
        lam = lam_ref[0]
        inv = 1.0 / acc_ref[DIFF_VD:DIFF_VD + 1, :]
        o = (acc_ref[0:DIFF_VD, 0:tq] * inv[:, 0:tq]
             - lam * (acc_ref[0:DIFF_VD, tq:2 * tq] * inv[:, tq:2 * tq]))
        o = o * lax.rsqrt(jnp.mean(o * o, axis=0, keepdims=True) + RMS_EPS)
        o_ref[...] = (o.T * dn_ref[...] * (1.0 - LAMBDA_INIT)).astype(o_ref.dtype)


def diff_attn_prompt(q, k, vt, lam, diff_norm, batch, seq):
    tq = _tile(seq, FLASH_TILE)
    nq = seq // tq
    pairs =[(i, j) for i in range(nq) for j in range(i + 1)]
    qi_tab = jnp.asarray([p[0] for p in pairs], jnp.int32)
    ki_tab = jnp.asarray([p[1] for p in pairs], jnp.int32)
    grid_spec = pltpu.PrefetchScalarGridSpec(
        num_scalar_prefetch=3,
        grid=(batch, DIFF_HEADS, len(pairs)),
        in_specs=[
            pl.BlockSpec((tq, DIFF_VD), lambda b, h, s, qt, kt, lam: (b * nq + qt[s], h)),
            pl.BlockSpec((tq, DIFF_VD), lambda b, h, s, qt, kt, lam: (b * nq + kt[s], h)),
            pl.BlockSpec((DIFF_VD, tq), lambda b, h, s, qt, kt, lam: (h, b * nq + kt[s])),
            pl.BlockSpec((1, DIFF_VD), lambda b, h, s, qt, kt, lam: (0, 0)),
        ],
        out_specs=pl.BlockSpec((tq, DIFF_VD), lambda b, h, s, qt, kt, lam: (b * nq + qt[s], h)),
        scratch_shapes=[
            pltpu.VMEM((2 * tq, DIFF_VD), BF16),
            pltpu.VMEM((1, 2 * tq), F32),
            pltpu.VMEM((DIFF_VD + ONES_ROWS, 2 * tq), F32),
        ],
    )
    return pl.pallas_call(
        functools.partial(_flash_body, tq=tq),
        grid_spec=grid_spec,
        out_shape=jax.ShapeDtypeStruct((batch * seq, DIFF_WIDTH), BF16),
        compiler_params=_cparams("parallel", "parallel", "arbitrary"),
        name="diff_attn_prompt",
    )(qi_tab, ki_tab, lam.reshape(1), q, k, vt, diff_norm.reshape(1, DIFF_VD).astype(F32))


PAGE_ROWS = PAGE_SIZE * DIFF_HEADS


def _head_mask(shape, tp, extra_mask=None):
    r = lax.broadcasted_iota(jnp.int32, shape, 0)
    c = lax.broadcasted_iota(jnp.int32, shape, 1)
    ok = jnp.bitwise_and(c, DIFF_HEADS - 1) == jnp.right_shift(r, (2 * tp).bit_length() - 1)
    if extra_mask is not None:
        ok = jnp.logical_and(ok, extra_mask(r, c))
    return jnp.where(ok, 0.0, NEG_INIT).astype(F32)


def _paged_body(pt_ref, lam_ref, q_ref, *rest, n_tok, pp, tp):
    kp_refs = rest[0:pp]
    vp_refs = rest[pp:2 * pp]
    kn_ref, vn_ref, dn_ref, o_ref, m_ref, l_ref, acc_ref, bias_ref = rest[2 * pp:]
    p = pl.program_id(1)
    npg = pl.num_programs(1)
    kshift = DIFF_HEADS.bit_length() - 1

    @pl.when(p == 0)
    def _():
        m_ref[...] = jnp.full(m_ref.shape, NEG_INIT, F32)
        l_ref[...] = jnp.zeros(l_ref.shape, F32)
        acc_ref[...] = jnp.zeros(acc_ref.shape, F32)
        bias_ref[...] = _head_mask(bias_ref.shape, tp)

    qh, qm = _split2(q_ref[0])
    qcat = jnp.concatenate([qh, qh, qm], axis=1)

    def scores(kb, bias):
        kh, km = _split2(kb)
        kcat = jnp.concatenate([kh, km, kh], axis=1)
        return lax.dot_general(qcat, kcat, _NT, preferred_element_type=F32) + bias

    def weighted_values(pr, vb):
        ph, pm = _split2(pr)
        vh, vm = _split2(vb)
        wide = jnp.dot(ph, jnp.concatenate([vh, vm], axis=1), preferred_element_type=F32)
        return (wide[:, 0:DIFF_VD] + wide[:, DIFF_VD:2 * DIFF_VD]) + jnp.dot(pm, vh, preferred_element_type=F32)

    def update(ss, vbs):
        m_prev = m_ref[...]
        m_new = m_prev
        for s in ss:
            m_new = jnp.maximum(m_new, jnp.max(s, axis=1, keepdims=True))
        alpha = jnp.exp(m_prev - m_new)
        l_new = alpha * l_ref[...]
        acc = alpha * acc_ref[...]
        for s, vb in zip(ss, vbs):
            pr = jnp.exp(s - m_new)
            l_new = l_new + jnp.sum(pr, axis=1, keepdims=True)
            acc = acc + weighted_values(pr, vb)
        l_ref[...] = l_new
        acc_ref[...] = acc
        m_ref[...] = m_new

    bias = bias_ref[...]
    update([scores(kp_refs[j][0], bias) for j in range(pp)], [vp_refs[j][0] for j in range(pp)])

    @pl.when(p == npg - 1)
    def _():
        def causal(r, c):
            tok = jnp.right_shift(c, kshift)
            return jnp.logical_and(tok <= jnp.bitwise_and(r, tp - 1), tok < n_tok)

        update([scores(kn_ref[0], _head_mask((q_ref.shape[1], kn_ref.shape[1]), tp, causal))], [vn_ref[0]])
        lam = lam_ref[0]
        inv = 1.0 / l_ref[...]
        for h in range(DIFF_HEADS):
            r0 = h * 2 * tp
            if tp == SUBLANES:
                o1 = acc_ref[r0:r0 + tp, :] * inv[r0:r0 + tp]
                o2 = acc_ref[r0 + tp:r0 + 2 * tp, :] * inv[r0 + tp:r0 + 2 * tp]
            else:
                o1 = acc_ref[r0:r0 + SUBLANES, :] * inv[r0:r0 + SUBLANES]
                o2 = pltpu.roll(o1, tp, 0)
            o = o1 - lam * o2
            o = o * lax.rsqrt(jnp.mean(o * o, axis=-1, keepdims=True) + RMS_EPS) * dn_ref[...]
            o_ref[0, h * SUBLANES:(h + 1) * SUBLANES, :] = (o * (1.0 - LAMBDA_INIT)).astype(o_ref.dtype)


def diff_attn_sample(q, k_new, v_new, lam, diff_norm, cache_k, cache_v, page_table, n_tok):
    bd = q.shape[0]
    n_pool = cache_k.shape[0]
    n_pages = page_table.shape[1]
    assert n_tok <= SUBLANES
    tp = SUBLANES // 2 if n_tok <= SUBLANES // 2 else SUBLANES
    qrows_n = 2 * DIFF_HEADS * tp
    pp = PAGES_PER_STEP
    while n_pages % pp:
        pp //= 2
    qh = q.reshape(bd, n_tok, DIFF_HEADS, 2, DIFF_HD)
    qh = jnp.pad(qh, ((0, 0), (0, tp - n_tok), (0, 0), (0, 0), (0, 0)))
    qh = jnp.transpose(qh, (0, 2, 3, 1, 4))
    zeros = jnp.zeros_like(qh[:, :, 0])
    qrows = jnp.stack([jnp.concatenate([qh[:, :, 0], zeros], axis=-1),
                       jnp.concatenate([zeros, qh[:, :, 1]], axis=-1)], axis=2).reshape(bd, qrows_n, DIFF_VD)
    new_rows = tp * DIFF_HEADS
    out_rows = SUBLANES * DIFF_HEADS
    padn = ((0, 0), (0, new_rows - n_tok * DIFF_HEADS), (0, 0))
    kn = jnp.pad(k_new.reshape(bd, n_tok * DIFF_HEADS, DIFF_VD), padn)
    vn = jnp.pad(v_new.reshape(bd, n_tok * DIFF_HEADS, DIFF_VD), padn)
    ck = cache_k.reshape(n_pool, PAGE_ROWS, DIFF_VD)
    cv = cache_v.reshape(n_pool, PAGE_ROWS, DIFF_VD)

    def page_spec(j):
        return pl.BlockSpec((1, PAGE_ROWS, DIFF_VD), lambda b, p, pt, lam: (pt[b, p * pp + j], 0, 0))

    grid_spec = pltpu.PrefetchScalarGridSpec(
        num_scalar_prefetch=2,
        grid=(bd, n_pages // pp),
        in_specs=(
            [pl.BlockSpec((1, qrows_n, DIFF_VD), lambda b, p, pt, lam: (b, 0, 0))]
            + [page_spec(j) for j in range(pp)]
            + [page_spec(j) for j in range(pp)]
            + [pl.BlockSpec((1, new_rows, DIFF_VD), lambda b, p, pt, lam: (b, 0, 0)),
               pl.BlockSpec((1, new_rows, DIFF_VD), lambda b, p, pt, lam: (b, 0, 0)),
               pl.BlockSpec((1, DIFF_VD), lambda b, p, pt, lam: (0, 0))]
        ),
        out_specs=pl.BlockSpec((1, out_rows, DIFF_VD), lambda b, p, pt, lam: (b, 0, 0)),
        scratch_shapes=[
            pltpu.VMEM((qrows_n, 1), F32),
            pltpu.VMEM((qrows_n, 1), F32),
            pltpu.VMEM((qrows_n, DIFF_VD), F32),
            pltpu.VMEM((qrows_n, PAGE_ROWS), F32),
        ],
    )
    out = pl.pallas_call(
        functools.partial(_paged_body, n_tok=n_tok, pp=pp, tp=tp),
        grid_spec=grid_spec,
        out_shape=jax.ShapeDtypeStruct((bd, out_rows, DIFF_VD), F32),
        compiler_params=_cparams("parallel", "arbitrary"),
        cost_estimate=pl.CostEstimate(
            flops=2 * 5 * bd * n_pages * qrows_n * PAGE_ROWS * DIFF_VD,
            transcendentals=bd * n_pages * qrows_n * PAGE_ROWS,
            bytes_accessed=2 * bd * n_pages * PAGE_ROWS * DIFF_VD * 4),
        name="diff_attn_sample",
    )(page_table, lam.reshape(1), qrows, *([ck] * pp), *([cv] * pp), kn, vn,
      diff_norm.reshape(1, DIFF_VD).astype(F32))
    out = out.reshape(bd, DIFF_HEADS, SUBLANES, DIFF_VD)[:, :, :n_tok]
    return jnp.transpose(out, (0, 2, 1, 3)).reshape(bd, n_tok, DIFF_WIDTH)


def _mem_body(q_ref, k_ref, v_ref, o_ref):
    hp = q_ref.dtype == F32
    for h in range(MEM_HEADS):
        sl = slice(h * MEM_HD, (h + 1) * MEM_HD)
        s = _dotg(q_ref[0, :, sl], k_ref[0, :, sl], _NT, hp) * (MEM_HD ** -0.5)
        s = s - jnp.max(s, axis=1, keepdims=True)
        e = jnp.exp(s)
        p = e / jnp.sum(e, axis=1, keepdims=True)
        o_ref[0, :, sl] = _dotg(p, v_ref[0, :, sl], _NN, hp).astype(o_ref.dtype)


def mem_attn(q, mem_k, mem_v):
    b, t, _ = q.shape
    mlen = mem_k.shape[1]
    tq = _tile(t, 512)
    return pl.pallas_call(
        _mem_body,
        grid=(b, t // tq),
        in_specs=[
            pl.BlockSpec((1, tq, MEM_WIDTH), lambda i, j: (i, j, 0)),
            pl.BlockSpec((1, mlen, MEM_WIDTH), lambda i, j: (i, 0, 0)),
            pl.BlockSpec((1, mlen, MEM_WIDTH), lambda i, j: (i, 0, 0)),
        ],
        out_specs=pl.BlockSpec((1, tq, MEM_WIDTH), lambda i, j: (i, j, 0)),
        out_shape=jax.ShapeDtypeStruct((b, t, MEM_WIDTH), q.dtype),
        compiler_params=_cparams("parallel", "arbitrary"),
        name="mem_attn",
    )(q, mem_k, mem_v)


def _norm_router_body(x_ref, g_ref, wr_ref, br_ref, h_ref, rt_ref):
    x = x_ref[...]
    hn = x * lax.rsqrt(jnp.mean(x * x, axis=-1, keepdims=True) + RMS_EPS) * g_ref[...]
    h_ref[...] = hn.astype(h_ref.dtype)
    lg = _dot3(hn, wr_ref[...]) + br_ref[...]
    lane = lax.broadcasted_iota(jnp.int32, lg.shape, 1)
    lanef = lane.astype(F32)
    big = float(LANES)

    def first_argmax(vals, mask):
        top = jnp.max(jnp.where(mask, vals, -jnp.inf), axis=1, keepdims=True)
        idx = jnp.min(jnp.where(jnp.logical_and(mask, vals == top), lanef, big), axis=1, keepdims=True)
        return top, idx

    gmask = lane < N_GROUPS
    gtop, gidx = first_argmax(lg, gmask)
    g_w = 1.0 / jnp.sum(jnp.where(gmask, jnp.exp(lg - gtop), 0.0), axis=1, keepdims=True)
    first = N_GROUPS + EXPERTS_PER_GROUP * gidx.astype(jnp.int32)
    emask = jnp.logical_and(lane >= first, lane < first + EXPERTS_PER_GROUP)
    e1, i1 = first_argmax(lg, emask)
    e2, i2 = first_argmax(lg, jnp.logical_and(emask, lanef != i1))
    ez = jnp.sum(jnp.where(emask, jnp.exp(lg - e1), 0.0), axis=1, keepdims=True)
    p1 = 1.0 / ez
    p2 = jnp.exp(e2 - e1) / ez
    psum = p1 + p2
    rt_ref[...] = jnp.where(lane == 0, i1 - N_GROUPS,
                            jnp.where(lane == 1, i2 - N_GROUPS,
                                      jnp.where(lane == 2, g_w * (p1 / psum),
                                                jnp.where(lane == 3, g_w * (p2 / psum), 0.0))))


def norm_router(x, g, w_router, b_router):
    m, d = x.shape
    tm = _tile(m, 512)
    return pl.pallas_call(
        _norm_router_body,
        grid=(m // tm,),
        in_specs=[
            pl.BlockSpec((tm, d), lambda i: (i, 0)),
            pl.BlockSpec((1, d), lambda i: (0, 0)),
            pl.BlockSpec((d, LANES), lambda i: (0, 0)),
            pl.BlockSpec((1, LANES), lambda i: (0, 0)),
        ],
        out_specs=[pl.BlockSpec((tm, d), lambda i: (i, 0)), pl.BlockSpec((tm, LANES), lambda i: (i, 0))],
        out_shape=[jax.ShapeDtypeStruct((m, d), F32), jax.ShapeDtypeStruct((m, LANES), F32)],
        compiler_params=_cparams("parallel"),
        name="norm_router",
    )(x, g.reshape(1, d).astype(F32), w_router, b_router)


def _expert_body(te_ref, tv_ref, after_ref, x_ref, wg_ref, wu_ref, wd_ref, o_ref, wg16, wu16, wd16):
    del after_ref
    i = pl.program_id(0)
    new_expert = jnp.logical_or(i == 0, te_ref[i] != te_ref[jnp.maximum(i - 1, 0)])

    @pl.when(jnp.logical_and(tv_ref[i] != 0, new_expert))
    def _():
        wg16[...] = wg_ref[0].astype(BF16)
        wu16[...] = wu_ref[0].astype(BF16)
        wd16[...] = wd_ref[0].astype(BF16)

    @pl.when(tv_ref[i] != 0)
    def _():
        x = x_ref[...].astype(BF16)
        a = jnp.dot(x, wg16[...], preferred_element_type=F32)
        b = jnp.dot(x, wu16[...], preferred_element_type=F32)
        hid = (a * jax.nn.sigmoid(a)) * b
        o_ref[...] = jnp.dot(hid.astype(BF16), wd16[...], preferred_element_type=F32)

    @pl.when(tv_ref[i] == 0)
    def _():
        o_ref[...] = jnp.zeros(o_ref.shape, o_ref.dtype)


def grouped_experts(xs, tile_expert, tile_valid, after, w_gate, w_up, w_down, tm):
    rows, d = xs.shape
    nt = rows // tm
    ff = w_gate.shape[2]
    grid_spec = pltpu.PrefetchScalarGridSpec(
        num_scalar_prefetch=3,
        grid=(nt,),
        in_specs=[
            pl.BlockSpec((tm, d), lambda i, te, tv, af: (i, 0)),
            pl.BlockSpec((1, d, ff), lambda i, te, tv, af: (te[i], 0, 0)),
            pl.BlockSpec((1, d, ff), lambda i, te, tv, af: (te[i], 0, 0)),
            pl.BlockSpec((1, ff, d), lambda i, te, tv, af: (te[i], 0, 0)),
        ],
        out_specs=pl.BlockSpec((tm, d), lambda i, te, tv, af: (i, 0)),
        scratch_shapes=[pltpu.VMEM((d, ff), BF16), pltpu.VMEM((d, ff), BF16), pltpu.VMEM((ff, d), BF16)],
    )
    return pl.pallas_call(
        _expert_body,
        grid_spec=grid_spec,
        out_shape=jax.ShapeDtypeStruct((rows, d), F32),
        compiler_params=_cparams("arbitrary", vmem=VMEM_LIMIT_EXPERTS),
        name="grouped_experts",
    )(tile_expert, tile_valid, after, xs, w_gate, w_up, w_down)


def _final_body(x_ref, ya_ref, yb_ref, rt_ref, g_ref, o_ref):
    rt = rt_ref[...]
    x = x_ref[...] + (_col(rt, 2) * ya_ref[...] + _col(rt, 3) * yb_ref[...])
    o_ref[...] = x * lax.rsqrt(jnp.mean(x * x, axis=-1, keepdims=True) + RMS_EPS) * g_ref[...]


def final_norm(x, ya, yb, row0, route, g):
    m, d = x.shape
    tm = _tile(m, 512)
    assert row0 % tm == 0
    blk0 = row0 // tm
    spec = pl.BlockSpec((tm, d), lambda i: (i, 0))
    yspec = pl.BlockSpec((tm, d), lambda i: (i + blk0, 0))
    return pl.pallas_call(
        _final_body,
        grid=(m // tm,),
        in_specs=[spec, yspec, yspec, pl.BlockSpec((tm, LANES), lambda i: (i, 0)),
                  pl.BlockSpec((1, d), lambda i: (0, 0))],
        out_specs=spec,
        out_shape=jax.ShapeDtypeStruct((m, d), F32),
        compiler_params=_cparams("parallel"),
        name="final_norm",
    )(x, ya, yb, route, g.reshape(1, d).astype(F32))


MOE_TILE = 512


def moe_and_final(x_list, norm_ffn, w_rg, b_rg, w_re, b_re, w_gate, w_up, w_down, norm_final, after=None):
    if after is None:
        after = jnp.zeros((1,), jnp.int32)
    else:
        after = (after.reshape(-1)[0:1] > jnp.finfo(F32).max).astype(jnp.int32)
    d = x_list[0].shape[1]
    w_router = jnp.zeros((d, LANES), F32).at[:, 0:N_GROUPS].set(w_rg.astype(F32))
    w_router = w_router.at[:, N_GROUPS:N_GROUPS + N_EXPERTS].set(w_re.astype(F32))
    b_router = jnp.zeros((1, LANES), F32).at[0, 0:N_GROUPS].set(b_rg.astype(F32))
    b_router = b_router.at[0, N_GROUPS:N_GROUPS + N_EXPERTS].set(b_re.astype(F32))
    hs, rts = [], []
    for x in x_list:
        hn, rt = norm_router(x, norm_ffn, w_router, b_router)
        hs.append(hn)
        rts.append(rt)
    hn = jnp.concatenate(hs, axis=0)
    n = hn.shape[0]
    expert_id = jnp.concatenate([rt[:, 0:TOP_K_INNER] for rt in rts], axis=0).astype(jnp.int32)
    tm = min(MOE_TILE, max(SUBLANES * 2, n * TOP_K_INNER // N_EXPERTS * 4))
    flat_e = expert_id.reshape(-1)
    na = flat_e.shape[0]
    onehot = (flat_e[:, None] == jnp.arange(N_EXPERTS, dtype=jnp.int32)[None, :]).astype(jnp.int32)
    csum = jnp.cumsum(onehot, axis=0)
    counts = csum[-1]
    padded = ((counts + tm - 1) // tm) * tm
    pad_end = jnp.cumsum(padded)
    pad_start = pad_end - padded
    dest = jnp.sum(onehot * (csum - 1 + pad_start[None, :]), axis=1)
    nt = (na + tm - 1) // tm + N_EXPERTS
    rows = nt * tm
    row_tok = jnp.zeros((rows,), jnp.int32).at[dest].set(jnp.arange(na, dtype=jnp.int32) // TOP_K_INNER)
    tile_start = jnp.arange(nt, dtype=jnp.int32) * tm
    tile_expert = jnp.minimum(jnp.sum((pad_end[None, :] <= tile_start[:, None]).astype(jnp.int32), axis=1),
                              N_EXPERTS - 1)
    tile_valid = (tile_start < pad_end[-1]).astype(jnp.int32)
    xs = hn.at[row_tok].get(mode='promise_in_bounds')
    ys = grouped_experts(xs, tile_expert, tile_valid, after, w_gate, w_up, w_down, tm)
    dest2 = dest.reshape(n, TOP_K_INNER)
    ya = ys.at[dest2[:, 0]].get(mode='promise_in_bounds')
    yb = ys.at[dest2[:, 1]].get(mode='promise_in_bounds')
    outs, o = [], 0
    for x, rt in zip(x_list, rts):
        outs.append(final_norm(x, ya, yb, o, rt, norm_final))
        o += x.shape[0]
    return outs


def _rope_tables(pos, rows):
    half = ROT_DIM // 2
    inv_freq = ROPE_THETA ** (-jnp.arange(0, ROT_DIM, 2, dtype=F32) / ROT_DIM)
    ang = pos.astype(F32)[:, None] * inv_freq[None, :]
    cos, sin = jnp.cos(ang), jnp.sin(ang)
    t = pos.shape[0]
    one = jnp.ones((t, DIFF_HD - ROT_DIM), F32)
    zero = jnp.zeros((t, DIFF_HD - ROT_DIM), F32)
    zh = jnp.zeros((t, half), F32)
    c = jnp.concatenate([cos, cos, one], axis=1)
    s1 = jnp.concatenate([-sin, zh, zero], axis=1)
    s2 = jnp.concatenate([zh, sin, zero], axis=1)
    rep = rows // t
    tile = lambda a: jnp.tile(jnp.concatenate([a, a], axis=1), (rep, 1))
    return tile(c), tile(s1), tile(s2)


def _prep_weights(p, dtype):
    w_in = p['w_in']
    o = 4 * GDN_WIDTH
    ob = o + 2 * GDN_HEADS
    d = w_in.shape[0]
    w = {}
    w['a'] = w_in[:, 0:o].astype(dtype)
    w['bd'] = jnp.zeros((d, LANES), dtype).at[:, 0:2 * GDN_HEADS].set(w_in[:, o:ob].astype(dtype))
    w['dq'] = w_in[:, ob:ob + DIFF_WIDTH].astype(dtype)
    w['dk'] = w_in[:, ob + DIFF_WIDTH:ob + 2 * DIFF_WIDTH].astype(dtype)
    w['dv'] = w_in[:, ob + 2 * DIFF_WIDTH:ob + 3 * DIFF_WIDTH].astype(dtype)
    w['g'] = w_in[:, ob + 3 * DIFF_WIDTH:].astype(dtype)
    for name in ('w_branch_a', 'w_branch_b', 'w_out', 'w_mq', 'w_mk', 'w_mv', 'w_mo'):
        w[name] = p[name].astype(dtype)
    return w


def _mixers(x, batch, seq, pos, conv_state, delta_state, mem_k, mem_v, p, w, lam, sample_ctx):
    m, d = x.shape
    hp = sample_ctx is not None
    od = F32 if hp else BF16
    h = rmsnorm_rows(x, p['norm_mix'], out_dtype=od)
    u_a = matmul(h, w['a'])
    u_bd = matmul(h, w['bd'])
    u_g = matmul(h, w['g'])
    tm = _tile(m, 1024)
    tables = _rope_tables(pos, max(tm, seq))
    d_v = matmul(h, w['dv'])
    d_k = matmul(h, w['dk'], rope=tables)
    chunk = min(GDN_CHUNK, seq)
    seq_pad = seq
    if chunk % SUBLANES:
        chunk = SUBLANES
        seq_pad = SUBLANES
        padr = lambda a: jnp.pad(a.reshape(batch, seq, -1), ((0, 0), (0, seq_pad - seq), (0, 0))).reshape(
            batch * seq_pad, -1)
        u_a_g, u_bd_g = padr(u_a), padr(u_bd)
    else:
        u_a_g, u_bd_g = u_a, u_bd
    state8 = jnp.pad(conv_state.astype(F32), ((0, 0), (SUBLANES - (CONV_W - 1), 0), (0, 0)))
    o_a, new_delta = gdn(u_a_g, u_bd_g, state8, delta_state.astype(F32), p['w_conv'], p['a_log'], p['dt_bias'],
                         p['gdn_norm'], batch, seq_pad, chunk, min(seq, chunk), hp)
    if seq_pad != seq:
        o_a = o_a.reshape(batch, seq_pad, GDN_WIDTH)[:, :seq].reshape(m, GDN_WIDTH)
    keep = min(seq, CONV_W - 1)
    tail = u_a.reshape(batch, seq, -1)[:, seq - keep:, 0:3 * GDN_WIDTH]
    new_conv = jnp.concatenate([conv_state.astype(F32)[:, keep:], tail], axis=1)
    if sample_ctx is None:
        d_q = matmul(h, w['dq'], out_dtype=BF16, rope=tables, scale=DIFF_HD ** -0.5 * LOG2E)
        o_b = diff_attn_prompt(d_q, d_k.astype(BF16), d_v.astype(BF16).T, lam, p['diff_norm'], batch, seq)
    else:
        cache_k, cache_v, page_table = sample_ctx
        d_q = matmul(h, w['dq'], rope=tables, scale=DIFF_HD ** -0.5)
        o_b = diff_attn_sample(d_q.reshape(batch, seq, -1), d_k.reshape(batch, seq, -1), d_v.reshape(batch, seq, -1),
                               lam, p['diff_norm'], cache_k, cache_v, page_table, seq)
        o_b = o_b[:, :seq].reshape(m, DIFF_WIDTH)
    mixed = merge_branches(o_a, o_b, w['w_branch_a'], w['w_branch_b'], u_g)
    x = matmul(mixed, w['w_out'], residual=x)
    hc = rmsnorm_rows(x, p['norm_cross'], out_dtype=od)
    mq = matmul(hc, w['w_mq'], out_dtype=od).reshape(batch, seq, MEM_WIDTH)
    if seq % SUBLANES:
        mq = jnp.pad(mq, ((0, 0), (0, SUBLANES - seq), (0, 0)))
    mo = mem_attn(mq, mem_k, mem_v)[:, :seq].reshape(m, MEM_WIDTH)
    x = matmul(mo, w['w_mo'], residual=x)
    return x, new_conv, new_delta, d_k, d_v, o_b


def kernel(x_prompt, x_sample, cache_k, cache_v, cache_mem_k, cache_mem_v, state_delta, state_conv, page_table, mem_prompt, norm_mix, w_in, w_conv, a_log, dt_bias, gdn_norm, lambda_q1, lambda_k1, lambda_q2, lambda_k2, diff_norm, w_branch_a, w_branch_b, w_out, norm_cross, norm_mem, w_mq, w_mk, w_mv, w_mo, norm_ffn, w_router_group, b_router_group, w_router_expert, b_router_expert, w_gate, w_up, w_down, norm_final):
    p = dict(norm_mix=norm_mix, w_in=w_in, w_conv=w_conv, a_log=a_log, dt_bias=dt_bias, gdn_norm=gdn_norm,
             diff_norm=diff_norm, w_branch_a=w_branch_a, w_branch_b=w_branch_b, w_out=w_out,
             norm_cross=norm_cross, w_mq=w_mq, w_mk=w_mk, w_mv=w_mv, w_mo=w_mo)
    bp, tp, d = x_prompt.shape
    bs, ts, _ = x_sample.shape
    past_len = page_table.shape[1] * PAGE_SIZE
    w = _prep_weights(p, BF16)
    w_hp = _prep_weights(p, F32)
    lam =(jnp.exp(jnp.sum(lambda_q1.astype(F32) * lambda_k1.astype(F32)))
           - jnp.exp(jnp.sum(lambda_q2.astype(F32) * lambda_k2.astype(F32))) + LAMBDA_INIT)
    mlen = mem_prompt.shape[1]
    hm = rmsnorm_rows(mem_prompt.reshape(bp * mlen, d), norm_mem)
    mem_k_p = matmul(hm, w['w_mk'])
    mem_v_p = matmul(hm, w['w_mv'])
    conv0 = jnp.zeros((bp, CONV_W - 1, 3 * GDN_WIDTH), F32)
    delta0 = jnp.zeros((bp, GDN_HEADS, GDN_HD, GDN_HD), F32)
    xp, conv_p, delta_p, k_p, v_p, _ = _mixers(
        x_prompt.reshape(bp * tp, d), bp, tp, jnp.arange(tp), conv0, delta0,
        mem_k_p.astype(BF16).reshape(bp, mlen, MEM_WIDTH), mem_v_p.astype(BF16).reshape(bp, mlen, MEM_WIDTH),
        p, w, lam, None)
    xs, conv_s, delta_s, k_s, v_s, attn_s = _mixers(
        x_sample.reshape(bs * ts, d), bs, ts, past_len + jnp.arange(ts), state_conv, state_delta,
        cache_mem_k.astype(F32).reshape(bs, -1, MEM_WIDTH), cache_mem_v.astype(F32).reshape(bs, -1, MEM_WIDTH),
        p, w_hp, lam, (cache_k, cache_v, page_table))
    moe_w = (norm_ffn, w_router_group, b_router_group, w_router_expert, b_router_expert, w_gate, w_up, w_down,
             norm_final)
    yp, = moe_and_final([xp], *moe_w, after=attn_s)
    ys, = moe_and_final([xs], *moe_w)
    return (yp.reshape(bp, tp, d), ys.reshape(bs, ts, d),
            k_p.reshape(bp, tp, DIFF_HEADS, DIFF_VD), v_p.reshape(bp, tp, DIFF_HEADS, DIFF_VD),
            mem_k_p.reshape(bp, mlen, MEM_HEADS, MEM_HD), mem_v_p.reshape(bp, mlen, MEM_HEADS, MEM_HD),
            delta_p.astype(state_delta.dtype), conv_p.astype(x_prompt.dtype),
            k_s.reshape(bs, ts, DIFF_HEADS, DIFF_VD), v_s.reshape(bs, ts, DIFF_HEADS, DIFF_VD),
            delta_s.astype(state_delta.dtype), conv_s.astype(state_conv.dtype))
```

```python
import functools
import math

import jax
import jax.numpy as jnp
from jax import lax
from jax.experimental import pallas as pl
from jax.experimental.pallas import tpu as pltpu

F32 = jnp.float32
BF16 = jnp.bfloat16

GDN_HEADS = 8
GDN_HD = 128
GDN_WIDTH = GDN_HEADS * GDN_HD
CONV_W = 4
GDN_CHUNK = 128
DIFF_HEADS = 8
DIFF_HD = 64
DIFF_VD = 2 * DIFF_HD
DIFF_WIDTH = DIFF_HEADS * DIFF_VD
ROT_DIM = DIFF_HD // 4
ROPE_THETA = 500000.0
LAMBDA_INIT = 0.2
PAGE_SIZE = 128
MEM_HEADS = 4
MEM_HD = 128
MEM_WIDTH = MEM_HEADS * MEM_HD
N_GROUPS = 4
EXPERTS_PER_GROUP = 8
N_EXPERTS = N_GROUPS * EXPERTS_PER_GROUP
TOP_K_INNER = 2
RMS_EPS = 1e-6
L2_EPS = 1e-6
NEG_INIT = -1e30

LANES = 128
SUBLANES = 8
VMEM_LIMIT = 48 * 1024 * 1024
FLASH_TILE = 1024
ONES_ROWS = 16
PAGES_PER_STEP = 16
LOG2E = math.log2(math.e)

def _cparams(*sem):
    return pltpu.CompilerParams(dimension_semantics=sem, vmem_limit_bytes=VMEM_LIMIT)


def _tile(n, pref):
    if n <= pref:
        return n
    t = pref
    while n % t:
        t //= 2
    return t


_NN = (((1,), (0,)), ((), ()))
_NT = (((1,), (1,)), ((), ()))
_TN = (((0,), (0,)), ((), ()))


def _dotg(a, b, dims=_NN, hp=False):
    d = lambda x, y: lax.dot_general(x, y, dims, preferred_element_type=F32)
    if not hp:
        return d(a.astype(BF16), b.astype(BF16))
    (ah, am), (bh, bm) = _split2(a), _split2(b)
    return d(ah, bh) + (d(ah, bm) + d(am, bh))


def _mm(a, b):
    return _dotg(a, b, _NN, hp=(a.dtype == F32 and b.dtype == F32))


def _split3(x):
    hi = x.astype(BF16)
    r = x - hi.astype(F32)
    mid = r.astype(BF16)
    lo = (r - mid.astype(F32)).astype(BF16)
    return hi, mid, lo


def _split2(x):
    hi = x.astype(BF16)
    return hi, (x - hi.astype(F32)).astype(BF16)


def _dot3s(a2, b2):
    (ah, am), (bh, bm) = a2, b2
    d = lambda x, y: jnp.dot(x, y, preferred_element_type=F32)
    return d(ah, bh) + (d(ah, bm) + d(am, bh))


def _dot3(a, b):
    return _dot3s(_split2(a), _split2(b))


def _rmsnorm_body(x_ref, g_ref, o_ref):
    x = x_ref[...]
    ms = jnp.mean(x * x, axis=-1, keepdims=True)
    o_ref[...] = (x * lax.rsqrt(ms + RMS_EPS) * g_ref[...]).astype(o_ref.dtype)


def rmsnorm_rows(x, g, out_dtype=BF16):
    m, d = x.shape
    tm = _tile(m, 512)
    return pl.pallas_call(
        _rmsnorm_body,
        grid=(m // tm,),
        in_specs=[pl.BlockSpec((tm, d), lambda i: (i, 0)), pl.BlockSpec((1, d), lambda i: (0, 0))],
        out_specs=pl.BlockSpec((tm, d), lambda i: (i, 0)),
        out_shape=jax.ShapeDtypeStruct((m, d), out_dtype),
        compiler_params=_cparams("parallel"),
        name="rmsnorm_rows",
    )(x, g.reshape(1, d).astype(F32))


def _mm_body(a_ref, w_ref, o_ref):
    o_ref[...] = _mm(a_ref[...], w_ref[...]).astype(o_ref.dtype)


def _mm_res_body(a_ref, w_ref, r_ref, o_ref):
    o_ref[...] = r_ref[...] + _mm(a_ref[...], w_ref[...])


def _mm_rope_body(a_ref, w_ref, c_ref, s1_ref, s2_ref, o_ref, *, scale, reps):
    o = _mm(a_ref[...], w_ref[...])
    tn = o.shape[1]
    c = jnp.tile(c_ref[...], (1, reps))
    s1 = jnp.tile(s1_ref[...], (1, reps))
    s2 = jnp.tile(s2_ref[...], (1, reps))
    r = o * c + pltpu.roll(o, tn - ROT_DIM // 2, 1) * s1 + pltpu.roll(o, ROT_DIM // 2, 1) * s2
    if scale != 1.0:
        r = r * scale
    o_ref[...] = r.astype(o_ref.dtype)


def matmul(a, w, out_dtype=F32, residual=None, rope=None, scale=1.0):
    m, k = a.shape
    n = w.shape[1]
    tm = _tile(m, 1024)
    tn = _tile(n, 1024 if a.dtype == BF16 else 512)
    in_specs = [pl.BlockSpec((tm, k), lambda i, j: (i, 0)), pl.BlockSpec((k, tn), lambda i, j: (0, j))]
    args = [a, w]
    if residual is not None:
        body = _mm_res_body
        in_specs.append(pl.BlockSpec((tm, tn), lambda i, j: (i, j)))
        args.append(residual)
    elif rope is not None:
        c, s1, s2 = rope
        nt = c.shape[0] // tm
        body = functools.partial(_mm_rope_body, scale=scale, reps=tn // LANES)
        tspec = pl.BlockSpec((tm, LANES), lambda i, j: (i % nt, 0))
        in_specs += [tspec, tspec, tspec]
        args += [c, s1, s2]
    else:
        body = _mm_body
    return pl.pallas_call(
        body,
        grid=(m // tm, n // tn),
        in_specs=in_specs,
        out_specs=pl.BlockSpec((tm, tn), lambda i, j: (i, j)),
        out_shape=jax.ShapeDtypeStruct((m, n), out_dtype),
        compiler_params=_cparams("parallel", "arbitrary"),
        name="matmul",
    )(*args)


def _merge_body(oa_ref, ob_ref, wa_ref, wb_ref, ga_ref, gb_ref, o_ref):
    br_a = _mm(oa_ref[...], wa_ref[...])
    br_b = _mm(ob_ref[...], wb_ref[...])
    o_ref[...] = (jax.nn.sigmoid(ga_ref[...]) * br_a + jax.nn.sigmoid(gb_ref[...]) * br_b).astype(o_ref.dtype)


def merge_branches(oa, ob, wa, wb, gates):
    m, ka = oa.shape
    n = wa.shape[1]
    tm = _tile(m, 1024)
    tn = _tile(n, 512)
    nb = n // tn
    return pl.pallas_call(
        _merge_body,
        grid=(m // tm, nb),
        in_specs=[
            pl.BlockSpec((tm, ka), lambda i, j: (i, 0)),
            pl.BlockSpec((tm, ob.shape[1]), lambda i, j: (i, 0)),
            pl.BlockSpec((ka, tn), lambda i, j: (0, j)),
            pl.BlockSpec((wb.shape[0], tn), lambda i, j: (0, j)),
            pl.BlockSpec((tm, tn), lambda i, j: (i, j)),
            pl.BlockSpec((tm, tn), lambda i, j: (i, j + nb)),
        ],
        out_specs=pl.BlockSpec((tm, tn), lambda i, j: (i, j)),
        out_shape=jax.ShapeDtypeStruct((m, n), oa.dtype),
        compiler_params=_cparams("parallel", "arbitrary"),
        name="merge_branches",
    )(oa, ob, wa, wb, gates, gates)


def _col(arr, idx):
    lane = lax.broadcasted_iota(jnp.int32, arr.shape, 1)
    return jnp.sum(jnp.where(lane == idx, arr, 0.0), axis=1, keepdims=True)


def _row(arr, idx):
    sub = lax.broadcasted_iota(jnp.int32, arr.shape, 0)
    return jnp.sum(jnp.where(sub == idx, arr, 0.0), axis=0, keepdims=True)


def _gdn_body(cur_ref, prev_ref, st8_ref, bd_ref, s0_ref, wconv_ref, alog_ref, dtb_ref, gnorm_ref,
              o_ref, s_ref, ext_ref, *, chunk, n_valid, hp):
    c = pl.program_id(1)
    C = chunk
    H = GDN_HEADS
    W3 = 3 * GDN_WIDTH

    @pl.when(c == 0)
    def _():
        s_ref[...] = s0_ref[...]
        ext_ref[0:SUBLANES, :] = st8_ref[0]

    @pl.when(c != 0)
    def _():
        ext_ref[0:SUBLANES, :] = prev_ref[:, 0:W3]

    ext_ref[SUBLANES:SUBLANES + C, :] = cur_ref[:, 0:W3]
    base = SUBLANES - (CONV_W - 1)
    acc = ext_ref[base:base + C, :] * wconv_ref[0:1, :]
    for i in range(1, CONV_W):
        acc = acc + ext_ref[base + i:base + i + C, :] * wconv_ref[i:i + 1, :]
    qkv = acc * jax.nn.sigmoid(acc)

    bd = bd_ref[...]
    lane = lax.broadcasted_iota(jnp.int32, bd.shape, 1)
    beta_all = jax.nn.sigmoid(bd)
    xg = bd + dtb_ref[...]
    softplus = jnp.maximum(xg, 0.0) + jnp.log1p(jnp.exp(-jnp.abs(xg)))
    g_all = -jnp.exp(alog_ref[...]) * softplus
    gb = jnp.where(lane < H, beta_all, jnp.where(lane < 2 * H, g_all, 0.0))
    if n_valid < C:
        gb = jnp.where(lax.broadcasted_iota(jnp.int32, bd.shape, 0) < n_valid, gb, 0.0)
    ri = lax.broadcasted_iota(jnp.int32, (C, C), 0)
    ci = lax.broadcasted_iota(jnp.int32, (C, C), 1)
    causal = ri >= ci
    strict = ri > ci
    ltri = jnp.where(causal, 1.0, 0.0).astype(BF16)
    gc_cols = sum(jnp.dot(ltri, piece, preferred_element_type=F32) for piece in _split3(gb))
    gc_rows = gc_cols.T
    eye = jnp.where(ri == ci, 1.0, 0.0).astype(F32)

    heads = []
    for h in range(H):
        q = qkv[:, h * GDN_HD:(h + 1) * GDN_HD]
        k = qkv[:, GDN_WIDTH + h * GDN_HD:GDN_WIDTH + (h + 1) * GDN_HD]
        v = qkv[:, 2 * GDN_WIDTH + h * GDN_HD:2 * GDN_WIDTH + (h + 1) * GDN_HD]
        q = q * lax.rsqrt(jnp.sum(q * q, axis=-1, keepdims=True) + L2_EPS) * (GDN_HD ** -0.5)
        k = k * lax.rsqrt(jnp.sum(k * k, axis=-1, keepdims=True) + L2_EPS)
        beta_c = _col(gb, h)
        gc_c = _col(gc_cols, H + h)
        gc_r = _row(gc_rows, H + h)
        g_last = gc_c[C - 1:C, :]
        decay = jnp.where(causal, jnp.exp(jnp.where(causal, gc_c - gc_r, 0.0)), 0.0)
        kb = k * beta_c
        kk = _dotg(jnp.concatenate([kb, q], axis=0), k, _NT, hp)
        a_kk = jnp.where(strict, kk[0:C] * decay, 0.0)
        a_qk = jnp.where(causal, kk[C:2 * C] * decay, 0.0)
        rhs = jnp.concatenate([v * beta_c, kb * jnp.exp(gc_c)], axis=1)
        heads.append(dict(q=q, k=k, gc_c=gc_c, g_last=g_last, a_qk=a_qk, rhs=rhs, nmat=-a_kk, tinv=eye - a_kk))
    for hd in heads:
        hd['ns'] = _split2(hd['nmat'])
    span = 2
    while span < C:
        for hd in heads:
            hd['ns'] = _split2(_dot3s(hd['ns'], hd['ns']))
        for hd in heads:
            hd['tinv'] = hd['tinv'] + _dot3s(_split2(hd['tinv']), hd['ns'])
        span *= 2
    for hd in heads:
        hd['sol'] = _dot3s(_split2(hd['tinv']), _split2(hd['rhs']))
    for h, hd in enumerate(heads):
        sl = slice(h * GDN_HD, (h + 1) * GDN_HD)
        q, k, gc_c, g_last = hd['q'], hd['k'], hd['gc_c'], hd['g_last']
        u = hd['sol'][:, 0:GDN_HD]
        w = hd['sol'][:, GDN_HD:2 * GDN_HD]
        s = s_ref[0, h]
        ws = _dotg(jnp.concatenate([w, q * jnp.exp(gc_c)], axis=0), s, _NN, hp)
        v_new = u - ws[0:C]
        o = ws[C:2 * C] + _dotg(hd['a_qk'], v_new, _NN, hp)
        s_ref[0, h] = s * jnp.exp(g_last) + _dotg(k * jnp.exp(g_last - gc_c), v_new, _TN, hp)
        z = cur_ref[:, W3 + h * GDN_HD:W3 + (h + 1) * GDN_HD]
        o = o * lax.rsqrt(jnp.mean(o * o, axis=-1, keepdims=True) + RMS_EPS) * gnorm_ref[...]
        o = o * (z * jax.nn.sigmoid(z))
        o_ref[:, sl] = o.astype(o_ref.dtype)


def gdn(u_a, u_bd, state8, s0, w_conv, a_log, dt_bias, gdn_norm, batch, seq, chunk, n_valid, hp):
    nc = seq // chunk
    rb = chunk // SUBLANES
    alog = jnp.zeros((1, LANES), F32).at[0, GDN_HEADS:2 * GDN_HEADS].set(a_log.astype(F32))
    dtb = jnp.zeros((1, LANES), F32).at[0, GDN_HEADS:2 * GDN_HEADS].set(dt_bias.astype(F32))
    W3 = 3 * GDN_WIDTH
    return pl.pallas_call(
        functools.partial(_gdn_body, chunk=chunk, n_valid=n_valid, hp=hp),
        grid=(batch, nc),
        in_specs=[
            pl.BlockSpec((chunk, 4 * GDN_WIDTH), lambda b, c: (b * nc + c, 0)),
            pl.BlockSpec((SUBLANES, 4 * GDN_WIDTH), lambda b, c: (jnp.maximum((b * nc + c) * rb - 1, 0), 0)),
            pl.BlockSpec((1, SUBLANES, W3), lambda b, c: (b, 0, 0)),
            pl.BlockSpec((chunk, LANES), lambda b, c: (b * nc + c, 0)),
            pl.BlockSpec((1, GDN_HEADS, GDN_HD, GDN_HD), lambda b, c: (b, 0, 0, 0)),
            pl.BlockSpec((CONV_W, W3), lambda b, c: (0, 0)),
            pl.BlockSpec((1, LANES), lambda b, c: (0, 0)),
            pl.BlockSpec((1, LANES), lambda b, c: (0, 0)),
            pl.BlockSpec((1, GDN_HD), lambda b, c: (0, 0)),
        ],
        out_specs=[
            pl.BlockSpec((chunk, GDN_WIDTH), lambda b, c: (b * nc + c, 0)),
            pl.BlockSpec((1, GDN_HEADS, GDN_HD, GDN_HD), lambda b, c: (b, 0, 0, 0)),
        ],
        out_shape=[
            jax.ShapeDtypeStruct((batch * seq, GDN_WIDTH), F32 if hp else BF16),
            jax.ShapeDtypeStruct((batch, GDN_HEADS, GDN_HD, GDN_HD), F32),
        ],
        scratch_shapes=[pltpu.VMEM((SUBLANES + chunk, W3), F32)],
        compiler_params=_cparams("parallel", "arbitrary"),
        name="gdn",
    )(u_a, u_a, state8, u_bd, s0, w_conv.astype(F32), alog, dtb, gdn_norm.reshape(1, GDN_HD).astype(F32))


def _flash_body(qi_ref, ki_ref, lam_ref, q_ref, k_ref, vt_ref, dn_ref, o_ref, qs_ref, m_ref, acc_ref, *, tq):
    step = pl.program_id(2)
    qi = qi_ref[step]
    ki = ki_ref[step]

    @pl.when(ki == 0)
    def _():
        q = q_ref[...]
        lane = lax.broadcasted_iota(jnp.int32, q.shape, 1)
        zero = jnp.zeros_like(q)
        qs_ref[0:tq, :] = jnp.where(lane < DIFF_HD, q, zero)
        qs_ref[tq:2 * tq, :] = jnp.where(lane >= DIFF_HD, q, zero)
        m_ref[...] = jnp.full(m_ref.shape, NEG_INIT, F32)
        acc_ref[...] = jnp.zeros(acc_ref.shape, F32)

    def update(masked):
        vt1 = jnp.concatenate([vt_ref[...], jnp.ones((ONES_ROWS, vt_ref.shape[1]), BF16)], axis=0)
        s = lax.dot_general(k_ref[...], qs_ref[...], _NT, preferred_element_type=F32)
        if masked:
            key = lax.broadcasted_iota(jnp.int32, s.shape, 0)
            qry = lax.rem(lax.broadcasted_iota(jnp.int32, s.shape, 1), tq)
            s = jnp.where(key <= qry, s, NEG_INIT)
        m_prev = m_ref[...]
        m_new = jnp.maximum(m_prev, jnp.max(s, axis=0, keepdims=True))
        alpha = jnp.exp2(m_prev - m_new)
        p = jnp.exp2(s - m_new).astype(BF16)
        acc_ref[...] = alpha * acc_ref[...] + jnp.dot(vt1, p, preferred_element_type=F32)
        m_ref[...] = m_new

    @pl.when(ki < qi)
    def _():
        update(False)

    @pl.when(ki == qi)
    def _():
        update(True)
        lam = lam_ref[0]
        inv = 1.0 / acc_ref[DIFF_VD:DIFF_VD + 1, :]
        o = (acc_ref[0:DIFF_VD, 0:tq] * inv[:, 0:tq]
             - lam * (acc_ref[0:DIFF_VD, tq:2 * tq] * inv[:, tq:2 * tq]))
        o = o * lax.rsqrt(jnp.mean(o * o, axis=0, keepdims=True) + RMS_EPS)
        o_ref[...] = (o.T * dn_ref[...] * (1.0 - LAMBDA_INIT)).astype(o_ref.dtype)


def diff_attn_prompt(q, k, vt, lam, diff_norm, batch, seq):
    tq = _tile(seq, FLASH_TILE)
    nq = seq // tq
    pairs =[(i, j) for i in range(nq) for j in range(i + 1)]
    qi_tab = jnp.asarray([p[0] for p in pairs], jnp.int32)
    ki_tab = jnp.asarray([p[1] for p in pairs], jnp.int32)
    grid_spec = pltpu.PrefetchScalarGridSpec(
        num_scalar_prefetch=3,
        grid=(batch, DIFF_HEADS, len(pairs)),
        in_specs=[
            pl.BlockSpec((tq, DIFF_VD), lambda b, h, s, qt, kt, lam: (b * nq + qt[s], h)),
            pl.BlockSpec((tq, DIFF_VD), lambda b, h, s, qt, kt, lam: (b * nq + kt[s], h)),
            pl.BlockSpec((DIFF_VD, tq), lambda b, h, s, qt, kt, lam: (h, b * nq + kt[s])),
            pl.BlockSpec((1, DIFF_VD), lambda b, h, s, qt, kt, lam: (0, 0)),
        ],
        out_specs=pl.BlockSpec((tq, DIFF_VD), lambda b, h, s, qt, kt, lam: (b * nq + qt[s], h)),
        scratch_shapes=[
            pltpu.VMEM((2 * tq, DIFF_VD), BF16),
            pltpu.VMEM((1, 2 * tq), F32),
            pltpu.VMEM((DIFF_VD + ONES_ROWS, 2 * tq), F32),
        ],
    )
    return pl.pallas_call(
        functools.partial(_flash_body, tq=tq),
        grid_spec=grid_spec,
        out_shape=jax.ShapeDtypeStruct((batch * seq, DIFF_WIDTH), BF16),
        compiler_params=_cparams("parallel", "parallel", "arbitrary"),
        name="diff_attn_prompt",
    )(qi_tab, ki_tab, lam.reshape(1), q, k, vt, diff_norm.reshape(1, DIFF_VD).astype(F32))


PAGE_ROWS = PAGE_SIZE * DIFF_HEADS


def _head_mask(shape, tp, extra_mask=None):
    r = lax.broadcasted_iota(jnp.int32, shape, 0)
    c = lax.broadcasted_iota(jnp.int32, shape, 1)
    ok = jnp.bitwise_and(c, DIFF_HEADS - 1) == jnp.right_shift(r, (2 * tp).bit_length() - 1)
    if extra_mask is not None:
        ok = jnp.logical_and(ok, extra_mask(r, c))
    return jnp.where(ok, 0.0, NEG_INIT).astype(F32)


def _paged_body(pt_ref, lam_ref, q_ref, *rest, n_tok, pp, tp):
    kp_refs = rest[0:pp]
    vp_refs = rest[pp:2 * pp]
    kn_ref, vn_ref, dn_ref, o_ref, m_ref, l_ref, acc_ref, bias_ref = rest[2 * pp:]
    p = pl.program_id(1)
    npg = pl.num_programs(1)
    kshift = DIFF_HEADS.bit_length() - 1

    @pl.when(p == 0)
    def _():
        m_ref[...] = jnp.full(m_ref.shape, NEG_INIT, F32)
        l_ref[...] = jnp.zeros(l_ref.shape, F32)
        acc_ref[...] = jnp.zeros(acc_ref.shape, F32)
        bias_ref[...] = _head_mask(bias_ref.shape, tp)

    qh, qm = _split2(q_ref[0])
    qcat = jnp.concatenate([qh, qh, qm], axis=1)

    def scores(kb, bias):
        kh, km = _split2(kb)
        kcat = jnp.concatenate([kh, km, kh], axis=1)
        return lax.dot_general(qcat, kcat, _NT, preferred_element_type=F32) + bias

    def weighted_values(pr, vb):
        ph, pm = _split2(pr)
        vh, vm = _split2(vb)
        wide = jnp.dot(ph, jnp.concatenate([vh, vm], axis=1), preferred_element_type=F32)
        return (wide[:, 0:DIFF_VD] + wide[:, DIFF_VD:2 * DIFF_VD]) + jnp.dot(pm, vh, preferred_element_type=F32)

    def update(ss, vbs):
        m_prev = m_ref[...]
        m_new = m_prev
        for s in ss:
            m_new = jnp.maximum(m_new, jnp.max(s, axis=1, keepdims=True))
        alpha = jnp.exp(m_prev - m_new)
        l_new = alpha * l_ref[...]
        acc = alpha * acc_ref[...]
        for s, vb in zip(ss, vbs):
            pr = jnp.exp(s - m_new)
            l_new = l_new + jnp.sum(pr, axis=1, keepdims=True)
            acc = acc + weighted_values(pr, vb)
        l_ref[...] = l_new
        acc_ref[...] = acc
        m_ref[...] = m_new

    bias = bias_ref[...]
    update([scores(kp_refs[j][0], bias) for j in range(pp)], [vp_refs[j][0] for j in range(pp)])

    @pl.when(p == npg - 1)
    def _():
        def causal(r, c):
            tok = jnp.right_shift(c, kshift)
            return jnp.logical_and(tok <= jnp.bitwise_and(r, tp - 1), tok < n_tok)

        update([scores(kn_ref[0], _head_mask((q_ref.shape[1], kn_ref.shape[1]), tp, causal))], [vn_ref[0]])
        lam = lam_ref[0]
        inv = 1.0 / l_ref[...]
        for h in range(DIFF_HEADS):
            r0 = h * 2 * tp
            if tp == SUBLANES:
                o1 = acc_ref[r0:r0 + tp, :] * inv[r0:r0 + tp]
                o2 = acc_ref[r0 + tp:r0 + 2 * tp, :] * inv[r0 + tp:r0 + 2 * tp]
            else:
                o1 = acc_ref[r0:r0 + SUBLANES, :] * inv[r0:r0 + SUBLANES]
                o2 = pltpu.roll(o1, tp, 0)
            o = o1 - lam * o2
            o = o * lax.rsqrt(jnp.mean(o * o, axis=-1, keepdims=True) + RMS_EPS) * dn_ref[...]
            o_ref[0, h * SUBLANES:(h + 1) * SUBLANES, :] = (o * (1.0 - LAMBDA_INIT)).astype(o_ref.dtype)


def diff_attn_sample(q, k_new, v_new, lam, diff_norm, cache_k, cache_v, page_table, n_tok):
    bd = q.shape[0]
    n_pool = cache_k.shape[0]
    n_pages = page_table.shape[1]
    assert n_tok <= SUBLANES
    tp = SUBLANES // 2 if n_tok <= SUBLANES // 2 else SUBLANES
    qrows_n = 2 * DIFF_HEADS * tp
    pp = PAGES_PER_STEP
    while n_pages % pp:
        pp //= 2
    qh = q.reshape(bd, n_tok, DIFF_HEADS, 2, DIFF_HD)
    qh = jnp.pad(qh, ((0, 0), (0, tp - n_tok), (0, 0), (0, 0), (0, 0)))
    qh = jnp.transpose(qh, (0, 2, 3, 1, 4))
    zeros = jnp.zeros_like(qh[:, :, 0])
    qrows = jnp.stack([jnp.concatenate([qh[:, :, 0], zeros], axis=-1),
                       jnp.concatenate([zeros, qh[:, :, 1]], axis=-1)], axis=2).reshape(bd, qrows_n, DIFF_VD)
    new_rows = tp * DIFF_HEADS
    out_rows = SUBLANES * DIFF_HEADS
    padn = ((0, 0), (0, new_rows - n_tok * DIFF_HEADS), (0, 0))
    kn = jnp.pad(k_new.reshape(bd, n_tok * DIFF_HEADS, DIFF_VD), padn)
    vn = jnp.pad(v_new.reshape(bd, n_tok * DIFF_HEADS, DIFF_VD), padn)
    ck = cache_k.reshape(n_pool, PAGE_ROWS, DIFF_VD)
    cv = cache_v.reshape(n_pool, PAGE_ROWS, DIFF_VD)

    def page_spec(j):
        return pl.BlockSpec((1, PAGE_ROWS, DIFF_VD), lambda b, p, pt, lam: (pt[b, p * pp + j], 0, 0))

    grid_spec = pltpu.PrefetchScalarGridSpec(
        num_scalar_prefetch=2,
        grid=(bd, n_pages // pp),
        in_specs=(
            [pl.BlockSpec((1, qrows_n, DIFF_VD), lambda b, p, pt, lam: (b, 0, 0))]
            + [page_spec(j) for j in range(pp)]
            + [page_spec(j) for j in range(pp)]
            + [pl.BlockSpec((1, new_rows, DIFF_VD), lambda b, p, pt, lam: (b, 0, 0)),
               pl.BlockSpec((1, new_rows, DIFF_VD), lambda b, p, pt, lam: (b, 0, 0)),
               pl.BlockSpec((1, DIFF_VD), lambda b, p, pt, lam: (0, 0))]
        ),
        out_specs=pl.BlockSpec((1, out_rows, DIFF_VD), lambda b, p, pt, lam: (b, 0, 0)),
        scratch_shapes=[
            pltpu.VMEM((qrows_n, 1), F32),
            pltpu.VMEM((qrows_n, 1), F32),
            pltpu.VMEM((qrows_n, DIFF_VD), F32),
            pltpu.VMEM((qrows_n, PAGE_ROWS), F32),
        ],
    )
    out = pl.pallas_call(
        functools.partial(_paged_body, n_tok=n_tok, pp=pp, tp=tp),
        grid_spec=grid_spec,
        out_shape=jax.ShapeDtypeStruct((bd, out_rows, DIFF_VD), F32),
        compiler_params=_cparams("parallel", "arbitrary"),
        cost_estimate=pl.CostEstimate(
            flops=2 * 5 * bd * n_pages * qrows_n * PAGE_ROWS * DIFF_VD,
            transcendentals=bd * n_pages * qrows_n * PAGE_ROWS,
            bytes_accessed=2 * bd * n_pages * PAGE_ROWS * DIFF_VD * 4),
        name="diff_attn_sample",
    )(page_table, lam.reshape(1), qrows, *([ck] * pp), *([cv] * pp), kn, vn,
      diff_norm.reshape(1, DIFF_VD).astype(F32))
    out = out.reshape(bd, DIFF_HEADS, SUBLANES, DIFF_VD)[:, :, :n_tok]
    return jnp.transpose(out, (0, 2, 1, 3)).reshape(bd, n_tok, DIFF_WIDTH)


def _mem_body(q_ref, k_ref, v_ref, o_ref):
    hp = q_ref.dtype == F32
    for h in range(MEM_HEADS):
        sl = slice(h * MEM_HD, (h + 1) * MEM_HD)
        s = _dotg(q_ref[0, :, sl], k_ref[0, :, sl], _NT, hp) * (MEM_HD ** -0.5)
        s = s - jnp.max(s, axis=1, keepdims=True)
        e = jnp.exp(s)
        p = e / jnp.sum(e, axis=1, keepdims=True)
        o_ref[0, :, sl] = _dotg(p, v_ref[0, :, sl], _NN, hp).astype(o_ref.dtype)


def mem_attn(q, mem_k, mem_v):
    b, t, _ = q.shape
    mlen = mem_k.shape[1]
    tq = _tile(t, 512)
    return pl.pallas_call(
        _mem_body,
        grid=(b, t // tq),
        in_specs=[
            pl.BlockSpec((1, tq, MEM_WIDTH), lambda i, j: (i, j, 0)),
            pl.BlockSpec((1, mlen, MEM_WIDTH), lambda i, j: (i, 0, 0)),
            pl.BlockSpec((1, mlen, MEM_WIDTH), lambda i, j: (i, 0, 0)),
        ],
        out_specs=pl.BlockSpec((1, tq, MEM_WIDTH), lambda i, j: (i, j, 0)),
        out_shape=jax.ShapeDtypeStruct((b, t, MEM_WIDTH), q.dtype),
        compiler_params=_cparams("parallel", "arbitrary"),
        name="mem_attn",
    )(q, mem_k, mem_v)


def _norm_router_body(x_ref, g_ref, wr_ref, br_ref, h_ref, rt_ref):
    x = x_ref[...]
    hn = x * lax.rsqrt(jnp.mean(x * x, axis=-1, keepdims=True) + RMS_EPS) * g_ref[...]
    h_ref[...] = hn.astype(h_ref.dtype)
    lg = _dot3(hn, wr_ref[...]) + br_ref[...]
    lane = lax.broadcasted_iota(jnp.int32, lg.shape, 1)
    lanef = lane.astype(F32)
    big = float(LANES)

    def first_argmax(vals, mask):
        top = jnp.max(jnp.where(mask, vals, NEG_INIT), axis=1, keepdims=True)
        idx = jnp.min(jnp.where(jnp.logical_and(mask, vals == top), lanef, big), axis=1, keepdims=True)
        return top, idx

    gmask = lane < N_GROUPS
    gtop, gidx = first_argmax(lg, gmask)
    g_w = 1.0 / jnp.sum(jnp.where(gmask, jnp.exp(lg - gtop), 0.0), axis=1, keepdims=True)
    first = N_GROUPS + EXPERTS_PER_GROUP * gidx.astype(jnp.int32)
    emask = jnp.logical_and(lane >= first, lane < first + EXPERTS_PER_GROUP)
    e1, i1 = first_argmax(lg, emask)
    e2, i2 = first_argmax(lg, jnp.logical_and(emask, lanef != i1))
    ez = jnp.sum(jnp.where(emask, jnp.exp(lg - e1), 0.0), axis=1, keepdims=True)
    p1 = 1.0 / ez
    p2 = jnp.exp(e2 - e1) / ez
    psum = p1 + p2
    rt_ref[...] = jnp.where(lane == 0, i1 - N_GROUPS,
                            jnp.where(lane == 1, i2 - N_GROUPS,
                                      jnp.where(lane == 2, g_w * (p1 / psum),
                                                jnp.where(lane == 3, g_w * (p2 / psum), 0.0))))


def norm_router(x, g, w_router, b_router):
    m, d = x.shape
    tm = _tile(m, 512)
    return pl.pallas_call(
        _norm_router_body,
        grid=(m // tm,),
        in_specs=[
            pl.BlockSpec((tm, d), lambda i: (i, 0)),
            pl.BlockSpec((1, d), lambda i: (0, 0)),
            pl.BlockSpec((d, LANES), lambda i: (0, 0)),
            pl.BlockSpec((1, LANES), lambda i: (0, 0)),
        ],
        out_specs=[pl.BlockSpec((tm, d), lambda i: (i, 0)), pl.BlockSpec((tm, LANES), lambda i: (i, 0))],
        out_shape=[jax.ShapeDtypeStruct((m, d), BF16), jax.ShapeDtypeStruct((m, LANES), F32)],
        compiler_params=_cparams("parallel"),
        name="norm_router",
    )(x, g.reshape(1, d).astype(F32), w_router, b_router)


def _expert_body(te_ref, tv_ref, after_ref, x_ref, wg_ref, wu_ref, wd_ref, o_ref, wg16, wu16, wd16):
    del after_ref
    i = pl.program_id(0)
    new_expert = jnp.logical_or(i == 0, te_ref[i] != te_ref[jnp.maximum(i - 1, 0)])

    @pl.when(jnp.logical_and(tv_ref[i] != 0, new_expert))
    def _():
        wg16[...] = wg_ref[0].astype(BF16)
        wu16[...] = wu_ref[0].astype(BF16)
        wd16[...] = wd_ref[0].astype(BF16)

    @pl.when(tv_ref[i] != 0)
    def _():
        x = x_ref[...].astype(BF16)
        a = jnp.dot(x, wg16[...], preferred_element_type=F32)
        b = jnp.dot(x, wu16[...], preferred_element_type=F32)
        hid = (a * jax.nn.sigmoid(a)) * b
        o_ref[...] = jnp.dot(hid.astype(BF16), wd16[...], preferred_element_type=F32)

    @pl.when(tv_ref[i] == 0)
    def _():
        o_ref[...] = jnp.zeros(o_ref.shape, o_ref.dtype)


def grouped_experts(xs, tile_expert, tile_valid, after, w_gate, w_up, w_down, tm):
    rows, d = xs.shape
    nt = rows // tm
    ff = w_gate.shape[2]
    grid_spec = pltpu.PrefetchScalarGridSpec(
        num_scalar_prefetch=3,
        grid=(nt,),
        in_specs=[
            pl.BlockSpec((tm, d), lambda i, te, tv, af: (i, 0)),
            pl.BlockSpec((1, d, ff), lambda i, te, tv, af: (te[i], 0, 0)),
            pl.BlockSpec((1, d, ff), lambda i, te, tv, af: (te[i], 0, 0)),
            pl.BlockSpec((1, ff, d), lambda i, te, tv, af: (te[i], 0, 0)),
        ],
        out_specs=pl.BlockSpec((tm, d), lambda i, te, tv, af: (i, 0)),
        scratch_shapes=[pltpu.VMEM((d, ff), BF16), pltpu.VMEM((d, ff), BF16), pltpu.VMEM((ff, d), BF16)],
    )
    return pl.pallas_call(
        _expert_body,
        grid_spec=grid_spec,
        out_shape=jax.ShapeDtypeStruct((rows, d), F32),
        compiler_params=_cparams("arbitrary"),
        name="grouped_experts",
    )(tile_expert, tile_valid, after, xs, w_gate, w_up, w_down)


def _final_body(x_ref, ya_ref, yb_ref, rt_ref, g_ref, o_ref):
    rt = rt_ref[...]
    x = x_ref[...] + (_col(rt, 2) * ya_ref[...] + _col(rt, 3) * yb_ref[...])
    o_ref[...] = x * lax.rsqrt(jnp.mean(x * x, axis=-1, keepdims=True) + RMS_EPS) * g_ref[...]


def final_norm(x, ya, yb, row0, route, g):
    m, d = x.shape
    tm = _tile(m, 512)
    assert row0 % tm == 0
    blk0 = row0 // tm
    spec = pl.BlockSpec((tm, d), lambda i: (i, 0))
    yspec = pl.BlockSpec((tm, d), lambda i: (i + blk0, 0))
    return pl.pallas_call(
        _final_body,
        grid=(m // tm,),
        in_specs=[spec, yspec, yspec, pl.BlockSpec((tm, LANES), lambda i: (i, 0)),
                  pl.BlockSpec((1, d), lambda i: (0, 0))],
        out_specs=spec,
        out_shape=jax.ShapeDtypeStruct((m, d), F32),
        compiler_params=_cparams("parallel"),
        name="final_norm",
    )(x, ya, yb, route, g.reshape(1, d).astype(F32))


MOE_TILE = 256


def moe_and_final(x_list, norm_ffn, w_rg, b_rg, w_re, b_re, w_gate, w_up, w_down, norm_final, after=None):
    if after is None:
        after = jnp.zeros((1,), jnp.int32)
    else:
        after = (after.reshape(-1)[0:1] > jnp.finfo(F32).max).astype(jnp.int32)
    d = x_list[0].shape[1]
    w_router = jnp.zeros((d, LANES), F32).at[:, 0:N_GROUPS].set(w_rg.astype(F32))
    w_router = w_router.at[:, N_GROUPS:N_GROUPS + N_EXPERTS].set(w_re.astype(F32))
    b_router = jnp.zeros((1, LANES), F32).at[0, 0:N_GROUPS].set(b_rg.astype(F32))
    b_router = b_router.at[0, N_GROUPS:N_GROUPS + N_EXPERTS].set(b_re.astype(F32))
    hs, rts = [], []
    for x in x_list:
        hn, rt = norm_router(x, norm_ffn, w_router, b_router)
        hs.append(hn)
        rts.append(rt)
    hn = jnp.concatenate(hs, axis=0)
    n = hn.shape[0]
    expert_id = jnp.concatenate([rt[:, 0:TOP_K_INNER] for rt in rts], axis=0).astype(jnp.int32)
    tm = min(MOE_TILE, max(SUBLANES * 2, n * TOP_K_INNER // N_EXPERTS * 4))
    flat_e = expert_id.reshape(-1)
    na = flat_e.shape[0]
    onehot = (flat_e[:, None] == jnp.arange(N_EXPERTS, dtype=jnp.int32)[None, :]).astype(jnp.int32)
    csum = jnp.cumsum(onehot, axis=0)
    counts = csum[-1]
    padded = ((counts + tm - 1) // tm) * tm
    pad_end = jnp.cumsum(padded)
    pad_start = pad_end - padded
    dest = jnp.sum(onehot * (csum - 1 + pad_start[None, :]), axis=1)
    nt = (na + tm - 1) // tm + N_EXPERTS
    rows = nt * tm
    row_tok = jnp.zeros((rows,), jnp.int32).at[dest].set(jnp.arange(na, dtype=jnp.int32) // TOP_K_INNER)
    tile_start = jnp.arange(nt, dtype=jnp.int32) * tm
    tile_expert = jnp.minimum(jnp.sum((pad_end[None, :] <= tile_start[:, None]).astype(jnp.int32), axis=1),
                              N_EXPERTS - 1)
    tile_valid = (tile_start < pad_end[-1]).astype(jnp.int32)
    xs = hn.at[row_tok].get(mode='promise_in_bounds')
    ys = grouped_experts(xs, tile_expert, tile_valid, after, w_gate, w_up, w_down, tm)
    dest2 = dest.reshape(n, TOP_K_INNER)
    ya = ys.at[dest2[:, 0]].get(mode='promise_in_bounds')
    yb = ys.at[dest2[:, 1]].get(mode='promise_in_bounds')
    outs, o = [], 0
    for x, rt in zip(x_list, rts):
        outs.append(final_norm(x, ya, yb, o, rt, norm_final))
        o += x.shape[0]
    return outs


def _rope_tables(pos, rows):
    half = ROT_DIM // 2
    inv_freq = ROPE_THETA ** (-jnp.arange(0, ROT_DIM, 2, dtype=F32) / ROT_DIM)
    ang = pos.astype(F32)[:, None] * inv_freq[None, :]
    cos, sin = jnp.cos(ang), jnp.sin(ang)
    t = pos.shape[0]
    one = jnp.ones((t, DIFF_HD - ROT_DIM), F32)
    zero = jnp.zeros((t, DIFF_HD - ROT_DIM), F32)
    zh = jnp.zeros((t, half), F32)
    c = jnp.concatenate([cos, cos, one], axis=1)
    s1 = jnp.concatenate([-sin, zh, zero], axis=1)
    s2 = jnp.concatenate([zh, sin, zero], axis=1)
    rep = rows // t
    tile = lambda a: jnp.tile(jnp.concatenate([a, a], axis=1), (rep, 1))
    return tile(c), tile(s1), tile(s2)


def _prep_weights(p, dtype):
    w_in = p['w_in']
    o = 4 * GDN_WIDTH
    ob = o + 2 * GDN_HEADS
    d = w_in.shape[0]
    w = {}
    w['a'] = w_in[:, 0:o].astype(dtype)
    w['bd'] = jnp.zeros((d, LANES), dtype).at[:, 0:2 * GDN_HEADS].set(w_in[:, o:ob].astype(dtype))
    w['dq'] = w_in[:, ob:ob + DIFF_WIDTH].astype(dtype)
    w['dk'] = w_in[:, ob + DIFF_WIDTH:ob + 2 * DIFF_WIDTH].astype(dtype)
    w['dv'] = w_in[:, ob + 2 * DIFF_WIDTH:ob + 3 * DIFF_WIDTH].astype(dtype)
    w['g'] = w_in[:, ob + 3 * DIFF_WIDTH:].astype(dtype)
    for name in ('w_branch_a', 'w_branch_b', 'w_out', 'w_mq', 'w_mk', 'w_mv', 'w_mo'):
        w[name] = p[name].astype(dtype)
    return w


def _mixers(x, batch, seq, pos, conv_state, delta_state, mem_k, mem_v, p, w, lam, sample_ctx):
    m, d = x.shape
    hp = sample_ctx is not None
    od = F32 if hp else BF16
    h = rmsnorm_rows(x, p['norm_mix'], out_dtype=od)
    u_a = matmul(h, w['a'])
    u_bd = matmul(h, w['bd'])
    u_g = matmul(h, w['g'])
    tm = _tile(m, 1024)
    tables = _rope_tables(pos, max(tm, seq))
    d_v = matmul(h, w['dv'])
    d_k = matmul(h, w['dk'], rope=tables)
    chunk = min(GDN_CHUNK, seq)
    seq_pad = seq
    if chunk % SUBLANES:
        chunk = SUBLANES
        seq_pad = SUBLANES
        padr = lambda a: jnp.pad(a.reshape(batch, seq, -1), ((0, 0), (0, seq_pad - seq), (0, 0))).reshape(
            batch * seq_pad, -1)
        u_a_g, u_bd_g = padr(u_a), padr(u_bd)
    else:
        u_a_g, u_bd_g = u_a, u_bd
    state8 = jnp.pad(conv_state.astype(F32), ((0, 0), (SUBLANES - (CONV_W - 1), 0), (0, 0)))
    o_a, new_delta = gdn(u_a_g, u_bd_g, state8, delta_state.astype(F32), p['w_conv'], p['a_log'], p['dt_bias'],
                         p['gdn_norm'], batch, seq_pad, chunk, min(seq, chunk), hp)
    if seq_pad != seq:
        o_a = o_a.reshape(batch, seq_pad, GDN_WIDTH)[:, :seq].reshape(m, GDN_WIDTH)
    keep = min(seq, CONV_W - 1)
    tail = u_a.reshape(batch, seq, -1)[:, seq - keep:, 0:3 * GDN_WIDTH]
    new_conv = jnp.concatenate([conv_state.astype(F32)[:, keep:], tail], axis=1)
    if sample_ctx is None:
        d_q = matmul(h, w['dq'], out_dtype=BF16, rope=tables, scale=DIFF_HD ** -0.5 * LOG2E)
        o_b = diff_attn_prompt(d_q, d_k.astype(BF16), d_v.astype(BF16).T, lam, p['diff_norm'], batch, seq)
    else:
        cache_k, cache_v, page_table = sample_ctx
        d_q = matmul(h, w['dq'], rope=tables, scale=DIFF_HD ** -0.5)
        o_b = diff_attn_sample(d_q.reshape(batch, seq, -1), d_k.reshape(batch, seq, -1), d_v.reshape(batch, seq, -1),
                               lam, p['diff_norm'], cache_k, cache_v, page_table, seq)
        o_b = o_b[:, :seq].reshape(m, DIFF_WIDTH)
    mixed = merge_branches(o_a, o_b, w['w_branch_a'], w['w_branch_b'], u_g)
    x = matmul(mixed, w['w_out'], residual=x)
    hc = rmsnorm_rows(x, p['norm_cross'], out_dtype=od)
    mq = matmul(hc, w['w_mq'], out_dtype=od).reshape(batch, seq, MEM_WIDTH)
    if seq % SUBLANES:
        mq = jnp.pad(mq, ((0, 0), (0, SUBLANES - seq), (0, 0)))
    mo = mem_attn(mq, mem_k, mem_v)[:, :seq].reshape(m, MEM_WIDTH)
    x = matmul(mo, w['w_mo'], residual=x)
    return x, new_conv, new_delta, d_k, d_v, o_b


def kernel(x_prompt, x_sample, cache_k, cache_v, cache_mem_k, cache_mem_v, state_delta, state_conv, page_table, mem_prompt, norm_mix, w_in, w_conv, a_log, dt_bias, gdn_norm, lambda_q1, lambda_k1, lambda_q2, lambda_k2, diff_norm, w_branch_a, w_branch_b, w_out, norm_cross, norm_mem, w_mq, w_mk, w_mv, w_mo, norm_ffn, w_router_group, b_router_group, w_router_expert, b_router_expert, w_gate, w_up, w_down, norm_final):
    p = dict(norm_mix=norm_mix, w_in=w_in, w_conv=w_conv, a_log=a_log, dt_bias=dt_bias, gdn_norm=gdn_norm,
             diff_norm=diff_norm, w_branch_a=w_branch_a, w_branch_b=w_branch_b, w_out=w_out,
             norm_cross=norm_cross, w_mq=w_mq, w_mk=w_mk, w_mv=w_mv, w_mo=w_mo)
    bp, tp, d = x_prompt.shape
    bs, ts, _ = x_sample.shape
    past_len = page_table.shape[1] * PAGE_SIZE
    w = _prep_weights(p, BF16)
    w_hp = _prep_weights(p, F32)
    lam =(jnp.exp(jnp.sum(lambda_q1.astype(F32) * lambda_k1.astype(F32)))
           - jnp.exp(jnp.sum(lambda_q2.astype(F32) * lambda_k2.astype(F32))) + LAMBDA_INIT)
    mlen = mem_prompt.shape[1]
    hm = rmsnorm_rows(mem_prompt.reshape(bp * mlen, d), norm_mem)
    mem_k_p = matmul(hm, w['w_mk'])
    mem_v_p = matmul(hm, w['w_mv'])
    conv0 = jnp.zeros((bp, CONV_W - 1, 3 * GDN_WIDTH), F32)
    delta0 = jnp.zeros((bp, GDN_HEADS, GDN_HD, GDN_HD), F32)
    xp, conv_p, delta_p, k_p, v_p, _ = _mixers(
        x_prompt.reshape(bp * tp, d), bp, tp, jnp.arange(tp), conv0, delta0,
        mem_k_p.astype(BF16).reshape(bp, mlen, MEM_WIDTH), mem_v_p.astype(BF16).reshape(bp, mlen, MEM_WIDTH),
        p, w, lam, None)
    xs, conv_s, delta_s, k_s, v_s, attn_s = _mixers(
        x_sample.reshape(bs * ts, d), bs, ts, past_len + jnp.arange(ts), state_conv, state_delta,
        cache_mem_k.astype(F32).reshape(bs, -1, MEM_WIDTH), cache_mem_v.astype(F32).reshape(bs, -1, MEM_WIDTH),
        p, w_hp, lam, (cache_k, cache_v, page_table))
    moe_w = (norm_ffn, w_router_group, b_router_group, w_router_expert, b_router_expert, w_gate, w_up, w_down,
             norm_final)
    yp, = moe_and_final([xp], *moe_w, after=attn_s)
    ys, = moe_and_final([xs], *moe_w)
    return (yp.reshape(bp, tp, d), ys.reshape(bs, ts, d),
            k_p.reshape(bp, tp, DIFF_HEADS, DIFF_VD), v_p.reshape(bp, tp, DIFF_HEADS, DIFF_VD),
            mem_k_p.reshape(bp, mlen, MEM_HEADS, MEM_HD), mem_v_p.reshape(bp, mlen, MEM_HEADS, MEM_HD),
            delta_p.astype(state_delta.dtype), conv_p.astype(x_prompt.dtype),
            k_s.reshape(bs, ts, DIFF_HEADS, DIFF_VD), v_s.reshape(bs, ts, DIFF_HEADS, DIFF_VD),
            delta_s.astype(state_delta.dtype), conv_s.astype(state_conv.dtype))
```
